```python
import math
import jax
import jax.numpy as jnp
from jax import lax
import numpy as np

D_MODEL = 1024
BATCH = 8
SEQ = 8192
DEPTH = 2

GRID_W = 64
CTX_LEN = 256
N_EVEN = (DEPTH + 1) // 2
N_ODD = DEPTH // 2
N_MOD = 9
D_FF = 2816
FFN_RES = 0.5
EPS = 1e-6
ROPE_BASE = 10000.0
Q_BLOCK = 128
NEG_INF = -1e30

MLA_HEADS = 8
MLA_Q_LORA = 256
MLA_KV_LORA = 128
MLA_NOPE = 64
MLA_ROPE = 32
MLA_V = 64
MLA_SCALE = (MLA_NOPE + MLA_ROPE) ** -0.5

SWA_HEADS = 8
SWA_KV_HEADS = 2
SWA_GROUP = SWA_HEADS // SWA_KV_HEADS
SWA_HEAD_DIM = 64
SWA_WINDOW = 128
SWA_SCALE = SWA_HEAD_DIM ** -0.5

ATTN_IN = MLA_Q_LORA + MLA_KV_LORA + MLA_ROPE + (SWA_HEADS + 2 * SWA_KV_HEADS) * SWA_HEAD_DIM
ATTN_OUT = MLA_HEADS * MLA_V + SWA_HEADS * SWA_HEAD_DIM

S5_WIDTH = D_MODEL
S5_GROUP = 16
S5_GROUPS = S5_WIDTH // S5_GROUP
S5_STATE = 64
S5_CHUNK = 128
S5_MAX_RE = -1e-4

kernel_name = 'hybrid_mla_swa_s5_macaron_dit'


def rmsnorm(x, g):
    xf = x.astype(jnp.float32)
    y = xf * lax.rsqrt(jnp.mean(xf * xf, axis=-1, keepdims=True) + EPS)
    return (y * g.astype(jnp.float32)).astype(x.dtype)


def modulate(h, g, shift, scale):
    return rmsnorm(h, g) * (1 + scale) + shift


def swiglu(h, w13, w2):
    gate, up = jnp.split(h @ w13, 2, axis=-1)
    return (jax.nn.silu(gate) * up) @ w2


def ffn_sublayer(h, m, j, g_pre, g_post, w13, w2):
    a = modulate(h, g_pre, m[3 * j], m[3 * j + 1])
    return h + FFN_RES * m[3 * j + 2] * rmsnorm(swiglu(a, w13, w2), g_post)


def axial_rope_table(rows, cols, d_rot):
    d_axis = d_rot // 2
    inv = ROPE_BASE ** (-jnp.arange(0, d_axis, 2, dtype=jnp.float32) / d_axis)
    ang = jnp.concatenate([rows.astype(jnp.float32)[:, None] * inv,
                           cols.astype(jnp.float32)[:, None] * inv], axis=-1)
    return jnp.cos(ang), jnp.sin(ang)


def apply_rope(x, table):
    cos, sin = table
    bshape = cos.shape[:1] + (1,) * (x.ndim - 3) + cos.shape[1:]
    cos = cos.reshape(bshape).astype(x.dtype)
    sin = sin.reshape(bshape).astype(x.dtype)
    x1, x2 = jnp.split(x, 2, axis=-1)
    return jnp.concatenate([x1 * cos - x2 * sin, x1 * sin + x2 * cos], axis=-1)


def to_blocks(t):
    b, s = t.shape[:2]
    return jnp.moveaxis(t.reshape((b, s // Q_BLOCK, Q_BLOCK) + t.shape[2:]), 1, 0)


def from_blocks(t):
    t = jnp.moveaxis(t, 0, 1)
    return t.reshape((t.shape[0], t.shape[1] * t.shape[2]) + t.shape[3:])


def softmax_attention(q, k, v, scale):
    s = jnp.einsum('bqhd,bkhd->bhqk', q, k, preferred_element_type=jnp.float32) * scale
    p = jax.nn.softmax(s, axis=-1).astype(v.dtype)
    return jnp.einsum('bhqk,bkhd->bqhd', p, v)


def sink_attention(q, k, v, sink, mask):
    s = jnp.einsum('bqkgd,bjkd->bkgqj', q, k, preferred_element_type=jnp.float32) * SWA_SCALE
    if mask is not None:
        s = jnp.where(mask, s, NEG_INF)
    sink_col = jnp.broadcast_to(sink.astype(jnp.float32)[None, :, :, None, None], s.shape[:-1] + (1,))
    p = jax.nn.softmax(jnp.concatenate([sink_col, s], axis=-1), axis=-1)[..., 1:]
    return jnp.einsum('bkgqj,bjkd->bqkgd', p.astype(v.dtype), v)


def attn_project(h, w_in, q_norm, w_uq, kv_norm, w_ukv, rope_a, rope_b):
    b, n = h.shape[:2]
    sizes = [MLA_Q_LORA, MLA_KV_LORA, MLA_ROPE, SWA_HEADS * SWA_HEAD_DIM, SWA_KV_HEADS * SWA_HEAD_DIM]
    cuts = [int(v) for v in np.cumsum(sizes)]
    cq, ckv, kpe, qs, ks, vs = jnp.split(h @ w_in, cuts, axis=-1)
    q = (rmsnorm(cq, q_norm) @ w_uq).reshape(b, n, MLA_HEADS, MLA_NOPE + MLA_ROPE)
    kv = (rmsnorm(ckv, kv_norm) @ w_ukv).reshape(b, n, MLA_HEADS, MLA_NOPE + MLA_V)
    q_nope, q_pe = q[..., :MLA_NOPE], q[..., MLA_NOPE:]
    k_nope, v_a = kv[..., :MLA_NOPE], kv[..., MLA_NOPE:]
    k_pe = kpe.reshape(b, n, 1, MLA_ROPE)
    q_s = qs.reshape(b, n, SWA_KV_HEADS, SWA_GROUP, SWA_HEAD_DIM)
    k_s = ks.reshape(b, n, SWA_KV_HEADS, SWA_HEAD_DIM)
    v_s = vs.reshape(b, n, SWA_KV_HEADS, SWA_HEAD_DIM)
    if rope_a is not None:
        q_pe = apply_rope(q_pe, rope_a)
        k_pe = apply_rope(k_pe, rope_a)
        q_s = apply_rope(q_s, rope_b)
        k_s = apply_rope(k_s, rope_b)
    q_a = jnp.concatenate([q_nope, q_pe], axis=-1)
    k_a = jnp.concatenate([k_nope, jnp.broadcast_to(k_pe, (b, n, MLA_HEADS, MLA_ROPE))], axis=-1)
    return q_a, k_a, v_a, q_s, k_s, v_s


def attn_mixer(h_ctx, h_lat, rope_a, rope_b, w_in, q_norm, w_uq, kv_norm, w_ukv, sink, w_out, ctx_out):
    qa_c, ka_c, va_c, qs_c, ks_c, vs_c = attn_project(h_ctx, w_in, q_norm, w_uq, kv_norm, w_ukv, None, None)
    qa_l, ka_l, va_l, qs_l, ks_l, vs_l = attn_project(h_lat, w_in, q_norm, w_uq, kv_norm, w_ukv, rope_a, rope_b)
    b, n = h_lat.shape[:2]
    n_ctx = h_ctx.shape[1]
    sink_kg = sink.reshape(SWA_KV_HEADS, SWA_GROUP)
    k_all = jnp.concatenate([ka_c, ka_l], axis=1)
    v_all = jnp.concatenate([va_c, va_l], axis=1)
    o_a = from_blocks(lax.map(lambda qb: softmax_attention(qb, k_all, v_all, MLA_SCALE), to_blocks(qa_l)))
    pad = ((0, 0), (SWA_WINDOW, SWA_WINDOW), (0, 0), (0, 0))
    k_pad = jnp.pad(ks_l, pad)
    v_pad = jnp.pad(vs_l, pad)
    span = Q_BLOCK + 2 * SWA_WINDOW

    def swa_block(args):
        i, qb = args
        start = i * Q_BLOCK
        k_loc = lax.dynamic_slice_in_dim(k_pad, start, span, axis=1)
        v_loc = lax.dynamic_slice_in_dim(v_pad, start, span, axis=1)
        q_pos = start + jnp.arange(Q_BLOCK)
        k_pos = start - SWA_WINDOW + jnp.arange(span)
        band = (jnp.abs(q_pos[:, None] - k_pos[None, :]) <= SWA_WINDOW) & (k_pos >= 0)[None, :] & (k_pos < n)[None, :]
        mask = jnp.concatenate([jnp.ones((Q_BLOCK, n_ctx), dtype=bool), band], axis=1)
        return sink_attention(qb, jnp.concatenate([ks_c, k_loc], axis=1),
                              jnp.concatenate([vs_c, v_loc], axis=1), sink_kg, mask)

    o_b = from_blocks(lax.map(swa_block, (jnp.arange(n // Q_BLOCK), to_blocks(qs_l))))
    y_lat = jnp.concatenate([o_a.reshape(b, n, -1), o_b.reshape(b, n, -1)], axis=-1) @ w_out
    y_ctx = None
    if ctx_out:
        o_a_c = softmax_attention(qa_c, ka_c, va_c, MLA_SCALE)
        o_b_c = sink_attention(qs_c, ks_c, vs_c, sink_kg, None)
        y_ctx = jnp.concatenate([o_a_c.reshape(b, n_ctx, -1), o_b_c.reshape(b, n_ctx, -1)], axis=-1) @ w_out
    return y_ctx, y_lat


def s5_discretise(lam_re, lam_im, b_re, b_im, log_step):
    lam_re = jnp.minimum(lam_re, S5_MAX_RE)
    dt = jnp.exp(log_step)[:, None]
    mag = jnp.exp(lam_re * dt)
    a_re = mag * jnp.cos(lam_im * dt)
    a_im = mag * jnp.sin(lam_im * dt)
    den = lam_re * lam_re + lam_im * lam_im
    f_re = ((a_re - 1.0) * lam_re + a_im * lam_im) / den
    f_im = (a_im * lam_re - (a_re - 1.0) * lam_im) / den
    bb_re = f_re[..., None] * b_re - f_im[..., None] * b_im
    bb_im = f_re[..., None] * b_im + f_im[..., None] * b_re
    return a_re, a_im, bb_re, bb_im


def complex_affine_combine(e1, e2):
    a1r, a1i, b1r, b1i = e1
    a2r, a2i, b2r, b2i = e2
    return (a2r * a1r - a2i * a1i, a2r * a1i + a2i * a1r,
            a2r * b1r - a2i * b1i + b2r, a2r * b1i + a2i * b1r + b2i)


def s5_scan(u, a_re, a_im, bb_re, bb_im, c_re, c_im, h_re, h_im, emit):
    b, n = u.shape[:2]
    chunks = jnp.moveaxis(u.reshape(b, n // S5_CHUNK, S5_CHUNK, S5_GROUPS, S5_GROUP), 1, 0)

    def step(carry, u_c):
        hr, hi = carry
        bu_re = jnp.einsum('blgh,gph->blgp', u_c, bb_re)
        bu_im = jnp.einsum('blgh,gph->blgp', u_c, bb_im)
        ar = jnp.broadcast_to(a_re, bu_re.shape)
        ai = jnp.broadcast_to(a_im, bu_re.shape)
        pa_re, pa_im, s_re, s_im = lax.associative_scan(complex_affine_combine, (ar, ai, bu_re, bu_im), axis=1)
        x_re = s_re + pa_re * hr[:, None] - pa_im * hi[:, None]
        x_im = s_im + pa_re * hi[:, None] + pa_im * hr[:, None]
        new = (x_re[:, -1], x_im[:, -1])
        if not emit:
            return new, None
        y = jnp.einsum('blgp,ghp->blgh', x_re, c_re) - jnp.einsum('blgp,ghp->blgh', x_im, c_im)
        return new, y

    h_end, ys = lax.scan(step, (h_re, h_im), chunks)
    y = jnp.moveaxis(ys, 0, 1).reshape(b, n, S5_WIDTH) if emit else None
    return y, h_end


def s5_mixer(h_ctx, h_lat, w_in, lam_re, lam_im, b_re, b_im, c_re, c_im, log_step, d_skip, w_glu, ctx_out):
    f32 = jnp.float32
    b = h_lat.shape[0]
    u_ctx = (h_ctx @ w_in).astype(f32)
    u_lat = (h_lat @ w_in).astype(f32)
    dsk = d_skip.astype(f32)
    y_lat = u_lat * dsk
    y_ctx = u_ctx * dsk if ctx_out else None
    for dr in range(2):
        a_re, a_im, bb_re, bb_im = s5_discretise(lam_re[dr].astype(f32), lam_im[dr].astype(f32),
                                                 b_re[dr].astype(f32), b_im[dr].astype(f32),
                                                 log_step[dr].astype(f32))
        cr = c_re[dr].astype(f32)
        ci = c_im[dr].astype(f32)
        uc = u_ctx.reshape(b, -1, S5_GROUPS, S5_GROUP)
        ul = u_lat.reshape(b, -1, S5_GROUPS, S5_GROUP)
        if dr == 1:
            uc, ul = uc[:, ::-1], ul[:, ::-1]
        zero = jnp.zeros((b, S5_GROUPS, S5_STATE), f32)
        yc, h_ctx_end = s5_scan(uc, a_re, a_im, bb_re, bb_im, cr, ci, zero, zero, ctx_out)
        yl, _ = s5_scan(ul, a_re, a_im, bb_re, bb_im, cr, ci, h_ctx_end[0], h_ctx_end[1], True)
        if dr == 1:
            yl = yl[:, ::-1]
            yc = yc[:, ::-1] if ctx_out else None
        y_lat = y_lat + yl
        if ctx_out:
            y_ctx = y_ctx + yc

    def glu_out(y):
        a, g = jnp.split(jax.nn.gelu(y).astype(h_lat.dtype) @ w_glu, 2, axis=-1)
        return a * jax.nn.sigmoid(g)

    return (glu_out(y_ctx) if ctx_out else None), glu_out(y_lat)


def setup_inputs(seed: int = 0) -> dict:
    key = jax.random.key(seed)
    keys = iter(jax.random.split(key, 32))
    f32 = jnp.float32

    def nrm(shape, scale):
        return scale * jax.random.normal(next(keys), shape, f32)

    def gain(shape):
        return 1.0 + 0.01 * jax.random.normal(next(keys), shape, f32)

    G, P, H = S5_GROUPS, S5_STATE, S5_GROUP
    inp = {}
    inp['x'] = nrm((BATCH, SEQ, D_MODEL), 1.0)
    inp['c'] = nrm((BATCH, D_MODEL), 1.0)
    inp['ctx'] = nrm((BATCH, CTX_LEN, D_MODEL), 1.0)
    inp['c_ctx'] = nrm((D_MODEL,), 1.0)
    inp['mod_w'] = nrm((DEPTH, D_MODEL, N_MOD * D_MODEL), D_MODEL ** -0.5)
    inp['mod_b'] = nrm((DEPTH, N_MOD * D_MODEL), 0.01)
    inp['norm_pre'] = gain((DEPTH, 3, D_MODEL))
    inp['norm_post'] = gain((DEPTH, 3, D_MODEL))
    inp['ffn_w13'] = nrm((DEPTH, 2, D_MODEL, 2 * D_FF), D_MODEL ** -0.5)
    inp['ffn_w2'] = nrm((DEPTH, 2, D_FF, D_MODEL), D_FF ** -0.5)
    inp['attn_w_in'] = nrm((N_EVEN, D_MODEL, ATTN_IN), D_MODEL ** -0.5)
    inp['mla_q_norm'] = gain((N_EVEN, MLA_Q_LORA))
    inp['mla_w_uq'] = nrm((N_EVEN, MLA_Q_LORA, MLA_HEADS * (MLA_NOPE + MLA_ROPE)), MLA_Q_LORA ** -0.5)
    inp['mla_kv_norm'] = gain((N_EVEN, MLA_KV_LORA))
    inp['mla_w_ukv'] = nrm((N_EVEN, MLA_KV_LORA, MLA_HEADS * (MLA_NOPE + MLA_V)), MLA_KV_LORA ** -0.5)
    inp['swa_sink'] = nrm((N_EVEN, SWA_HEADS), 0.5)
    inp['attn_w_out'] = nrm((N_EVEN, ATTN_OUT, D_MODEL), ATTN_OUT ** -0.5)
    inp['s5_w_in'] = nrm((N_ODD, D_MODEL, S5_WIDTH), D_MODEL ** -0.5)
    inp['s5_lambda_re'] = -0.5 + nrm((N_ODD, 2, G, P), 0.01)
    inp['s5_lambda_im'] = np.pi * jnp.arange(P, dtype=f32) + nrm((N_ODD, 2, G, P), 0.01)
    inp['s5_b_re'] = nrm((N_ODD, 2, G, P, H), (2 * H) ** -0.5)
    inp['s5_b_im'] = nrm((N_ODD, 2, G, P, H), (2 * H) ** -0.5)
    inp['s5_c_re'] = nrm((N_ODD, 2, G, H, P), (2 * P) ** -0.5)
    inp['s5_c_im'] = nrm((N_ODD, 2, G, H, P), (2 * P) ** -0.5)
    inp['s5_log_step'] = jax.random.uniform(next(keys), (N_ODD, 2, G), f32, math.log(1e-3), math.log(1e-1))
    inp['s5_d'] = nrm((N_ODD, S5_WIDTH), 1.0)
    inp['s5_w_glu'] = nrm((N_ODD, S5_WIDTH, 2 * D_MODEL), S5_WIDTH ** -0.5)
    return inp


def reference(x, c, ctx, c_ctx, mod_w, mod_b, norm_pre, norm_post, ffn_w13, ffn_w2,
              attn_w_in, mla_q_norm, mla_w_uq, mla_kv_norm, mla_w_ukv, swa_sink, attn_w_out,
              s5_w_in, s5_lambda_re, s5_lambda_im, s5_b_re, s5_b_im, s5_c_re, s5_c_im,
              s5_log_step, s5_d, s5_w_glu):
    b, n = x.shape[:2]
    ROWS = n // GRID_W
    rows = jnp.repeat(jnp.arange(ROWS), GRID_W)
    cols = jnp.tile(jnp.arange(GRID_W), ROWS)
    rope_a = axial_rope_table(rows, cols, MLA_ROPE)
    rope_b = axial_rope_table(rows, cols, SWA_HEAD_DIM)
    h_lat, h_ctx = x, ctx
    for l in range(DEPTH):
        last = l == DEPTH - 1
        m_lat = jnp.moveaxis((jax.nn.silu(c) @ mod_w[l] + mod_b[l]).reshape(b, N_MOD, 1, D_MODEL), 1, 0)
        m_ctx = (jax.nn.silu(c_ctx) @ mod_w[l] + mod_b[l]).reshape(N_MOD, 1, 1, D_MODEL)
        h_lat = ffn_sublayer(h_lat, m_lat, 0, norm_pre[l, 0], norm_post[l, 0], ffn_w13[l, 0], ffn_w2[l, 0])
        h_ctx = ffn_sublayer(h_ctx, m_ctx, 0, norm_pre[l, 0], norm_post[l, 0], ffn_w13[l, 0], ffn_w2[l, 0])
        a_lat = modulate(h_lat, norm_pre[l, 1], m_lat[3], m_lat[4])
        a_ctx = modulate(h_ctx, norm_pre[l, 1], m_ctx[3], m_ctx[4])
        if l % 2 == 0:
            e = l // 2
            y_ctx, y_lat = attn_mixer(a_ctx, a_lat, rope_a, rope_b, attn_w_in[e], mla_q_norm[e], mla_w_uq[e],
                                      mla_kv_norm[e], mla_w_ukv[e], swa_sink[e], attn_w_out[e], not last)
        else:
            o = l // 2
            y_ctx, y_lat = s5_mixer(a_ctx, a_lat, s5_w_in[o], s5_lambda_re[o], s5_lambda_im[o], s5_b_re[o],
                                    s5_b_im[o], s5_c_re[o], s5_c_im[o], s5_log_step[o], s5_d[o], s5_w_glu[o],
                                    not last)
        h_lat = h_lat + m_lat[5] * rmsnorm(y_lat, norm_post[l, 1])
        h_lat = ffn_sublayer(h_lat, m_lat, 2, norm_pre[l, 2], norm_post[l, 2], ffn_w13[l, 1], ffn_w2[l, 1])
        if not last:
            h_ctx = h_ctx + m_ctx[5] * rmsnorm(y_ctx, norm_post[l, 1])
            h_ctx = ffn_sublayer(h_ctx, m_ctx, 2, norm_pre[l, 2], norm_post[l, 2], ffn_w13[l, 1], ffn_w2[l, 1])
    return h_lat
```

```python
import functools
import math

import numpy as np
import jax
import jax.numpy as jnp
from jax import lax
from jax.experimental import pallas as pl
from jax.experimental.pallas import tpu as pltpu

F32 = jnp.float32
BF16 = jnp.bfloat16

LANES = 128
VMEM_LIMIT = 56 * 1024 * 1024

N_MOD = 9
FFN_RES = 0.5
EPS = 1e-6
ROPE_BASE = 10000.0
GRID_W = 64
NEG_INF = -1e30

MLA_HEADS = 8
MLA_Q_LORA = 256
MLA_KV_LORA = 128
MLA_NOPE = 64
MLA_ROPE = 32
MLA_V = 64
MLA_SCALE = (MLA_NOPE + MLA_ROPE) ** -0.5

SWA_HEADS = 8
SWA_KV_HEADS = 2
SWA_GROUP = SWA_HEADS // SWA_KV_HEADS
SWA_HEAD_DIM = 64
SWA_WINDOW = 128
SWA_SCALE = SWA_HEAD_DIM ** -0.5
Q_BLOCK = 128

S5_GROUP = 16
S5_STATE = 64
S5_CHUNK = 128
S5_MAX_RE = -1e-4


def _params(n_grid):
    return pltpu.CompilerParams(dimension_semantics=("arbitrary",) * n_grid,
                                vmem_limit_bytes=VMEM_LIMIT)


def _rms(x, g):
    return x * lax.rsqrt(jnp.mean(x * x, axis=-1, keepdims=True) + EPS) * g


def _const_spec(shape):
    nd = len(shape)
    return pl.BlockSpec(shape, lambda *_: (0,) * nd, pipeline_mode=pl.Buffered(1))


def _dot_nt(a, b):
    return lax.dot_general(a, b, (((1,), (1,)), ((), ())), preferred_element_type=F32)


def _mod_kernel(c_ref, w_ref, b_ref, o_ref):
    a = jax.nn.silu(c_ref[...]).astype(BF16)
    o_ref[...] = jnp.dot(a, w_ref[...].astype(BF16), preferred_element_type=F32) + b_ref[...]


def _modulation(cs, mod_w, mod_b, tn=1024):
    depth, d, n = mod_w.shape
    r = cs.shape[0]
    return pl.pallas_call(
        _mod_kernel,
        grid=(depth, n // tn),
        in_specs=[pl.BlockSpec((r, d), lambda l, j: (0, 0)),
                  pl.BlockSpec((None, d, tn), lambda l, j: (l, 0, j)),
                  pl.BlockSpec((None, 1, tn), lambda l, j: (l, 0, j))],
        out_specs=pl.BlockSpec((None, r, tn), lambda l, j: (l, 0, j)),
        out_shape=jax.ShapeDtypeStruct((depth, r, n), F32),
        compiler_params=_params(2),
    )(cs, mod_w, mod_b.reshape(depth, 1, n))


def _ffn_kernel(h_ref, mod_ref, gpre_ref, gpost_ref, w13_ref, w2_ref, o_ref, acc_ref, *, j, fc):
    x = h_ref[...]
    shift = mod_ref[3 * j:3 * j + 1, :]
    scale = mod_ref[3 * j + 1:3 * j + 2, :]
    gate = mod_ref[3 * j + 2:3 * j + 3, :]
    a = (_rms(x, gpre_ref[...]) * (1.0 + scale) + shift).astype(BF16)
    f = w2_ref.shape[0]
    for c in range(f // fc):
        g = jnp.dot(a, w13_ref[:, c * fc:(c + 1) * fc], preferred_element_type=F32)
        u = jnp.dot(a, w13_ref[:, f + c * fc:f + (c + 1) * fc], preferred_element_type=F32)
        act = (jax.nn.silu(g) * u).astype(BF16)
        contrib = jnp.dot(act, w2_ref[c * fc:(c + 1) * fc, :], preferred_element_type=F32)
        if c == 0:
            acc_ref[...] = contrib
        else:
            acc_ref[...] += contrib
    o_ref[...] = x + FFN_RES * gate * _rms(acc_ref[...], gpost_ref[...])


def _ffn(h, mod, j, g_pre, g_post, w13, w2, tm):
    b, s, d = h.shape
    f = w2.shape[0]
    fc = 256 if f % 256 == 0 else f
    tm = min(tm, s)
    return pl.pallas_call(
        functools.partial(_ffn_kernel, j=j, fc=fc),
        grid=(b, s // tm),
        in_specs=[pl.BlockSpec((None, tm, d), lambda bi, i: (bi, i, 0)),
                  pl.BlockSpec((None, N_MOD, d), lambda bi, i: (bi, 0, 0)),
                  _const_spec((1, d)), _const_spec((1, d)),
                  _const_spec(w13.shape), _const_spec(w2.shape)],
        out_specs=pl.BlockSpec((None, tm, d), lambda bi, i: (bi, i, 0)),
        out_shape=jax.ShapeDtypeStruct((b, s, d), F32),
        scratch_shapes=[pltpu.VMEM((tm, d), F32)],
        compiler_params=_params(2),
    )(h, mod, g_pre.reshape(1, d), g_post.reshape(1, d), w13, w2)


_O_CQ = 0
_O_CKV = _O_CQ + MLA_Q_LORA
_O_KPE = _O_CKV + MLA_KV_LORA
_O_KPE_SW = _O_KPE + LANES
_O_QS = _O_KPE_SW + LANES
_O_QS_SW = _O_QS + SWA_HEADS * SWA_HEAD_DIM
_O_KS = _O_QS_SW + SWA_HEADS * SWA_HEAD_DIM
_O_KS_SW = _O_KS + SWA_KV_HEADS * LANES
_O_VS = _O_KS_SW + SWA_KV_HEADS * LANES
_W1_COLS = _O_VS + SWA_KV_HEADS * LANES
_HL = MLA_HEADS * LANES
_QS_W = SWA_HEADS * SWA_HEAD_DIM
_KS_W = SWA_KV_HEADS * LANES


def _attn_proj_kernel(h_ref, mod_ref, gpre_ref, w1_ref, qn_ref, wq_ref, kvn_ref, wkv_ref,
                      ca_ref, sa_ref, cb_ref, sb_ref,
                      qa_ref, ka_ref, va_ref, qs_ref, ks_ref, vs_ref):
    x = h_ref[...]
    shift = mod_ref[3:4, :]
    scale = mod_ref[4:5, :]
    a = (_rms(x, gpre_ref[...]) * (1.0 + scale) + shift).astype(BF16)
    p = jnp.dot(a, w1_ref[...], preferred_element_type=F32)
    ca, sa, cb, sb = ca_ref[...], sa_ref[...], cb_ref[...], sb_ref[...]
    lane = lax.broadcasted_iota(jnp.int32, (1, LANES), 1)
    ones_col = (lane == MLA_V).astype(F32)

    cqn = _rms(p[:, _O_CQ:_O_CQ + MLA_Q_LORA], qn_ref[...]).astype(BF16)
    q2 = jnp.dot(cqn, wq_ref[...], preferred_element_type=F32)
    ckvn = _rms(p[:, _O_CKV:_O_CKV + MLA_KV_LORA], kvn_ref[...]).astype(BF16)
    kv2 = jnp.dot(ckvn, wkv_ref[...], preferred_element_type=F32)
    kpe = p[:, _O_KPE:_O_KPE + LANES] * ca + p[:, _O_KPE_SW:_O_KPE_SW + LANES] * sa
    for hd in range(MLA_HEADS):
        lo, hi = hd * LANES, (hd + 1) * LANES
        q = q2[:, lo:hi] * ca + q2[:, _HL + lo:_HL + hi] * sa
        qa_ref[:, lo:hi] = (q * MLA_SCALE).astype(BF16)
        ka_ref[:, lo:hi] = (kv2[:, lo:hi] + kpe).astype(BF16)
        va_ref[:, lo:hi] = (kv2[:, _HL + lo:_HL + hi] + ones_col).astype(BF16)

    for t in range(_QS_W // LANES):
        lo, hi = t * LANES, (t + 1) * LANES
        q = p[:, _O_QS + lo:_O_QS + hi] * cb + p[:, _O_QS_SW + lo:_O_QS_SW + hi] * sb
        qs_ref[:, lo:hi] = (q * SWA_SCALE).astype(BF16)
    for t in range(SWA_KV_HEADS):
        lo, hi = t * LANES, (t + 1) * LANES
        k = p[:, _O_KS + lo:_O_KS + hi] * cb + p[:, _O_KS_SW + lo:_O_KS_SW + hi] * sb
        ks_ref[:, lo:hi] = k.astype(BF16)
        vs_ref[:, lo:hi] = (p[:, _O_VS + lo:_O_VS + hi] + ones_col).astype(BF16)


def _attn_project(h, mod, g_pre, w1, q_norm, wq, kv_norm, wkv, tabs, tm):
    b, s, d = h.shape
    tm = min(tm, s)
    row = lambda bi, i: (bi, i, 0)
    tab = pl.BlockSpec((tm, LANES), lambda bi, i: (i, 0))
    widths = (_HL, _HL, _HL, _QS_W, _KS_W, _KS_W)
    return pl.pallas_call(
        _attn_proj_kernel,
        grid=(b, s // tm),
        in_specs=[pl.BlockSpec((None, tm, d), row),
                  pl.BlockSpec((None, N_MOD, d), lambda bi, i: (bi, 0, 0)),
                  _const_spec((1, d)), _const_spec(w1.shape),
                  _const_spec((1, MLA_Q_LORA)), _const_spec(wq.shape),
                  _const_spec((1, MLA_KV_LORA)), _const_spec(wkv.shape),
                  tab, tab, tab, tab],
        out_specs=[pl.BlockSpec((None, tm, w), row) for w in widths],
        out_shape=[jax.ShapeDtypeStruct((b, s, w), BF16) for w in widths],
        compiler_params=_params(2),
    )(h, mod, g_pre.reshape(1, d), w1, q_norm.reshape(1, -1), wq, kv_norm.reshape(1, -1), wkv, *tabs)


def _mla_kernel(*refs, n_lat_chunks, tk):
    if n_lat_chunks:
        q_ref, kc_ref, vc_ref, kl_ref, vl_ref, o_ref, m_ref, acc_ref = refs
    else:
        q_ref, kc_ref, vc_ref, o_ref, m_ref, acc_ref = refs
    q = q_ref[...]

    def update(k, v, first):
        s = _dot_nt(q, k)
        m_old = m_ref[...]
        m_blk = jnp.max(s, axis=-1, keepdims=True)
        m_new = m_blk if first else jnp.maximum(m_old, m_blk)
        p = jnp.exp(s - m_new).astype(BF16)
        pv = jnp.dot(p, v, preferred_element_type=F32)
        if first:
            acc_ref[...] = pv
        else:
            acc_ref[...] = acc_ref[...] * jnp.exp(m_old - m_new) + pv
        m_ref[...] = m_new

    update(kc_ref[...], vc_ref[...], True)
    if n_lat_chunks:
        def body(j, carry):
            off = pl.multiple_of(j * tk, tk)
            update(kl_ref[pl.ds(off, tk), :], vl_ref[pl.ds(off, tk), :], False)
            return carry
        lax.fori_loop(0, n_lat_chunks, body, 0)
    acc = acc_ref[...]
    o_ref[...] = (acc / acc[:, MLA_V:MLA_V + 1]).astype(o_ref.dtype)


def _mla(q, kc, vc, kl, vl, tq, tk):
    b, s, _ = q.shape
    n_ctx = kc.shape[1]
    tq = min(tq, s)
    qspec = pl.BlockSpec((None, tq, LANES), lambda bi, h, i: (bi, i, h))
    cspec = pl.BlockSpec((None, n_ctx, LANES), lambda bi, h, i: (bi, 0, h))
    in_specs = [qspec, cspec, cspec]
    args = [q, kc, vc]
    n_lat_chunks = 0
    if kl is not None:
        n_lat = kl.shape[1]
        tk = min(tk, n_lat)
        n_lat_chunks = n_lat // tk
        lspec = pl.BlockSpec((None, n_lat, LANES), lambda bi, h, i: (bi, 0, h))
        in_specs += [lspec, lspec]
        args += [kl, vl]
    return pl.pallas_call(
        functools.partial(_mla_kernel, n_lat_chunks=n_lat_chunks, tk=tk),
        grid=(b, MLA_HEADS, s // tq),
        in_specs=in_specs,
        out_specs=qspec,
        out_shape=jax.ShapeDtypeStruct((b, s, _HL), BF16),
        scratch_shapes=[pltpu.VMEM((tq, 1), F32), pltpu.VMEM((tq, LANES), F32)],
        compiler_params=_params(3),
    )(*args)


def _swa_kernel(*refs, windowed, n_blocks):
    if windowed:
        (sink_ref, q_ref, kc_ref, vc_ref, kp_ref, k0_ref, kn_ref,
         vp_ref, v0_ref, vn_ref, o_ref) = refs
    else:
        sink_ref, q_ref, kc_ref, vc_ref, o_ref = refs
    i = pl.program_id(1)
    tq = q_ref.shape[0]
    rows = SWA_GROUP * tq
    lane = lax.broadcasted_iota(jnp.int32, (1, LANES), 1)
    half_mask = [(lane < SWA_HEAD_DIM), (lane >= SWA_HEAD_DIM)]
    if windowed:
        r = jnp.bitwise_and(lax.broadcasted_iota(jnp.int32, (rows, tq), 0), tq - 1)
        c = lax.broadcasted_iota(jnp.int32, (rows, tq), 1)
        prev_ok = c >= r + jnp.where(i > 0, 0, tq)
        next_ok = c <= r - jnp.where(i < n_blocks - 1, 0, tq)
    for kv in range(SWA_KV_HEADS):
        lo, hi = kv * LANES, (kv + 1) * LANES
        qs, sk = [], []
        for g in range(SWA_GROUP):
            hd = kv * SWA_GROUP + g
            t = hd // 2
            qt = q_ref[:, t * LANES:(t + 1) * LANES]
            qs.append(jnp.where(half_mask[hd % 2], qt, jnp.zeros_like(qt)))
            sk.append(jnp.full((tq, 1), sink_ref[hd], F32))
        q4 = jnp.concatenate(qs, axis=0)
        sink = jnp.concatenate(sk, axis=0)
        scores = [_dot_nt(q4, kc_ref[:, lo:hi])]
        vals = [vc_ref[:, lo:hi]]
        if windowed:
            sp = jnp.where(prev_ok, _dot_nt(q4, kp_ref[:, lo:hi]), NEG_INF)
            s0 = _dot_nt(q4, k0_ref[:, lo:hi])
            sn = jnp.where(next_ok, _dot_nt(q4, kn_ref[:, lo:hi]), NEG_INF)
            scores += [sp, s0, sn]
            vals += [vp_ref[:, lo:hi], v0_ref[:, lo:hi], vn_ref[:, lo:hi]]
        m = sink
        for s_ in scores:
            m = jnp.maximum(m, jnp.max(s_, axis=-1, keepdims=True))
        acc = None
        for s_, v_ in zip(scores, vals):
            pv = jnp.dot(jnp.exp(s_ - m).astype(BF16), v_, preferred_element_type=F32)
            acc = pv if acc is None else acc + pv
        denom = acc[:, SWA_HEAD_DIM:SWA_HEAD_DIM + 1] + jnp.exp(sink - m)
        o = (acc / denom).astype(o_ref.dtype)
        for g in range(SWA_GROUP):
            hd = kv * SWA_GROUP + g
            o_ref[:, hd * LANES:(hd + 1) * LANES] = o[g * tq:(g + 1) * tq, :]


def _swa(sink, q, kc, vc, kl, vl):
    b, s, _ = q.shape
    n_ctx = kc.shape[1]
    windowed = kl is not None
    tq = Q_BLOCK if windowed else s
    nb = s // tq
    cspec = pl.BlockSpec((None, n_ctx, _KS_W), lambda bi, i: (bi, 0, 0))
    in_specs = [pl.BlockSpec(memory_space=pltpu.SMEM),
                pl.BlockSpec((None, tq, _QS_W), lambda bi, i: (bi, i, 0)), cspec, cspec]
    args = [sink, q, kc, vc]
    if windowed:
        prev = pl.BlockSpec((None, tq, _KS_W), lambda bi, i: (bi, jnp.maximum(i - 1, 0), 0))
        cur = pl.BlockSpec((None, tq, _KS_W), lambda bi, i: (bi, i, 0))
        nxt = pl.BlockSpec((None, tq, _KS_W), lambda bi, i: (bi, jnp.minimum(i + 1, nb - 1), 0))
        in_specs += [prev, cur, nxt, prev, cur, nxt]
        args += [kl, kl, kl, vl, vl, vl]
    return pl.pallas_call(
        functools.partial(_swa_kernel, windowed=windowed, n_blocks=nb),
        grid=(b, nb),
        in_specs=in_specs,
        out_specs=pl.BlockSpec((None, tq, SWA_HEADS * LANES), lambda bi, i: (bi, i, 0)),
        out_shape=jax.ShapeDtypeStruct((b, s, SWA_HEADS * LANES), BF16),
        compiler_params=_params(2),
    )(*args)


def _attn_out_kernel(oa_ref, ob_ref, h_ref, mod_ref, gpost_ref, wa_ref, wb_ref, o_ref):
    y = jnp.dot(oa_ref[...], wa_ref[...], preferred_element_type=F32)
    y = y + jnp.dot(ob_ref[...], wb_ref[...], preferred_element_type=F32)
    o_ref[...] = h_ref[...] + mod_ref[5:6, :] * _rms(y, gpost_ref[...])


def _attn_out(oa, ob, h, mod, g_post, wa, wb, tm):
    b, s, d = h.shape
    tm = min(tm, s)
    row = lambda bi, i: (bi, i, 0)
    return pl.pallas_call(
        _attn_out_kernel,
        grid=(b, s // tm),
        in_specs=[pl.BlockSpec((None, tm, oa.shape[2]), row),
                  pl.BlockSpec((None, tm, ob.shape[2]), row),
                  pl.BlockSpec((None, tm, d), row),
                  pl.BlockSpec((None, N_MOD, d), lambda bi, i: (bi, 0, 0)),
                  _const_spec((1, d)), _const_spec(wa.shape), _const_spec(wb.shape)],
        out_specs=pl.BlockSpec((None, tm, d), row),
        out_shape=jax.ShapeDtypeStruct((b, s, d), F32),
        compiler_params=_params(2),
    )(oa, ob, h, mod, g_post.reshape(1, d), wa, wb)


def _s5_in_kernel(h_ref, mod_ref, gpre_ref, wt_ref, o_ref):
    x = h_ref[...]
    a = (_rms(x, gpre_ref[...]) * (1.0 + mod_ref[4:5, :]) + mod_ref[3:4, :]).astype(BF16)
    o_ref[...] = _dot_nt(wt_ref[...], a)


def _s5_in(h, mod, g_pre, w_in_t, tm):
    b, s, d = h.shape
    w = w_in_t.shape[0]
    tm = min(tm, s)
    return pl.pallas_call(
        _s5_in_kernel,
        grid=(b, s // tm),
        in_specs=[pl.BlockSpec((None, tm, d), lambda bi, i: (bi, i, 0)),
                  pl.BlockSpec((None, N_MOD, d), lambda bi, i: (bi, 0, 0)),
                  _const_spec((1, d)), _const_spec(w_in_t.shape)],
        out_specs=pl.BlockSpec((None, w, tm), lambda bi, i: (bi, 0, i)),
        out_shape=jax.ShapeDtypeStruct((b, w, s), F32),
        compiler_params=_params(2),
    )(h, mod, g_pre.reshape(1, d), w_in_t)


def _cmul(x, p1, p2):
    return x * p1 + pltpu.roll(x, S5_STATE, 1) * p2


def _s5_core_kernel(u_ref, uc_ref, m_ref, et_ref, ft_ref, ap_ref, dsk_ref, o_ref,
                    ef_ref, eb_ref, xf_ref, xb_ref, *, nb, nc, ncc):
    h = S5_GROUP
    rows = nb * nc
    u32 = jnp.concatenate([u_ref[:, i].reshape(rows, S5_CHUNK) for i in range(h)], axis=1)
    ub = u32.astype(BF16)
    et = et_ref[...]
    ein = jnp.dot(ub, et, preferred_element_type=F32)
    ef_ref[...] = ein[:, :LANES]
    eb_ref[...] = ein[:, LANES:]
    pf1, pf2 = ap_ref[0:1, :LANES], ap_ref[1:2, :LANES]
    pb1, pb2 = ap_ref[0:1, LANES:], ap_ref[1:2, LANES:]

    xf = jnp.zeros((nb, LANES), F32)
    xb = jnp.zeros((nb, LANES), F32)
    for c in range(ncc):
        ecf = jnp.dot(uc_ref[c].astype(BF16), et, preferred_element_type=F32)
        ecb = jnp.dot(uc_ref[ncc - 1 - c].astype(BF16), et, preferred_element_type=F32)
        xf = _cmul(xf, pf1, pf2) + ecf[:, :LANES]
        xb = _cmul(xb, pb1, pb2) + ecb[:, LANES:]

    def body(t, carry):
        xf, xb = carry
        cf = pl.ds(t, nb, stride=nc)
        cb = pl.ds(nc - 1 - t, nb, stride=nc)
        xf_ref[cf, :] = xf
        xb_ref[cb, :] = xb
        xf = _cmul(xf, pf1, pf2) + ef_ref[cf, :]
        xb = _cmul(xb, pb1, pb2) + eb_ref[cb, :]
        return xf, xb

    lax.fori_loop(0, nc, body, (xf, xb))
    xs = jnp.concatenate([xf_ref[...], xb_ref[...]], axis=1).astype(BF16)
    mfull = jnp.concatenate(
        [jnp.concatenate([m_ref[i, j] for j in range(h)], axis=1) for i in range(h)], axis=0)
    y = jnp.dot(ub, mfull, preferred_element_type=F32)
    y = y + jnp.dot(xs, ft_ref[...], preferred_element_type=F32)
    y = y + u32 * dsk_ref[...]
    for j in range(h):
        o_ref[:, j] = y[:, j * S5_CHUNK:(j + 1) * S5_CHUNK].reshape(nb, nc, S5_CHUNK)


def _s5_core(u_t, uc, m, et, ft, ap, dsk, nb):
    b, w, s = u_t.shape
    g = w // S5_GROUP
    nc = s // S5_CHUNK
    ncc = uc.shape[2]
    hl = S5_GROUP * S5_CHUNK
    u5 = u_t.reshape(b, g, S5_GROUP, nc, S5_CHUNK)
    blk = pl.BlockSpec((nb, None, S5_GROUP, nc, S5_CHUNK), lambda gi, bi: (bi, gi, 0, 0, 0))
    per_g = lambda shape: pl.BlockSpec((None,) + shape, lambda gi, bi: (gi,) + (0,) * len(shape))
    out = pl.pallas_call(
        functools.partial(_s5_core_kernel, nb=nb, nc=nc, ncc=ncc),
        grid=(g, b // nb),
        in_specs=[blk,
                  pl.BlockSpec((None, None, ncc, nb, hl), lambda gi, bi: (gi, bi, 0, 0, 0)),
                  per_g((S5_GROUP, S5_GROUP, S5_CHUNK, S5_CHUNK)),
                  per_g((hl, 2 * LANES)), per_g((2 * LANES, hl)),
                  per_g((2, 2 * LANES)), per_g((1, hl))],
        out_specs=blk,
        out_shape=jax.ShapeDtypeStruct(u5.shape, F32),
        scratch_shapes=[pltpu.VMEM((nb * nc, LANES), F32) for _ in range(4)],
        compiler_params=_params(2),
    )(u5, uc, m, et, ft, ap, dsk)
    return out.reshape(b, w, s)


def _s5_out_kernel(y_ref, h_ref, mod_ref, gpost_ref, wt_ref, o_ref):
    d = h_ref.shape[1]
    gy = jax.nn.gelu(y_ref[...]).astype(BF16)
    z = jnp.dot(wt_ref[...], gy, preferred_element_type=F32)
    v = (z[:d, :] * jax.nn.sigmoid(z[d:, :])).T
    o_ref[...] = h_ref[...] + mod_ref[5:6, :] * _rms(v, gpost_ref[...])


def _s5_out(y_t, h, mod, g_post, w_glu_t, tm):
    b, s, d = h.shape
    w = y_t.shape[1]
    tm = min(tm, s)
    return pl.pallas_call(
        _s5_out_kernel,
        grid=(b, s // tm),
        in_specs=[pl.BlockSpec((None, w, tm), lambda bi, i: (bi, 0, i)),
                  pl.BlockSpec((None, tm, d), lambda bi, i: (bi, i, 0)),
                  pl.BlockSpec((None, N_MOD, d), lambda bi, i: (bi, 0, 0)),
                  _const_spec((1, d)), _const_spec(w_glu_t.shape)],
        out_specs=pl.BlockSpec((None, tm, d), lambda bi, i: (bi, i, 0)),
        out_shape=jax.ShapeDtypeStruct((b, s, d), F32),
        compiler_params=_params(2),
    )(y_t, h, mod, g_post.reshape(1, d), w_glu_t)


def _rope_angles(n, d_rot):
    t = jnp.arange(n)
    rows, cols = t // GRID_W, t % GRID_W
    d_axis = d_rot // 2
    inv = ROPE_BASE ** (-jnp.arange(0, d_axis, 2, dtype=F32) / d_axis)
    ang = jnp.concatenate([rows.astype(F32)[:, None] * inv, cols.astype(F32)[:, None] * inv], axis=-1)
    return jnp.cos(ang), jnp.sin(ang)


def _rope_tables(n, rotate):
    pad = LANES - MLA_NOPE - MLA_ROPE
    if rotate:
        ca, sa = _rope_angles(n, MLA_ROPE)
        cb, sb = _rope_angles(n, SWA_HEAD_DIM)
    else:
        ca, sa = jnp.ones((n, MLA_ROPE // 2), F32), jnp.zeros((n, MLA_ROPE // 2), F32)
        cb, sb = jnp.ones((n, SWA_HEAD_DIM // 2), F32), jnp.zeros((n, SWA_HEAD_DIM // 2), F32)
    one, zero = jnp.ones((n, MLA_NOPE), F32), jnp.zeros((n, MLA_NOPE), F32)
    zpad = jnp.zeros((n, pad), F32)
    return (jnp.concatenate([one, ca, ca, zpad], axis=1), jnp.concatenate([zero, sa, sa, zpad], axis=1),
            jnp.concatenate([cb] * 4, axis=1), jnp.concatenate([sb] * 4, axis=1))


def _rot_partner(w, half):
    return jnp.concatenate([-w[..., half:], w[..., :half]], axis=-1)


def _attn_weights(w_in, w_uq, w_ukv, w_out):
    d = w_in.shape[0]
    sizes = [MLA_Q_LORA, MLA_KV_LORA, MLA_ROPE, SWA_HEADS * SWA_HEAD_DIM,
             SWA_KV_HEADS * SWA_HEAD_DIM, SWA_KV_HEADS * SWA_HEAD_DIM]
    cq, ckv, kpe, qs, ks, vs = jnp.split(w_in, [int(v) for v in np.cumsum(sizes)[:-1]], axis=1)
    pad_a = LANES - MLA_NOPE - MLA_ROPE
    z = lambda *shape: jnp.zeros(shape, F32)
    kpe_blk = jnp.concatenate([z(d, MLA_NOPE), kpe, z(d, pad_a)], axis=1)
    kpe_sw = jnp.concatenate([z(d, MLA_NOPE), _rot_partner(kpe, MLA_ROPE // 2), z(d, pad_a)], axis=1)
    qs3 = qs.reshape(d, SWA_HEADS, SWA_HEAD_DIM)
    qs_sw = _rot_partner(qs3, SWA_HEAD_DIM // 2).reshape(d, -1)
    ks3 = ks.reshape(d, SWA_KV_HEADS, SWA_HEAD_DIM)
    ks_sw3 = _rot_partner(ks3, SWA_HEAD_DIM // 2)
    dup = lambda t: jnp.concatenate([t, t], axis=-1).reshape(d, -1)
    vs3 = vs.reshape(d, SWA_KV_HEADS, SWA_HEAD_DIM)
    vs_pad = jnp.concatenate([vs3, jnp.zeros_like(vs3)], axis=-1).reshape(d, -1)
    w1 = jnp.concatenate([cq, ckv, kpe_blk, kpe_sw, qs, qs_sw, dup(ks3), dup(ks_sw3), vs_pad], axis=1)
    assert w1.shape[1] == _W1_COLS

    ql = w_uq.shape[0]
    uq = w_uq.reshape(ql, MLA_HEADS, MLA_NOPE + MLA_ROPE)
    nope, pe = uq[..., :MLA_NOPE], uq[..., MLA_NOPE:]
    zq = jnp.zeros((ql, MLA_HEADS, pad_a), F32)
    wq_blk = jnp.concatenate([nope, pe, zq], axis=-1).reshape(ql, -1)
    wq_sw = jnp.concatenate([jnp.zeros_like(nope), _rot_partner(pe, MLA_ROPE // 2), zq], axis=-1).reshape(ql, -1)
    wq = jnp.concatenate([wq_blk, wq_sw], axis=1)

    kl = w_ukv.shape[0]
    ukv = w_ukv.reshape(kl, MLA_HEADS, MLA_NOPE + MLA_V)
    kn, vv = ukv[..., :MLA_NOPE], ukv[..., MLA_NOPE:]
    wk = jnp.concatenate([kn, jnp.zeros((kl, MLA_HEADS, LANES - MLA_NOPE), F32)], axis=-1).reshape(kl, -1)
    wv = jnp.concatenate([vv, jnp.zeros((kl, MLA_HEADS, LANES - MLA_V), F32)], axis=-1).reshape(kl, -1)
    wkv = jnp.concatenate([wk, wv], axis=1)

    dm = w_out.shape[1]
    na = MLA_HEADS * MLA_V
    oa = w_out[:na].reshape(MLA_HEADS, MLA_V, dm)
    ob = w_out[na:].reshape(SWA_HEADS, SWA_HEAD_DIM, dm)
    wa = jnp.concatenate([oa, jnp.zeros((MLA_HEADS, LANES - MLA_V, dm), F32)], axis=1).reshape(-1, dm)
    wb = jnp.concatenate([ob, jnp.zeros((SWA_HEADS, LANES - SWA_HEAD_DIM, dm), F32)], axis=1).reshape(-1, dm)
    return tuple(t.astype(BF16) for t in (w1, wq, wkv, wa, wb))


def _complex_powers(a_re, a_im, n):
    pr, pi = jnp.ones_like(a_re)[None], jnp.zeros_like(a_im)[None]
    sr, si = a_re, a_im
    while pr.shape[0] < n + 1:
        nr, ni = pr * sr - pi * si, pr * si + pi * sr
        pr, pi = jnp.concatenate([pr, nr], 0), jnp.concatenate([pi, ni], 0)
        sr, si = sr * sr - si * si, 2.0 * sr * si
    return pr[:n + 1], pi[:n + 1]


def _toeplitz(k):
    ln = (k.shape[-1] + 1) // 2
    kp = jnp.concatenate([k, jnp.zeros(k.shape[:-1] + (1,), k.dtype)], axis=-1)
    t = jnp.broadcast_to(kp[..., None, :], k.shape[:-1] + (ln, 2 * ln)).reshape(k.shape[:-1] + (2 * ln * ln,))
    t = t[..., :ln * (2 * ln - 1)].reshape(k.shape[:-1] + (ln, 2 * ln - 1))
    return t[..., ln - 1:]


def _s5_operators(lam_re, lam_im, b_re, b_im, c_re, c_im, log_step):
    hi = lax.Precision.HIGHEST
    ln = S5_CHUNK
    g, p = lam_re.shape[1:]
    hh = b_re.shape[-1]
    ks, ets, fts, aps = [], [], [], []
    for dr in range(2):
        lre = jnp.minimum(lam_re[dr], S5_MAX_RE)
        lim = lam_im[dr]
        dt = jnp.exp(log_step[dr])[:, None]
        mag = jnp.exp(lre * dt)
        a_re, a_im = mag * jnp.cos(lim * dt), mag * jnp.sin(lim * dt)
        den = lre * lre + lim * lim
        f_re = ((a_re - 1.0) * lre + a_im * lim) / den
        f_im = (a_im * lre - (a_re - 1.0) * lim) / den
        bb_re = f_re[..., None] * b_re[dr] - f_im[..., None] * b_im[dr]
        bb_im = f_re[..., None] * b_im[dr] + f_im[..., None] * b_re[dr]
        cr, ci = c_re[dr], c_im[dr]
        pr, pi = _complex_powers(a_re, a_im, ln)
        cb_re = jnp.einsum('gjp,gpi->gpji', cr, bb_re) - jnp.einsum('gjp,gpi->gpji', ci, bb_im)
        cb_im = jnp.einsum('gjp,gpi->gpji', cr, bb_im) + jnp.einsum('gjp,gpi->gpji', ci, bb_re)
        k = (jnp.einsum('dgp,gpji->gijd', pr[:ln], cb_re, precision=hi)
             - jnp.einsum('dgp,gpji->gijd', pi[:ln], cb_im, precision=hi))
        ks.append(k)
        er, ei = (pr[:ln][::-1], pi[:ln][::-1]) if dr == 0 else (pr[:ln], pi[:ln])
        e_re = jnp.einsum('mgp,gpi->gimp', er, bb_re) - jnp.einsum('mgp,gpi->gimp', ei, bb_im)
        e_im = jnp.einsum('mgp,gpi->gimp', er, bb_im) + jnp.einsum('mgp,gpi->gimp', ei, bb_re)
        ets.append(jnp.concatenate([e_re, e_im], axis=-1).reshape(g, hh * ln, 2 * p))
        fr, fi = (pr[1:], pi[1:]) if dr == 0 else (pr[1:][::-1], pi[1:][::-1])
        f_xre = jnp.einsum('gjp,lgp->gpjl', cr, fr) - jnp.einsum('gjp,lgp->gpjl', ci, fi)
        f_xim = -(jnp.einsum('gjp,lgp->gpjl', cr, fi) + jnp.einsum('gjp,lgp->gpjl', ci, fr))
        fts.append(jnp.concatenate([f_xre, f_xim], axis=1).reshape(g, 2 * p, hh * ln))
        aps.append(jnp.stack([jnp.concatenate([pr[ln], pr[ln]], axis=-1),
                              jnp.concatenate([-pi[ln], pi[ln]], axis=-1)], axis=1))
    kf, kb = ks
    k0 = kf[..., :1] + kb[..., :1]
    kk = jnp.concatenate([kb[..., :0:-1], k0, kf[..., 1:]], axis=-1)
    m = _toeplitz(kk.astype(BF16))
    et = jnp.concatenate(ets, axis=-1).astype(BF16)
    ft = jnp.concatenate(fts, axis=1).astype(BF16)
    ap = jnp.concatenate(aps, axis=-1)
    return m, et, ft, ap


def kernel(x, c, ctx, c_ctx, mod_w, mod_b, norm_pre, norm_post, ffn_w13, ffn_w2,
           attn_w_in, mla_q_norm, mla_w_uq, mla_kv_norm, mla_w_ukv, swa_sink, attn_w_out,
           s5_w_in, s5_lambda_re, s5_lambda_im, s5_b_re, s5_b_im, s5_c_re, s5_c_im,
           s5_log_step, s5_d, s5_w_glu):
    b, n, d = x.shape
    n_ctx = ctx.shape[1]
    depth = mod_w.shape[0]
    tm = 512

    rows = -(-(b + 1) // 8) * 8
    cs = jnp.concatenate([c, c_ctx[None], jnp.zeros((rows - b - 1, d), F32)], axis=0)
    mods = _modulation(cs, mod_w, mod_b)
    w13 = ffn_w13.astype(BF16)
    w2 = ffn_w2.astype(BF16)

    h_lat, h_ctx = x, ctx
    for l in range(depth):
        last = l == depth - 1
        m_lat = mods[l, :b].reshape(b, N_MOD, d)
        m_ctx = jnp.broadcast_to(mods[l, b].reshape(1, N_MOD, d), (b, N_MOD, d))
        h_lat = _ffn(h_lat, m_lat, 0, norm_pre[l, 0], norm_post[l, 0], w13[l, 0], w2[l, 0], tm)
        h_ctx = _ffn(h_ctx, m_ctx, 0, norm_pre[l, 0], norm_post[l, 0], w13[l, 0], w2[l, 0], tm)
        if l % 2 == 0:
            e = l // 2
            w1, wq, wkv, wa, wb = _attn_weights(attn_w_in[e], mla_w_uq[e], mla_w_ukv[e], attn_w_out[e])
            proj = functools.partial(_attn_project, g_pre=norm_pre[l, 1], w1=w1, q_norm=mla_q_norm[e],
                                     wq=wq, kv_norm=mla_kv_norm[e], wkv=wkv, tm=tm)
            qa_c, ka_c, va_c, qs_c, ks_c, vs_c = proj(h_ctx, m_ctx, tabs=_rope_tables(n_ctx, False))
            qa_l, ka_l, va_l, qs_l, ks_l, vs_l = proj(h_lat, m_lat, tabs=_rope_tables(n, True))
            o_a = _mla(qa_l, ka_c, va_c, ka_l, va_l, tq=256, tk=512)
            o_b = _swa(swa_sink[e], qs_l, ks_c, vs_c, ks_l, vs_l)
            h_lat = _attn_out(o_a, o_b, h_lat, m_lat, norm_post[l, 1], wa, wb, tm)
            if not last:
                o_a_c = _mla(qa_c, ka_c, va_c, None, None, tq=256, tk=512)
                o_b_c = _swa(swa_sink[e], qs_c, ks_c, vs_c, None, None)
                h_ctx = _attn_out(o_a_c, o_b_c, h_ctx, m_ctx, norm_post[l, 1], wa, wb, tm)
        else:
            o = l // 2
            assert last, "S5 context outputs are only needed when another layer follows"
            w_in_t = s5_w_in[o].T.astype(BF16)
            w_glu_t = s5_w_glu[o].T.astype(BF16)
            m_op, et, ft, ap = _s5_operators(s5_lambda_re[o], s5_lambda_im[o], s5_b_re[o], s5_b_im[o],
                                             s5_c_re[o], s5_c_im[o], s5_log_step[o])
            width = s5_w_in.shape[2]
            g = width // S5_GROUP
            u_lat = _s5_in(h_lat, m_lat, norm_pre[l, 1], w_in_t, tm)
            u_ctx = _s5_in(h_ctx, m_ctx, norm_pre[l, 1], w_in_t, tm)
            ncc = n_ctx // S5_CHUNK
            nb = b // 2 if b % 2 == 0 else b
            uc = u_ctx.reshape(b // nb, nb, g, S5_GROUP, ncc, S5_CHUNK).transpose(2, 0, 4, 1, 3, 5)
            uc = uc.reshape(g, b // nb, ncc, nb, S5_GROUP * S5_CHUNK)
            dsk = jnp.repeat(s5_d[o].astype(F32), S5_CHUNK).reshape(g, 1, S5_GROUP * S5_CHUNK)
            y_t = _s5_core(u_lat, uc, m_op, et, ft, ap, dsk, nb=nb)
            h_lat = _s5_out(y_t, h_lat, m_lat, norm_post[l, 1], w_glu_t, tm)
        h_lat = _ffn(h_lat, m_lat, 2, norm_pre[l, 2], norm_post[l, 2], w13[l, 1], w2[l, 1], tm)
        if not last:
            h_ctx = _ffn(h_ctx, m_ctx, 2, norm_pre[l, 2], norm_post[l, 2], w13[l, 1], w2[l, 1], tm)
    return h_lat
```

```python
import functools
import math

import numpy as np
import jax
import jax.numpy as jnp
from jax import lax
from jax.experimental import pallas as pl
from jax.experimental.pallas import tpu as pltpu

F32 = jnp.float32
BF16 = jnp.bfloat16

LANES = 128
VMEM_LIMIT = 56 * 1024 * 1024

N_MOD = 9
FFN_RES = 0.5
EPS = 1e-6
ROPE_BASE = 10000.0
GRID_W = 64
NEG_INF = -1e30

MLA_HEADS = 8
MLA_Q_LORA = 256
MLA_KV_LORA = 128
MLA_NOPE = 64
MLA_ROPE = 32
MLA_V = 64
MLA_SCALE = (MLA_NOPE + MLA_ROPE) ** -0.5

SWA_HEADS = 8
SWA_KV_HEADS = 2
SWA_GROUP = SWA_HEADS // SWA_KV_HEADS
SWA_HEAD_DIM = 64
SWA_WINDOW = 128
SWA_SCALE = SWA_HEAD_DIM ** -0.5
Q_BLOCK = 128

S5_GROUP = 16
S5_STATE = 64
S5_CHUNK = 128
S5_MAX_RE = -1e-4


def _params(n_grid):
    return pltpu.CompilerParams(dimension_semantics=("arbitrary",) * n_grid,
                                vmem_limit_bytes=VMEM_LIMIT)


def _rms(x, g):
    return x * lax.rsqrt(jnp.mean(x * x, axis=-1, keepdims=True) + EPS) * g


def _const_spec(shape):
    nd = len(shape)
    return pl.BlockSpec(shape, lambda *_: (0,) * nd, pipeline_mode=pl.Buffered(1))


def _dot_nt(a, b):
    return lax.dot_general(a, b, (((1,), (1,)), ((), ())), preferred_element_type=F32)


def _mod_kernel(c_ref, w_ref, b_ref, o_ref):
    a = jax.nn.silu(c_ref[...]).astype(BF16)
    o_ref[...] = jnp.dot(a, w_ref[...].astype(BF16), preferred_element_type=F32) + b_ref[...]


def _modulation(cs, mod_w, mod_b, tn=1024):
    depth, d, n = mod_w.shape
    r = cs.shape[0]
    return pl.pallas_call(
        _mod_kernel,
        name="modulation",
        grid=(depth, n // tn),
        in_specs=[pl.BlockSpec((r, d), lambda l, j: (0, 0)),
                  pl.BlockSpec((None, d, tn), lambda l, j: (l, 0, j)),
                  pl.BlockSpec((None, 1, tn), lambda l, j: (l, 0, j))],
        out_specs=pl.BlockSpec((None, r, tn), lambda l, j: (l, 0, j)),
        out_shape=jax.ShapeDtypeStruct((depth, r, n), F32),
        compiler_params=_params(2),
    )(cs, mod_w, mod_b.reshape(depth, 1, n))


def _ffn_kernel(h_ref, mod_ref, gpre_ref, gpost_ref, w13_ref, w2_ref, o_ref, acc_ref, *, j, fc):
    x = h_ref[...]
    shift = mod_ref[3 * j:3 * j + 1, :]
    scale = mod_ref[3 * j + 1:3 * j + 2, :]
    gate = mod_ref[3 * j + 2:3 * j + 3, :]
    a = (_rms(x, gpre_ref[...]) * (1.0 + scale) + shift).astype(BF16)
    f = w2_ref.shape[0]
    for c in range(f // fc):
        g = jnp.dot(a, w13_ref[:, c * fc:(c + 1) * fc], preferred_element_type=F32)
        u = jnp.dot(a, w13_ref[:, f + c * fc:f + (c + 1) * fc], preferred_element_type=F32)
        act = (jax.nn.silu(g) * u).astype(BF16)
        contrib = jnp.dot(act, w2_ref[c * fc:(c + 1) * fc, :], preferred_element_type=F32)
        if c == 0:
            acc_ref[...] = contrib
        else:
            acc_ref[...] += contrib
    o_ref[...] = x + FFN_RES * gate * _rms(acc_ref[...], gpost_ref[...])


def _ffn(h, mod, j, g_pre, g_post, w13, w2, tm):
    b, s, d = h.shape
    f = w2.shape[0]
    fc = 256 if f % 256 == 0 else f
    tm = min(tm, s)
    return pl.pallas_call(
        functools.partial(_ffn_kernel, j=j, fc=fc),
        name="ffn",
        grid=(b, s // tm),
        in_specs=[pl.BlockSpec((None, tm, d), lambda bi, i: (bi, i, 0)),
                  pl.BlockSpec((None, N_MOD, d), lambda bi, i: (bi, 0, 0)),
                  _const_spec((1, d)), _const_spec((1, d)),
                  _const_spec(w13.shape), _const_spec(w2.shape)],
        out_specs=pl.BlockSpec((None, tm, d), lambda bi, i: (bi, i, 0)),
        out_shape=jax.ShapeDtypeStruct((b, s, d), F32),
        scratch_shapes=[pltpu.VMEM((tm, d), F32)],
        compiler_params=_params(2),
    )(h, mod, g_pre.reshape(1, d), g_post.reshape(1, d), w13, w2)


_O_CQ = 0
_O_CKV = _O_CQ + MLA_Q_LORA
_O_KPE = _O_CKV + MLA_KV_LORA
_O_KPE_SW = _O_KPE + LANES
_O_QS = _O_KPE_SW + LANES
_O_QS_SW = _O_QS + SWA_HEADS * SWA_HEAD_DIM
_O_KS = _O_QS_SW + SWA_HEADS * SWA_HEAD_DIM
_O_KS_SW = _O_KS + SWA_KV_HEADS * LANES
_O_VS = _O_KS_SW + SWA_KV_HEADS * LANES
_W1_COLS = _O_VS + SWA_KV_HEADS * LANES
_HL = MLA_HEADS * LANES
_QS_W = SWA_HEADS * SWA_HEAD_DIM
_KS_W = SWA_KV_HEADS * LANES


def _attn_proj_kernel(h_ref, mod_ref, gpre_ref, w1_ref, qn_ref, wq_ref, kvn_ref, wk_ref, wvt_ref,
                      ca_ref, sa_ref, cb_ref, sb_ref,
                      qa_ref, ka_ref, vat_ref, qs_ref, ks_ref, vs_ref):
    x = h_ref[...]
    shift = mod_ref[3:4, :]
    scale = mod_ref[4:5, :]
    a = (_rms(x, gpre_ref[...]) * (1.0 + scale) + shift).astype(BF16)
    p = jnp.dot(a, w1_ref[...], preferred_element_type=F32)
    ca, sa, cb, sb = ca_ref[...], sa_ref[...], cb_ref[...], sb_ref[...]
    lane = lax.broadcasted_iota(jnp.int32, (1, LANES), 1)
    ones_col = (lane == MLA_V).astype(F32)

    cqn = _rms(p[:, _O_CQ:_O_CQ + MLA_Q_LORA], qn_ref[...]).astype(BF16)
    q2 = jnp.dot(cqn, wq_ref[...], preferred_element_type=F32)
    ckvn = _rms(p[:, _O_CKV:_O_CKV + MLA_KV_LORA], kvn_ref[...]).astype(BF16)
    kn = jnp.dot(ckvn, wk_ref[...], preferred_element_type=F32)
    kpe = p[:, _O_KPE:_O_KPE + LANES] * ca + p[:, _O_KPE_SW:_O_KPE_SW + LANES] * sa
    for hd in range(MLA_HEADS):
        lo, hi = hd * LANES, (hd + 1) * LANES
        q = q2[:, lo:hi] * ca + q2[:, _HL + lo:_HL + hi] * sa
        qa_ref[:, lo:hi] = (q * MLA_SCALE).astype(BF16)
        ka_ref[:, lo:hi] = (kn[:, lo:hi] + kpe).astype(BF16)
    vt = _dot_nt(wvt_ref[...], ckvn)
    row = lax.broadcasted_iota(jnp.int32, (_HL, 1), 0)
    vat_ref[...] = (vt + (jnp.bitwise_and(row, LANES - 1) == MLA_V).astype(F32)).astype(BF16)

    for t in range(_QS_W // LANES):
        lo, hi = t * LANES, (t + 1) * LANES
        q = p[:, _O_QS + lo:_O_QS + hi] * cb + p[:, _O_QS_SW + lo:_O_QS_SW + hi] * sb
        qs_ref[:, lo:hi] = (q * SWA_SCALE).astype(BF16)
    for t in range(SWA_KV_HEADS):
        lo, hi = t * LANES, (t + 1) * LANES
        k = p[:, _O_KS + lo:_O_KS + hi] * cb + p[:, _O_KS_SW + lo:_O_KS_SW + hi] * sb
        ks_ref[:, lo:hi] = k.astype(BF16)
        vs_ref[:, lo:hi] = (p[:, _O_VS + lo:_O_VS + hi] + ones_col).astype(BF16)


def _attn_project(h, mod, g_pre, w1, q_norm, wq, kv_norm, wk, wvt, tabs, tm):
    b, s, d = h.shape
    tm = min(tm, s)
    row = lambda bi, i: (bi, i, 0)
    tab = pl.BlockSpec((tm, LANES), lambda bi, i: (i, 0))
    tok = lambda w: (pl.BlockSpec((None, tm, w), row), jax.ShapeDtypeStruct((b, s, w), BF16))
    chan = (pl.BlockSpec((None, _HL, tm), lambda bi, i: (bi, 0, i)), jax.ShapeDtypeStruct((b, _HL, s), BF16))
    outs = [tok(_HL), tok(_HL), chan, tok(_QS_W), tok(_KS_W), tok(_KS_W)]
    return pl.pallas_call(
        _attn_proj_kernel,
        name="attn_project",
        grid=(b, s // tm),
        in_specs=[pl.BlockSpec((None, tm, d), row),
                  pl.BlockSpec((None, N_MOD, d), lambda bi, i: (bi, 0, 0)),
                  _const_spec((1, d)), _const_spec(w1.shape),
                  _const_spec((1, MLA_Q_LORA)), _const_spec(wq.shape),
                  _const_spec((1, MLA_KV_LORA)), _const_spec(wk.shape), _const_spec(wvt.shape),
                  tab, tab, tab, tab],
        out_specs=[o[0] for o in outs],
        out_shape=[o[1] for o in outs],
        compiler_params=_params(2),
    )(h, mod, g_pre.reshape(1, d), w1, q_norm.reshape(1, -1), wq, kv_norm.reshape(1, -1), wk, wvt, *tabs)


def _mla_kernel(*refs, n_lat_chunks, tk):
    if n_lat_chunks:
        q_ref, kc_ref, vct_ref, kl_ref, vlt_ref, o_ref = refs
    else:
        q_ref, kc_ref, vct_ref, o_ref = refs
    q = q_ref[...]

    def update(k, vt, m, acc):
        st = _dot_nt(k, q)
        m_blk = jnp.max(st, axis=0, keepdims=True)
        m_new = m_blk if m is None else jnp.maximum(m, m_blk)
        pt = jnp.exp(st - m_new).astype(BF16)
        pv = jnp.dot(vt, pt, preferred_element_type=F32)
        acc = pv if acc is None else acc * jnp.exp(m - m_new) + pv
        return m_new, acc

    m, acc = update(kc_ref[...], vct_ref[...], None, None)
    for j in range(n_lat_chunks):
        m, acc = update(kl_ref[j * tk:(j + 1) * tk, :], vlt_ref[:, j * tk:(j + 1) * tk], m, acc)
    o_ref[...] = (acc / acc[MLA_V:MLA_V + 1, :]).T.astype(o_ref.dtype)


def _mla(q, kc, vct, kl, vlt, tq, tk):
    b, s, _ = q.shape
    n_ctx = kc.shape[1]
    tq = min(tq, s)
    qspec = pl.BlockSpec((None, tq, LANES), lambda bi, h, i: (bi, i, h))
    in_specs = [qspec,
                pl.BlockSpec((None, n_ctx, LANES), lambda bi, h, i: (bi, 0, h)),
                pl.BlockSpec((None, LANES, n_ctx), lambda bi, h, i: (bi, h, 0))]
    args = [q, kc, vct]
    n_lat_chunks = 0
    if kl is not None:
        n_lat = kl.shape[1]
        tk = min(tk, n_lat)
        n_lat_chunks = n_lat // tk
        in_specs += [pl.BlockSpec((None, n_lat, LANES), lambda bi, h, i: (bi, 0, h)),
                     pl.BlockSpec((None, LANES, n_lat), lambda bi, h, i: (bi, h, 0))]
        args += [kl, vlt]
    return pl.pallas_call(
        functools.partial(_mla_kernel, n_lat_chunks=n_lat_chunks, tk=tk),
        name="mla_lat" if n_lat_chunks else "mla_ctx",
        grid=(b, MLA_HEADS, s // tq),
        in_specs=in_specs,
        out_specs=qspec,
        out_shape=jax.ShapeDtypeStruct((b, s, _HL), BF16),
        compiler_params=_params(3),
    )(*args)


def _swa_kernel(*refs, windowed, n_blocks):
    if windowed:
        (sink_ref, q_ref, kc_ref, vc_ref, kp_ref, k0_ref, kn_ref,
         vp_ref, v0_ref, vn_ref, o_ref) = refs
    else:
        sink_ref, q_ref, kc_ref, vc_ref, o_ref = refs
    i = pl.program_id(1)
    tq = q_ref.shape[0]
    rows = SWA_GROUP * tq
    lane = lax.broadcasted_iota(jnp.int32, (1, LANES), 1)
    half_mask = [(lane < SWA_HEAD_DIM), (lane >= SWA_HEAD_DIM)]
    if windowed:
        r = jnp.bitwise_and(lax.broadcasted_iota(jnp.int32, (rows, tq), 0), tq - 1)
        c = lax.broadcasted_iota(jnp.int32, (rows, tq), 1)
        prev_ok = c >= r + jnp.where(i > 0, 0, tq)
        next_ok = c <= r - jnp.where(i < n_blocks - 1, 0, tq)
    for kv in range(SWA_KV_HEADS):
        lo, hi = kv * LANES, (kv + 1) * LANES
        qs, sk = [], []
        for g in range(SWA_GROUP):
            hd = kv * SWA_GROUP + g
            t = hd // 2
            qt = q_ref[:, t * LANES:(t + 1) * LANES]
            qs.append(jnp.where(half_mask[hd % 2], qt, jnp.zeros_like(qt)))
            sk.append(jnp.full((tq, 1), sink_ref[hd], F32))
        q4 = jnp.concatenate(qs, axis=0)
        sink = jnp.concatenate(sk, axis=0)
        scores = [_dot_nt(q4, kc_ref[:, lo:hi])]
        vals = [vc_ref[:, lo:hi]]
        if windowed:
            sp = jnp.where(prev_ok, _dot_nt(q4, kp_ref[:, lo:hi]), NEG_INF)
            s0 = _dot_nt(q4, k0_ref[:, lo:hi])
            sn = jnp.where(next_ok, _dot_nt(q4, kn_ref[:, lo:hi]), NEG_INF)
            scores += [sp, s0, sn]
            vals += [vp_ref[:, lo:hi], v0_ref[:, lo:hi], vn_ref[:, lo:hi]]
        m = sink
        for s_ in scores:
            m = jnp.maximum(m, jnp.max(s_, axis=-1, keepdims=True))
        acc = None
        for s_, v_ in zip(scores, vals):
            pv = jnp.dot(jnp.exp(s_ - m).astype(BF16), v_, preferred_element_type=F32)
            acc = pv if acc is None else acc + pv
        denom = acc[:, SWA_HEAD_DIM:SWA_HEAD_DIM + 1] + jnp.exp(sink - m)
        o = (acc / denom).astype(o_ref.dtype)
        for g in range(SWA_GROUP):
            hd = kv * SWA_GROUP + g
            o_ref[:, hd * LANES:(hd + 1) * LANES] = o[g * tq:(g + 1) * tq, :]


def _swa(sink, q, kc, vc, kl, vl):
    b, s, _ = q.shape
    n_ctx = kc.shape[1]
    windowed = kl is not None
    tq = Q_BLOCK if windowed else s
    nb = s // tq
    cspec = pl.BlockSpec((None, n_ctx, _KS_W), lambda bi, i: (bi, 0, 0))
    in_specs = [pl.BlockSpec(memory_space=pltpu.SMEM),
                pl.BlockSpec((None, tq, _QS_W), lambda bi, i: (bi, i, 0)), cspec, cspec]
    args = [sink, q, kc, vc]
    if windowed:
        prev = pl.BlockSpec((None, tq, _KS_W), lambda bi, i: (bi, jnp.maximum(i - 1, 0), 0))
        cur = pl.BlockSpec((None, tq, _KS_W), lambda bi, i: (bi, i, 0))
        nxt = pl.BlockSpec((None, tq, _KS_W), lambda bi, i: (bi, jnp.minimum(i + 1, nb - 1), 0))
        in_specs += [prev, cur, nxt, prev, cur, nxt]
        args += [kl, kl, kl, vl, vl, vl]
    return pl.pallas_call(
        functools.partial(_swa_kernel, windowed=windowed, n_blocks=nb),
        name="swa_lat" if windowed else "swa_ctx",
        grid=(b, nb),
        in_specs=in_specs,
        out_specs=pl.BlockSpec((None, tq, SWA_HEADS * LANES), lambda bi, i: (bi, i, 0)),
        out_shape=jax.ShapeDtypeStruct((b, s, SWA_HEADS * LANES), BF16),
        compiler_params=_params(2),
    )(*args)


def _attn_out_kernel(oa_ref, ob_ref, h_ref, mod_ref, gpost_ref, wa_ref, wb_ref, o_ref):
    y = jnp.dot(oa_ref[...], wa_ref[...], preferred_element_type=F32)
    y = y + jnp.dot(ob_ref[...], wb_ref[...], preferred_element_type=F32)
    o_ref[...] = h_ref[...] + mod_ref[5:6, :] * _rms(y, gpost_ref[...])


def _attn_out(oa, ob, h, mod, g_post, wa, wb, tm):
    b, s, d = h.shape
    tm = min(tm, s)
    row = lambda bi, i: (bi, i, 0)
    return pl.pallas_call(
        _attn_out_kernel,
        name="attn_out",
        grid=(b, s // tm),
        in_specs=[pl.BlockSpec((None, tm, oa.shape[2]), row),
                  pl.BlockSpec((None, tm, ob.shape[2]), row),
                  pl.BlockSpec((None, tm, d), row),
                  pl.BlockSpec((None, N_MOD, d), lambda bi, i: (bi, 0, 0)),
                  _const_spec((1, d)), _const_spec(wa.shape), _const_spec(wb.shape)],
        out_specs=pl.BlockSpec((None, tm, d), row),
        out_shape=jax.ShapeDtypeStruct((b, s, d), F32),
        compiler_params=_params(2),
    )(oa, ob, h, mod, g_post.reshape(1, d), wa, wb)


def _s5_in_kernel(h_ref, mod_ref, gpre_ref, wt_ref, o_ref):
    x = h_ref[...]
    a = (_rms(x, gpre_ref[...]) * (1.0 + mod_ref[4:5, :]) + mod_ref[3:4, :]).astype(BF16)
    o_ref[...] = _dot_nt(wt_ref[...], a)


def _s5_in(h, mod, g_pre, w_in_t, tm):
    b, s, d = h.shape
    w = w_in_t.shape[0]
    tm = min(tm, s)
    return pl.pallas_call(
        _s5_in_kernel,
        name="s5_in",
        grid=(b, s // tm),
        in_specs=[pl.BlockSpec((None, tm, d), lambda bi, i: (bi, i, 0)),
                  pl.BlockSpec((None, N_MOD, d), lambda bi, i: (bi, 0, 0)),
                  _const_spec((1, d)), _const_spec(w_in_t.shape)],
        out_specs=pl.BlockSpec((None, w, tm), lambda bi, i: (bi, 0, i)),
        out_shape=jax.ShapeDtypeStruct((b, w, s), F32),
        compiler_params=_params(2),
    )(h, mod, g_pre.reshape(1, d), w_in_t)


def _cmul(x, p1, p2):
    return x * p1 + pltpu.roll(x, S5_STATE, 1) * p2


def _s5_core_kernel(u_ref, uc_ref, kk_ref, et_ref, ft_ref, ap_ref, dsk_ref, o_ref,
                    ef_ref, eb_ref, xf_ref, xb_ref, m_ref, *, nb, nc, ncc):
    h = S5_GROUP
    rows = nb * nc

    @pl.when(pl.program_id(1) == 0)
    def _():
        mi = lax.broadcasted_iota(jnp.int32, (S5_CHUNK, S5_CHUNK), 0)
        li = lax.broadcasted_iota(jnp.int32, (S5_CHUNK, S5_CHUNK), 1)
        causal = li >= mi

        def build(i, carry):
            r0 = pl.multiple_of(i * S5_CHUNK, S5_CHUNK)
            for j in range(h):
                lags = kk_ref[i, j:j + 1, :]
                fwd = pltpu.roll(jnp.broadcast_to(lags[:, S5_CHUNK:], (S5_CHUNK, S5_CHUNK)), 0, 1,
                                 stride=1, stride_axis=0)
                bwd = pltpu.roll(jnp.broadcast_to(lags[:, :S5_CHUNK], (S5_CHUNK, S5_CHUNK)), 0, 1,
                                 stride=1, stride_axis=0)
                m_ref[pl.ds(r0, S5_CHUNK), j * S5_CHUNK:(j + 1) * S5_CHUNK] = (
                    jnp.where(causal, fwd, bwd).astype(BF16))
            return carry

        lax.fori_loop(0, h, build, 0)

    u32 = jnp.concatenate([u_ref[:, i].reshape(rows, S5_CHUNK) for i in range(h)], axis=1)
    ub = u32.astype(BF16)
    et = et_ref[...]
    ein = jnp.dot(ub, et, preferred_element_type=F32)
    ef_ref[...] = ein[:, :LANES]
    eb_ref[...] = ein[:, LANES:]
    pf1, pf2 = ap_ref[0:1, :LANES], ap_ref[1:2, :LANES]
    pb1, pb2 = ap_ref[0:1, LANES:], ap_ref[1:2, LANES:]

    xf = jnp.zeros((nb, LANES), F32)
    xb = jnp.zeros((nb, LANES), F32)
    for c in range(ncc):
        ecf = jnp.dot(uc_ref[c].astype(BF16), et, preferred_element_type=F32)
        ecb = jnp.dot(uc_ref[ncc - 1 - c].astype(BF16), et, preferred_element_type=F32)
        xf = _cmul(xf, pf1, pf2) + ecf[:, :LANES]
        xb = _cmul(xb, pb1, pb2) + ecb[:, LANES:]

    def body(t, carry):
        xf, xb = carry
        cf = pl.ds(t, nb, stride=nc)
        cb = pl.ds(nc - 1 - t, nb, stride=nc)
        xf_ref[cf, :] = xf
        xb_ref[cb, :] = xb
        xf = _cmul(xf, pf1, pf2) + ef_ref[cf, :]
        xb = _cmul(xb, pb1, pb2) + eb_ref[cb, :]
        return xf, xb

    lax.fori_loop(0, nc, body, (xf, xb))
    xs = jnp.concatenate([xf_ref[...], xb_ref[...]], axis=1).astype(BF16)
    y = jnp.dot(ub, m_ref[...], preferred_element_type=F32)
    y = y + jnp.dot(xs, ft_ref[...], preferred_element_type=F32)
    y = y + u32 * dsk_ref[...]
    for j in range(h):
        o_ref[:, j] = y[:, j * S5_CHUNK:(j + 1) * S5_CHUNK].reshape(nb, nc, S5_CHUNK)


def _s5_core(u_t, uc, kk, et, ft, ap, dsk, nb):
    b, w, s = u_t.shape
    g = w // S5_GROUP
    nc = s // S5_CHUNK
    ncc = uc.shape[2]
    hl = S5_GROUP * S5_CHUNK
    u5 = u_t.reshape(b, g, S5_GROUP, nc, S5_CHUNK)
    blk = pl.BlockSpec((nb, None, S5_GROUP, nc, S5_CHUNK), lambda gi, bi: (bi, gi, 0, 0, 0))
    per_g = lambda shape: pl.BlockSpec((None,) + shape, lambda gi, bi: (gi,) + (0,) * len(shape))
    out = pl.pallas_call(
        functools.partial(_s5_core_kernel, nb=nb, nc=nc, ncc=ncc),
        name="s5_core",
        grid=(g, b // nb),
        in_specs=[blk,
                  pl.BlockSpec((None, None, ncc, nb, hl), lambda gi, bi: (gi, bi, 0, 0, 0)),
                  per_g((S5_GROUP, S5_GROUP, 2 * S5_CHUNK)),
                  per_g((hl, 2 * LANES)), per_g((2 * LANES, hl)),
                  per_g((2, 2 * LANES)), per_g((1, hl))],
        out_specs=blk,
        out_shape=jax.ShapeDtypeStruct(u5.shape, F32),
        scratch_shapes=[pltpu.VMEM((nb * nc, LANES), F32) for _ in range(4)]
        + [pltpu.VMEM((hl, hl), BF16)],
        compiler_params=_params(2),
    )(u5, uc, kk, et, ft, ap, dsk)
    return out.reshape(b, w, s)


def _s5_out_kernel(y_ref, h_ref, mod_ref, gpost_ref, wt_ref, o_ref):
    d = h_ref.shape[1]
    gy = jax.nn.gelu(y_ref[...]).astype(BF16)
    z = jnp.dot(wt_ref[...], gy, preferred_element_type=F32)
    v = (z[:d, :] * jax.nn.sigmoid(z[d:, :])).T
    o_ref[...] = h_ref[...] + mod_ref[5:6, :] * _rms(v, gpost_ref[...])


def _s5_out(y_t, h, mod, g_post, w_glu_t, tm):
    b, s, d = h.shape
    w = y_t.shape[1]
    tm = min(tm, s)
    return pl.pallas_call(
        _s5_out_kernel,
        name="s5_out",
        grid=(b, s // tm),
        in_specs=[pl.BlockSpec((None, w, tm), lambda bi, i: (bi, 0, i)),
                  pl.BlockSpec((None, tm, d), lambda bi, i: (bi, i, 0)),
                  pl.BlockSpec((None, N_MOD, d), lambda bi, i: (bi, 0, 0)),
                  _const_spec((1, d)), _const_spec(w_glu_t.shape)],
        out_specs=pl.BlockSpec((None, tm, d), lambda bi, i: (bi, i, 0)),
        out_shape=jax.ShapeDtypeStruct((b, s, d), F32),
        compiler_params=_params(2),
    )(y_t, h, mod, g_post.reshape(1, d), w_glu_t)


def _rope_angles(n, d_rot):
    t = jnp.arange(n)
    rows, cols = t // GRID_W, t % GRID_W
    d_axis = d_rot // 2
    inv = ROPE_BASE ** (-jnp.arange(0, d_axis, 2, dtype=F32) / d_axis)
    ang = jnp.concatenate([rows.astype(F32)[:, None] * inv, cols.astype(F32)[:, None] * inv], axis=-1)
    return jnp.cos(ang), jnp.sin(ang)


def _rope_tables(n, rotate):
    pad = LANES - MLA_NOPE - MLA_ROPE
    if rotate:
        ca, sa = _rope_angles(n, MLA_ROPE)
        cb, sb = _rope_angles(n, SWA_HEAD_DIM)
    else:
        ca, sa = jnp.ones((n, MLA_ROPE // 2), F32), jnp.zeros((n, MLA_ROPE // 2), F32)
        cb, sb = jnp.ones((n, SWA_HEAD_DIM // 2), F32), jnp.zeros((n, SWA_HEAD_DIM // 2), F32)
    one, zero = jnp.ones((n, MLA_NOPE), F32), jnp.zeros((n, MLA_NOPE), F32)
    zpad = jnp.zeros((n, pad), F32)
    return (jnp.concatenate([one, ca, ca, zpad], axis=1), jnp.concatenate([zero, sa, sa, zpad], axis=1),
            jnp.concatenate([cb] * 4, axis=1), jnp.concatenate([sb] * 4, axis=1))


def _rot_partner(w, half):
    return jnp.concatenate([-w[..., half:], w[..., :half]], axis=-1)


def _attn_weights(w_in, w_uq, w_ukv, w_out):
    d = w_in.shape[0]
    sizes = [MLA_Q_LORA, MLA_KV_LORA, MLA_ROPE, SWA_HEADS * SWA_HEAD_DIM,
             SWA_KV_HEADS * SWA_HEAD_DIM, SWA_KV_HEADS * SWA_HEAD_DIM]
    cq, ckv, kpe, qs, ks, vs = jnp.split(w_in, [int(v) for v in np.cumsum(sizes)[:-1]], axis=1)
    pad_a = LANES - MLA_NOPE - MLA_ROPE
    z = lambda *shape: jnp.zeros(shape, F32)
    kpe_blk = jnp.concatenate([z(d, MLA_NOPE), kpe, z(d, pad_a)], axis=1)
    kpe_sw = jnp.concatenate([z(d, MLA_NOPE), _rot_partner(kpe, MLA_ROPE // 2), z(d, pad_a)], axis=1)
    qs3 = qs.reshape(d, SWA_HEADS, SWA_HEAD_DIM)
    qs_sw = _rot_partner(qs3, SWA_HEAD_DIM // 2).reshape(d, -1)
    ks3 = ks.reshape(d, SWA_KV_HEADS, SWA_HEAD_DIM)
    ks_sw3 = _rot_partner(ks3, SWA_HEAD_DIM // 2)
    dup = lambda t: jnp.concatenate([t, t], axis=-1).reshape(d, -1)
    vs3 = vs.reshape(d, SWA_KV_HEADS, SWA_HEAD_DIM)
    vs_pad = jnp.concatenate([vs3, jnp.zeros_like(vs3)], axis=-1).reshape(d, -1)
    w1 = jnp.concatenate([cq, ckv, kpe_blk, kpe_sw, qs, qs_sw, dup(ks3), dup(ks_sw3), vs_pad], axis=1)
    assert w1.shape[1] == _W1_COLS

    ql = w_uq.shape[0]
    uq = w_uq.reshape(ql, MLA_HEADS, MLA_NOPE + MLA_ROPE)
    nope, pe = uq[..., :MLA_NOPE], uq[..., MLA_NOPE:]
    zq = jnp.zeros((ql, MLA_HEADS, pad_a), F32)
    wq_blk = jnp.concatenate([nope, pe, zq], axis=-1).reshape(ql, -1)
    wq_sw = jnp.concatenate([jnp.zeros_like(nope), _rot_partner(pe, MLA_ROPE // 2), zq], axis=-1).reshape(ql, -1)
    wq = jnp.concatenate([wq_blk, wq_sw], axis=1)

    kl = w_ukv.shape[0]
    ukv = w_ukv.reshape(kl, MLA_HEADS, MLA_NOPE + MLA_V)
    kn, vv = ukv[..., :MLA_NOPE], ukv[..., MLA_NOPE:]
    wk = jnp.concatenate([kn, jnp.zeros((kl, MLA_HEADS, LANES - MLA_NOPE), F32)], axis=-1).reshape(kl, -1)
    wv = jnp.concatenate([vv, jnp.zeros((kl, MLA_HEADS, LANES - MLA_V), F32)], axis=-1).reshape(kl, -1)
    wvt = wv.T

    dm = w_out.shape[1]
    na = MLA_HEADS * MLA_V
    oa = w_out[:na].reshape(MLA_HEADS, MLA_V, dm)
    ob = w_out[na:].reshape(SWA_HEADS, SWA_HEAD_DIM, dm)
    wa = jnp.concatenate([oa, jnp.zeros((MLA_HEADS, LANES - MLA_V, dm), F32)], axis=1).reshape(-1, dm)
    wb = jnp.concatenate([ob, jnp.zeros((SWA_HEADS, LANES - SWA_HEAD_DIM, dm), F32)], axis=1).reshape(-1, dm)
    return tuple(t.astype(BF16) for t in (w1, wq, wk, wvt, wa, wb))


def _complex_powers(a_re, a_im, n):
    pr, pi = jnp.ones_like(a_re)[None], jnp.zeros_like(a_im)[None]
    sr, si = a_re, a_im
    while pr.shape[0] < n + 1:
        nr, ni = pr * sr - pi * si, pr * si + pi * sr
        pr, pi = jnp.concatenate([pr, nr], 0), jnp.concatenate([pi, ni], 0)
        sr, si = sr * sr - si * si, 2.0 * sr * si
    return pr[:n + 1], pi[:n + 1]


def _s5_operators(lam_re, lam_im, b_re, b_im, c_re, c_im, log_step):
    hi = lax.Precision.HIGHEST
    ln = S5_CHUNK
    g, p = lam_re.shape[1:]
    hh = b_re.shape[-1]
    ks, ets, fts, aps = [], [], [], []
    for dr in range(2):
        lre = jnp.minimum(lam_re[dr], S5_MAX_RE)
        lim = lam_im[dr]
        dt = jnp.exp(log_step[dr])[:, None]
        mag = jnp.exp(lre * dt)
        a_re, a_im = mag * jnp.cos(lim * dt), mag * jnp.sin(lim * dt)
        den = lre * lre + lim * lim
        f_re = ((a_re - 1.0) * lre + a_im * lim) / den
        f_im = (a_im * lre - (a_re - 1.0) * lim) / den
        bb_re = f_re[..., None] * b_re[dr] - f_im[..., None] * b_im[dr]
        bb_im = f_re[..., None] * b_im[dr] + f_im[..., None] * b_re[dr]
        cr, ci = c_re[dr], c_im[dr]
        pr, pi = _complex_powers(a_re, a_im, ln)
        cb_re = jnp.einsum('gjp,gpi->gpji', cr, bb_re) - jnp.einsum('gjp,gpi->gpji', ci, bb_im)
        cb_im = jnp.einsum('gjp,gpi->gpji', cr, bb_im) + jnp.einsum('gjp,gpi->gpji', ci, bb_re)
        k = (jnp.einsum('dgp,gpji->gijd', pr[:ln], cb_re, precision=hi)
             - jnp.einsum('dgp,gpji->gijd', pi[:ln], cb_im, precision=hi))
        ks.append(k)
        er, ei = (pr[:ln][::-1], pi[:ln][::-1]) if dr == 0 else (pr[:ln], pi[:ln])
        e_re = jnp.einsum('mgp,gpi->gimp', er, bb_re) - jnp.einsum('mgp,gpi->gimp', ei, bb_im)
        e_im = jnp.einsum('mgp,gpi->gimp', er, bb_im) + jnp.einsum('mgp,gpi->gimp', ei, bb_re)
        ets.append(jnp.concatenate([e_re, e_im], axis=-1).reshape(g, hh * ln, 2 * p))
        fr, fi = (pr[1:], pi[1:]) if dr == 0 else (pr[1:][::-1], pi[1:][::-1])
        f_xre = jnp.einsum('gjp,lgp->gpjl', cr, fr) - jnp.einsum('gjp,lgp->gpjl', ci, fi)
        f_xim = -(jnp.einsum('gjp,lgp->gpjl', cr, fi) + jnp.einsum('gjp,lgp->gpjl', ci, fr))
        fts.append(jnp.concatenate([f_xre, f_xim], axis=1).reshape(g, 2 * p, hh * ln))
        aps.append(jnp.stack([jnp.concatenate([pr[ln], pr[ln]], axis=-1),
                              jnp.concatenate([-pi[ln], pi[ln]], axis=-1)], axis=1))
    kf, kb = ks
    k0 = kf[..., :1] + kb[..., :1]
    kk = jnp.concatenate([jnp.zeros_like(k0), kb[..., :0:-1], k0, kf[..., 1:]], axis=-1)
    et = jnp.concatenate(ets, axis=-1).astype(BF16)
    ft = jnp.concatenate(fts, axis=1).astype(BF16)
    ap = jnp.concatenate(aps, axis=-1)
    return kk, et, ft, ap


def kernel(x, c, ctx, c_ctx, mod_w, mod_b, norm_pre, norm_post, ffn_w13, ffn_w2,
           attn_w_in, mla_q_norm, mla_w_uq, mla_kv_norm, mla_w_ukv, swa_sink, attn_w_out,
           s5_w_in, s5_lambda_re, s5_lambda_im, s5_b_re, s5_b_im, s5_c_re, s5_c_im,
           s5_log_step, s5_d, s5_w_glu):
    b, n, d = x.shape
    n_ctx = ctx.shape[1]
    depth = mod_w.shape[0]
    tm = 512

    rows = -(-(b + 1) // 8) * 8
    cs = jnp.concatenate([c, c_ctx[None], jnp.zeros((rows - b - 1, d), F32)], axis=0)
    mods = _modulation(cs, mod_w, mod_b)
    w13 = ffn_w13.astype(BF16)
    w2 = ffn_w2.astype(BF16)

    h_lat, h_ctx = x, ctx
    for l in range(depth):
        last = l == depth - 1
        m_lat = mods[l, :b].reshape(b, N_MOD, d)
        m_ctx = jnp.broadcast_to(mods[l, b].reshape(1, N_MOD, d), (b, N_MOD, d))
        h_lat = _ffn(h_lat, m_lat, 0, norm_pre[l, 0], norm_post[l, 0], w13[l, 0], w2[l, 0], tm)
        h_ctx = _ffn(h_ctx, m_ctx, 0, norm_pre[l, 0], norm_post[l, 0], w13[l, 0], w2[l, 0], tm)
        if l % 2 == 0:
            e = l // 2
            w1, wq, wk, wvt, wa, wb = _attn_weights(attn_w_in[e], mla_w_uq[e], mla_w_ukv[e], attn_w_out[e])
            proj = functools.partial(_attn_project, g_pre=norm_pre[l, 1], w1=w1, q_norm=mla_q_norm[e],
                                     wq=wq, kv_norm=mla_kv_norm[e], wk=wk, wvt=wvt, tm=tm)
            qa_c, ka_c, va_c, qs_c, ks_c, vs_c = proj(h_ctx, m_ctx, tabs=_rope_tables(n_ctx, False))
            qa_l, ka_l, va_l, qs_l, ks_l, vs_l = proj(h_lat, m_lat, tabs=_rope_tables(n, True))
            o_a = _mla(qa_l, ka_c, va_c, ka_l, va_l, tq=256, tk=1024)
            o_b = _swa(swa_sink[e], qs_l, ks_c, vs_c, ks_l, vs_l)
            h_lat = _attn_out(o_a, o_b, h_lat, m_lat, norm_post[l, 1], wa, wb, tm)
            if not last:
                o_a_c = _mla(qa_c, ka_c, va_c, None, None, tq=256, tk=1024)
                o_b_c = _swa(swa_sink[e], qs_c, ks_c, vs_c, None, None)
                h_ctx = _attn_out(o_a_c, o_b_c, h_ctx, m_ctx, norm_post[l, 1], wa, wb, tm)
        else:
            o = l // 2
            assert last, "S5 context outputs are only needed when another layer follows"
            w_in_t = s5_w_in[o].T.astype(BF16)
            w_glu_t = s5_w_glu[o].T.astype(BF16)
            kk, et, ft, ap = _s5_operators(s5_lambda_re[o], s5_lambda_im[o], s5_b_re[o], s5_b_im[o],
                                             s5_c_re[o], s5_c_im[o], s5_log_step[o])
            width = s5_w_in.shape[2]
            g = width // S5_GROUP
            u_lat = _s5_in(h_lat, m_lat, norm_pre[l, 1], w_in_t, tm)
            u_ctx = _s5_in(h_ctx, m_ctx, norm_pre[l, 1], w_in_t, tm)
            ncc = n_ctx // S5_CHUNK
            nb = b // 2 if b % 2 == 0 else b
            uc = u_ctx.reshape(b // nb, nb, g, S5_GROUP, ncc, S5_CHUNK).transpose(2, 0, 4, 1, 3, 5)
            uc = uc.reshape(g, b // nb, ncc, nb, S5_GROUP * S5_CHUNK)
            dsk = jnp.repeat(s5_d[o].astype(F32), S5_CHUNK).reshape(g, 1, S5_GROUP * S5_CHUNK)
            y_t = _s5_core(u_lat, uc, kk, et, ft, ap, dsk, nb=nb)
            h_lat = _s5_out(y_t, h_lat, m_lat, norm_post[l, 1], w_glu_t, tm)
        h_lat = _ffn(h_lat, m_lat, 2, norm_pre[l, 2], norm_post[l, 2], w13[l, 1], w2[l, 1], tm)
        if not last:
            h_ctx = _ffn(h_ctx, m_ctx, 2, norm_pre[l, 2], norm_post[l, 2], w13[l, 1], w2[l, 1], tm)
    return h_lat
```

```python
import functools
import math

import numpy as np
import jax
import jax.numpy as jnp
from jax import lax
from jax.experimental import pallas as pl
from jax.experimental.pallas import tpu as pltpu

F32 = jnp.float32
BF16 = jnp.bfloat16

LANES = 128
VMEM_LIMIT = 56 * 1024 * 1024

N_MOD = 9
FFN_RES = 0.5
EPS = 1e-6
ROPE_BASE = 10000.0
GRID_W = 64
NEG_INF = -1e30

MLA_HEADS = 8
MLA_Q_LORA = 256
MLA_KV_LORA = 128
MLA_NOPE = 64
MLA_ROPE = 32
MLA_V = 64
MLA_SCALE = (MLA_NOPE + MLA_ROPE) ** -0.5

SWA_HEADS = 8
SWA_KV_HEADS = 2
SWA_GROUP = SWA_HEADS // SWA_KV_HEADS
SWA_HEAD_DIM = 64
SWA_WINDOW = 128
SWA_SCALE = SWA_HEAD_DIM ** -0.5
Q_BLOCK = 128

S5_GROUP = 16
S5_STATE = 64
S5_CHUNK = 128
S5_MAX_RE = -1e-4


def _params(n_grid):
    return pltpu.CompilerParams(dimension_semantics=("arbitrary",) * n_grid,
                                vmem_limit_bytes=VMEM_LIMIT)


def _rms(x, g):
    return x * lax.rsqrt(jnp.mean(x * x, axis=-1, keepdims=True) + EPS) * g


def _const_spec(shape):
    nd = len(shape)
    return pl.BlockSpec(shape, lambda *_: (0,) * nd, pipeline_mode=pl.Buffered(1))


def _dot_nt(a, b):
    return lax.dot_general(a, b, (((1,), (1,)), ((), ())), preferred_element_type=F32)


def _mod_kernel(c_ref, w_ref, b_ref, o_ref):
    a = jax.nn.silu(c_ref[...]).astype(BF16)
    o_ref[...] = jnp.dot(a, w_ref[...].astype(BF16), preferred_element_type=F32) + b_ref[...]


def _modulation(cs, mod_w, mod_b, tn=1024):
    depth, d, n = mod_w.shape
    r = cs.shape[0]
    return pl.pallas_call(
        _mod_kernel,
        name="modulation",
        grid=(depth, n // tn),
        in_specs=[pl.BlockSpec((r, d), lambda l, j: (0, 0)),
                  pl.BlockSpec((None, d, tn), lambda l, j: (l, 0, j)),
                  pl.BlockSpec((None, 1, tn), lambda l, j: (l, 0, j))],
        out_specs=pl.BlockSpec((None, r, tn), lambda l, j: (l, 0, j)),
        out_shape=jax.ShapeDtypeStruct((depth, r, n), F32),
        compiler_params=_params(2),
    )(cs, mod_w, mod_b.reshape(depth, 1, n))


def _ffn_kernel(h_ref, mod_ref, gpre_ref, gpost_ref, w13_ref, w2_ref, o_ref, acc_ref, *, j, fc):
    x = h_ref[...]
    shift = mod_ref[3 * j:3 * j + 1, :]
    scale = mod_ref[3 * j + 1:3 * j + 2, :]
    gate = mod_ref[3 * j + 2:3 * j + 3, :]
    a = (_rms(x, gpre_ref[...]) * (1.0 + scale) + shift).astype(BF16)
    f = w2_ref.shape[0]
    for c in range(f // fc):
        g = jnp.dot(a, w13_ref[:, c * fc:(c + 1) * fc], preferred_element_type=F32)
        u = jnp.dot(a, w13_ref[:, f + c * fc:f + (c + 1) * fc], preferred_element_type=F32)
        act = (jax.nn.silu(g) * u).astype(BF16)
        contrib = jnp.dot(act, w2_ref[c * fc:(c + 1) * fc, :], preferred_element_type=F32)
        if c == 0:
            acc_ref[...] = contrib
        else:
            acc_ref[...] += contrib
    o_ref[...] = x + FFN_RES * gate * _rms(acc_ref[...], gpost_ref[...])


def _ffn(h, mod, j, g_pre, g_post, w13, w2, tm):
    b, s, d = h.shape
    f = w2.shape[0]
    fc = 256 if f % 256 == 0 else f
    tm = min(tm, s)
    return pl.pallas_call(
        functools.partial(_ffn_kernel, j=j, fc=fc),
        name="ffn",
        grid=(b, s // tm),
        in_specs=[pl.BlockSpec((None, tm, d), lambda bi, i: (bi, i, 0)),
                  pl.BlockSpec((None, N_MOD, d), lambda bi, i: (bi, 0, 0)),
                  _const_spec((1, d)), _const_spec((1, d)),
                  _const_spec(w13.shape), _const_spec(w2.shape)],
        out_specs=pl.BlockSpec((None, tm, d), lambda bi, i: (bi, i, 0)),
        out_shape=jax.ShapeDtypeStruct((b, s, d), F32),
        scratch_shapes=[pltpu.VMEM((tm, d), F32)],
        compiler_params=_params(2),
    )(h, mod, g_pre.reshape(1, d), g_post.reshape(1, d), w13, w2)


_O_CQ = 0
_O_CKV = _O_CQ + MLA_Q_LORA
_O_KPE = _O_CKV + MLA_KV_LORA
_O_KPE_SW = _O_KPE + LANES
_O_QS = _O_KPE_SW + LANES
_O_QS_SW = _O_QS + SWA_HEADS * SWA_HEAD_DIM
_O_KS = _O_QS_SW + SWA_HEADS * SWA_HEAD_DIM
_O_KS_SW = _O_KS + SWA_KV_HEADS * LANES
_O_VS = _O_KS_SW + SWA_KV_HEADS * LANES
_W1_COLS = _O_VS + SWA_KV_HEADS * LANES
_HL = MLA_HEADS * LANES
_QS_W = SWA_HEADS * SWA_HEAD_DIM
_KS_W = SWA_KV_HEADS * LANES


def _attn_proj_kernel(h_ref, mod_ref, gpre_ref, w1_ref, qn_ref, wq_ref, kvn_ref, wk_ref, wvt_ref,
                      ca_ref, sa_ref, cb_ref, sb_ref,
                      qa_ref, ka_ref, vat_ref, qs_ref, ks_ref, vs_ref):
    x = h_ref[...]
    shift = mod_ref[3:4, :]
    scale = mod_ref[4:5, :]
    a = (_rms(x, gpre_ref[...]) * (1.0 + scale) + shift).astype(BF16)
    p = jnp.dot(a, w1_ref[...], preferred_element_type=F32)
    ca, sa, cb, sb = ca_ref[...], sa_ref[...], cb_ref[...], sb_ref[...]
    lane = lax.broadcasted_iota(jnp.int32, (1, LANES), 1)
    ones_col = (lane == MLA_V).astype(F32)

    cqn = _rms(p[:, _O_CQ:_O_CQ + MLA_Q_LORA], qn_ref[...]).astype(BF16)
    q2 = jnp.dot(cqn, wq_ref[...], preferred_element_type=F32)
    ckvn = _rms(p[:, _O_CKV:_O_CKV + MLA_KV_LORA], kvn_ref[...]).astype(BF16)
    kn = jnp.dot(ckvn, wk_ref[...], preferred_element_type=F32)
    kpe = p[:, _O_KPE:_O_KPE + LANES] * ca + p[:, _O_KPE_SW:_O_KPE_SW + LANES] * sa
    for hd in range(MLA_HEADS):
        lo, hi = hd * LANES, (hd + 1) * LANES
        q = q2[:, lo:hi] * ca + q2[:, _HL + lo:_HL + hi] * sa
        qa_ref[:, lo:hi] = (q * (MLA_SCALE * math.log2(math.e))).astype(BF16)
        ka_ref[:, lo:hi] = (kn[:, lo:hi] + kpe).astype(BF16)
    vt = _dot_nt(wvt_ref[...], ckvn)
    row = lax.broadcasted_iota(jnp.int32, (_HL, 1), 0)
    vat_ref[...] = (vt + (jnp.bitwise_and(row, LANES - 1) == MLA_V).astype(F32)).astype(BF16)

    for t in range(_QS_W // LANES):
        lo, hi = t * LANES, (t + 1) * LANES
        q = p[:, _O_QS + lo:_O_QS + hi] * cb + p[:, _O_QS_SW + lo:_O_QS_SW + hi] * sb
        qs_ref[:, lo:hi] = (q * SWA_SCALE).astype(BF16)
    for t in range(SWA_KV_HEADS):
        lo, hi = t * LANES, (t + 1) * LANES
        k = p[:, _O_KS + lo:_O_KS + hi] * cb + p[:, _O_KS_SW + lo:_O_KS_SW + hi] * sb
        ks_ref[:, lo:hi] = k.astype(BF16)
        vs_ref[:, lo:hi] = (p[:, _O_VS + lo:_O_VS + hi] + ones_col).astype(BF16)


def _attn_project(h, mod, g_pre, w1, q_norm, wq, kv_norm, wk, wvt, tabs, tm):
    b, s, d = h.shape
    tm = min(tm, s)
    row = lambda bi, i: (bi, i, 0)
    tab = pl.BlockSpec((tm, LANES), lambda bi, i: (i, 0))
    tok = lambda w: (pl.BlockSpec((None, tm, w), row), jax.ShapeDtypeStruct((b, s, w), BF16))
    chan = (pl.BlockSpec((None, _HL, tm), lambda bi, i: (bi, 0, i)), jax.ShapeDtypeStruct((b, _HL, s), BF16))
    outs = [tok(_HL), tok(_HL), chan, tok(_QS_W), tok(_KS_W), tok(_KS_W)]
    return pl.pallas_call(
        _attn_proj_kernel,
        name="attn_project",
        grid=(b, s // tm),
        in_specs=[pl.BlockSpec((None, tm, d), row),
                  pl.BlockSpec((None, N_MOD, d), lambda bi, i: (bi, 0, 0)),
                  _const_spec((1, d)), _const_spec(w1.shape),
                  _const_spec((1, MLA_Q_LORA)), _const_spec(wq.shape),
                  _const_spec((1, MLA_KV_LORA)), _const_spec(wk.shape), _const_spec(wvt.shape),
                  tab, tab, tab, tab],
        out_specs=[o[0] for o in outs],
        out_shape=[o[1] for o in outs],
        compiler_params=_params(2),
    )(h, mod, g_pre.reshape(1, d), w1, q_norm.reshape(1, -1), wq, kv_norm.reshape(1, -1), wk, wvt, *tabs)


def _mla_kernel(*refs, n_lat_chunks, tk):
    if n_lat_chunks:
        q_ref, kc_ref, vct_ref, kl_ref, vlt_ref, o_ref = refs
    else:
        q_ref, kc_ref, vct_ref, o_ref = refs
    q = q_ref[...]

    def scores(j):
        k = kc_ref[...] if j == 0 else kl_ref[(j - 1) * tk:j * tk, :]
        return _dot_nt(k, q)

    def update(j, st, m, acc):
        vt = vct_ref[...] if j == 0 else vlt_ref[:, (j - 1) * tk:j * tk]
        m_blk = jnp.max(st, axis=0, keepdims=True)
        m_new = m_blk if m is None else jnp.maximum(m, m_blk)
        pt = jnp.exp2(st - m_new).astype(BF16)
        pv = jnp.dot(vt, pt, preferred_element_type=F32)
        acc = pv if acc is None else acc * jnp.exp2(m - m_new) + pv
        return m_new, acc

    m, acc = None, None
    st_next = scores(0)
    for j in range(n_lat_chunks + 1):
        st = st_next
        if j < n_lat_chunks:
            st_next = scores(j + 1)
        m, acc = update(j, st, m, acc)
    o_ref[...] = (acc / acc[MLA_V:MLA_V + 1, :]).T.astype(o_ref.dtype)


def _mla(q, kc, vct, kl, vlt, tq, tk):
    b, s, _ = q.shape
    n_ctx = kc.shape[1]
    tq = min(tq, s)
    qspec = pl.BlockSpec((None, tq, LANES), lambda bi, h, i: (bi, i, h))
    in_specs = [qspec,
                pl.BlockSpec((None, n_ctx, LANES), lambda bi, h, i: (bi, 0, h)),
                pl.BlockSpec((None, LANES, n_ctx), lambda bi, h, i: (bi, h, 0))]
    args = [q, kc, vct]
    n_lat_chunks = 0
    if kl is not None:
        n_lat = kl.shape[1]
        tk = min(tk, n_lat)
        n_lat_chunks = n_lat // tk
        in_specs += [pl.BlockSpec((None, n_lat, LANES), lambda bi, h, i: (bi, 0, h)),
                     pl.BlockSpec((None, LANES, n_lat), lambda bi, h, i: (bi, h, 0))]
        args += [kl, vlt]
    return pl.pallas_call(
        functools.partial(_mla_kernel, n_lat_chunks=n_lat_chunks, tk=tk),
        name="mla_lat" if n_lat_chunks else "mla_ctx",
        grid=(b, MLA_HEADS, s // tq),
        in_specs=in_specs,
        out_specs=qspec,
        out_shape=jax.ShapeDtypeStruct((b, s, _HL), BF16),
        compiler_params=_params(3),
    )(*args)


def _swa_kernel(*refs, windowed, n_blocks):
    if windowed:
        (sink_ref, q_ref, kc_ref, vc_ref, kp_ref, k0_ref, kn_ref,
         vp_ref, v0_ref, vn_ref, o_ref) = refs
    else:
        sink_ref, q_ref, kc_ref, vc_ref, o_ref = refs
    i = pl.program_id(1)
    tq = q_ref.shape[0]
    rows = SWA_GROUP * tq
    lane = lax.broadcasted_iota(jnp.int32, (1, LANES), 1)
    half_mask = [(lane < SWA_HEAD_DIM), (lane >= SWA_HEAD_DIM)]
    if windowed:
        r = jnp.bitwise_and(lax.broadcasted_iota(jnp.int32, (rows, tq), 0), tq - 1)
        c = lax.broadcasted_iota(jnp.int32, (rows, tq), 1)
        prev_ok = c >= r + jnp.where(i > 0, 0, tq)
        next_ok = c <= r - jnp.where(i < n_blocks - 1, 0, tq)
    for kv in range(SWA_KV_HEADS):
        lo, hi = kv * LANES, (kv + 1) * LANES
        qs, sk = [], []
        for g in range(SWA_GROUP):
            hd = kv * SWA_GROUP + g
            t = hd // 2
            qt = q_ref[:, t * LANES:(t + 1) * LANES]
            qs.append(jnp.where(half_mask[hd % 2], qt, jnp.zeros_like(qt)))
            sk.append(jnp.full((tq, 1), sink_ref[hd], F32))
        q4 = jnp.concatenate(qs, axis=0)
        sink = jnp.concatenate(sk, axis=0)
        scores = [_dot_nt(q4, kc_ref[:, lo:hi])]
        vals = [vc_ref[:, lo:hi]]
        if windowed:
            sp = jnp.where(prev_ok, _dot_nt(q4, kp_ref[:, lo:hi]), NEG_INF)
            s0 = _dot_nt(q4, k0_ref[:, lo:hi])
            sn = jnp.where(next_ok, _dot_nt(q4, kn_ref[:, lo:hi]), NEG_INF)
            scores += [sp, s0, sn]
            vals += [vp_ref[:, lo:hi], v0_ref[:, lo:hi], vn_ref[:, lo:hi]]
        m = sink
        for s_ in scores:
            m = jnp.maximum(m, jnp.max(s_, axis=-1, keepdims=True))
        acc = None
        for s_, v_ in zip(scores, vals):
            pv = jnp.dot(jnp.exp(s_ - m).astype(BF16), v_, preferred_element_type=F32)
            acc = pv if acc is None else acc + pv
        denom = acc[:, SWA_HEAD_DIM:SWA_HEAD_DIM + 1] + jnp.exp(sink - m)
        o = (acc / denom).astype(o_ref.dtype)
        for g in range(SWA_GROUP):
            hd = kv * SWA_GROUP + g
            o_ref[:, hd * LANES:(hd + 1) * LANES] = o[g * tq:(g + 1) * tq, :]


def _swa(sink, q, kc, vc, kl, vl):
    b, s, _ = q.shape
    n_ctx = kc.shape[1]
    windowed = kl is not None
    tq = Q_BLOCK if windowed else s
    nb = s // tq
    cspec = pl.BlockSpec((None, n_ctx, _KS_W), lambda bi, i: (bi, 0, 0))
    in_specs = [pl.BlockSpec(memory_space=pltpu.SMEM),
                pl.BlockSpec((None, tq, _QS_W), lambda bi, i: (bi, i, 0)), cspec, cspec]
    args = [sink, q, kc, vc]
    if windowed:
        prev = pl.BlockSpec((None, tq, _KS_W), lambda bi, i: (bi, jnp.maximum(i - 1, 0), 0))
        cur = pl.BlockSpec((None, tq, _KS_W), lambda bi, i: (bi, i, 0))
        nxt = pl.BlockSpec((None, tq, _KS_W), lambda bi, i: (bi, jnp.minimum(i + 1, nb - 1), 0))
        in_specs += [prev, cur, nxt, prev, cur, nxt]
        args += [kl, kl, kl, vl, vl, vl]
    return pl.pallas_call(
        functools.partial(_swa_kernel, windowed=windowed, n_blocks=nb),
        name="swa_lat" if windowed else "swa_ctx",
        grid=(b, nb),
        in_specs=in_specs,
        out_specs=pl.BlockSpec((None, tq, SWA_HEADS * LANES), lambda bi, i: (bi, i, 0)),
        out_shape=jax.ShapeDtypeStruct((b, s, SWA_HEADS * LANES), BF16),
        compiler_params=_params(2),
    )(*args)


def _attn_out_kernel(oa_ref, ob_ref, h_ref, mod_ref, gpost_ref, wa_ref, wb_ref, o_ref):
    y = jnp.dot(oa_ref[...], wa_ref[...], preferred_element_type=F32)
    y = y + jnp.dot(ob_ref[...], wb_ref[...], preferred_element_type=F32)
    o_ref[...] = h_ref[...] + mod_ref[5:6, :] * _rms(y, gpost_ref[...])


def _attn_out(oa, ob, h, mod, g_post, wa, wb, tm):
    b, s, d = h.shape
    tm = min(tm, s)
    row = lambda bi, i: (bi, i, 0)
    return pl.pallas_call(
        _attn_out_kernel,
        name="attn_out",
        grid=(b, s // tm),
        in_specs=[pl.BlockSpec((None, tm, oa.shape[2]), row),
                  pl.BlockSpec((None, tm, ob.shape[2]), row),
                  pl.BlockSpec((None, tm, d), row),
                  pl.BlockSpec((None, N_MOD, d), lambda bi, i: (bi, 0, 0)),
                  _const_spec((1, d)), _const_spec(wa.shape), _const_spec(wb.shape)],
        out_specs=pl.BlockSpec((None, tm, d), row),
        out_shape=jax.ShapeDtypeStruct((b, s, d), F32),
        compiler_params=_params(2),
    )(oa, ob, h, mod, g_post.reshape(1, d), wa, wb)


def _s5_in_kernel(h_ref, mod_ref, gpre_ref, wt_ref, o_ref):
    x = h_ref[...]
    a = (_rms(x, gpre_ref[...]) * (1.0 + mod_ref[4:5, :]) + mod_ref[3:4, :]).astype(BF16)
    o_ref[...] = _dot_nt(wt_ref[...], a)


def _s5_in(h, mod, g_pre, w_in_t, tm):
    b, s, d = h.shape
    w = w_in_t.shape[0]
    tm = min(tm, s)
    return pl.pallas_call(
        _s5_in_kernel,
        name="s5_in",
        grid=(b, s // tm),
        in_specs=[pl.BlockSpec((None, tm, d), lambda bi, i: (bi, i, 0)),
                  pl.BlockSpec((None, N_MOD, d), lambda bi, i: (bi, 0, 0)),
                  _const_spec((1, d)), _const_spec(w_in_t.shape)],
        out_specs=pl.BlockSpec((None, w, tm), lambda bi, i: (bi, 0, i)),
        out_shape=jax.ShapeDtypeStruct((b, w, s), F32),
        compiler_params=_params(2),
    )(h, mod, g_pre.reshape(1, d), w_in_t)


def _cmul(x, p1, p2):
    return x * p1 + pltpu.roll(x, S5_STATE, 1) * p2


def _s5_core_kernel(u_ref, uc_ref, kk_ref, et_ref, ft_ref, ap_ref, dsk_ref, o_ref,
                    ef_ref, eb_ref, xf_ref, xb_ref, m_ref, *, nb, nc, ncc):
    h = S5_GROUP
    rows = nb * nc

    @pl.when(pl.program_id(1) == 0)
    def _():
        mi = lax.broadcasted_iota(jnp.int32, (S5_CHUNK, S5_CHUNK), 0)
        li = lax.broadcasted_iota(jnp.int32, (S5_CHUNK, S5_CHUNK), 1)
        causal = li >= mi

        def build(i, carry):
            r0 = pl.multiple_of(i * S5_CHUNK, S5_CHUNK)
            for j in range(h):
                lags = kk_ref[i, j:j + 1, :]
                fwd = pltpu.roll(jnp.broadcast_to(lags[:, S5_CHUNK:], (S5_CHUNK, S5_CHUNK)), 0, 1,
                                 stride=1, stride_axis=0)
                bwd = pltpu.roll(jnp.broadcast_to(lags[:, :S5_CHUNK], (S5_CHUNK, S5_CHUNK)), 0, 1,
                                 stride=1, stride_axis=0)
                m_ref[pl.ds(r0, S5_CHUNK), j * S5_CHUNK:(j + 1) * S5_CHUNK] = (
                    jnp.where(causal, fwd, bwd).astype(BF16))
            return carry

        lax.fori_loop(0, h, build, 0)

    u32 = jnp.concatenate([u_ref[:, i].reshape(rows, S5_CHUNK) for i in range(h)], axis=1)
    ub = u32.astype(BF16)
    et = et_ref[...]
    ein = jnp.dot(ub, et, preferred_element_type=F32)
    ef_ref[...] = ein[:, :LANES]
    eb_ref[...] = ein[:, LANES:]
    pf1, pf2 = ap_ref[0:1, :LANES], ap_ref[1:2, :LANES]
    pb1, pb2 = ap_ref[0:1, LANES:], ap_ref[1:2, LANES:]

    xf = jnp.zeros((nb, LANES), F32)
    xb = jnp.zeros((nb, LANES), F32)
    for c in range(ncc):
        ecf = jnp.dot(uc_ref[c].astype(BF16), et, preferred_element_type=F32)
        ecb = jnp.dot(uc_ref[ncc - 1 - c].astype(BF16), et, preferred_element_type=F32)
        xf = _cmul(xf, pf1, pf2) + ecf[:, :LANES]
        xb = _cmul(xb, pb1, pb2) + ecb[:, LANES:]

    def body(t, carry):
        xf, xb = carry
        cf = pl.ds(t, nb, stride=nc)
        cb = pl.ds(nc - 1 - t, nb, stride=nc)
        xf_ref[cf, :] = xf
        xb_ref[cb, :] = xb
        xf = _cmul(xf, pf1, pf2) + ef_ref[cf, :]
        xb = _cmul(xb, pb1, pb2) + eb_ref[cb, :]
        return xf, xb

    lax.fori_loop(0, nc, body, (xf, xb))
    xs = jnp.concatenate([xf_ref[...], xb_ref[...]], axis=1).astype(BF16)
    y = jnp.dot(ub, m_ref[...], preferred_element_type=F32)
    y = y + jnp.dot(xs, ft_ref[...], preferred_element_type=F32)
    y = y + u32 * dsk_ref[...]
    for j in range(h):
        o_ref[:, j] = y[:, j * S5_CHUNK:(j + 1) * S5_CHUNK].reshape(nb, nc, S5_CHUNK)


def _s5_core(u_t, uc, kk, et, ft, ap, dsk, nb):
    b, w, s = u_t.shape
    g = w // S5_GROUP
    nc = s // S5_CHUNK
    ncc = uc.shape[2]
    hl = S5_GROUP * S5_CHUNK
    u5 = u_t.reshape(b, g, S5_GROUP, nc, S5_CHUNK)
    blk = pl.BlockSpec((nb, None, S5_GROUP, nc, S5_CHUNK), lambda gi, bi: (bi, gi, 0, 0, 0))
    per_g = lambda shape: pl.BlockSpec((None,) + shape, lambda gi, bi: (gi,) + (0,) * len(shape))
    out = pl.pallas_call(
        functools.partial(_s5_core_kernel, nb=nb, nc=nc, ncc=ncc),
        name="s5_core",
        grid=(g, b // nb),
        in_specs=[blk,
                  pl.BlockSpec((None, None, ncc, nb, hl), lambda gi, bi: (gi, bi, 0, 0, 0)),
                  per_g((S5_GROUP, S5_GROUP, 2 * S5_CHUNK)),
                  per_g((hl, 2 * LANES)), per_g((2 * LANES, hl)),
                  per_g((2, 2 * LANES)), per_g((1, hl))],
        out_specs=blk,
        out_shape=jax.ShapeDtypeStruct(u5.shape, F32),
        scratch_shapes=[pltpu.VMEM((nb * nc, LANES), F32) for _ in range(4)]
        + [pltpu.VMEM((hl, hl), BF16)],
        compiler_params=_params(2),
    )(u5, uc, kk, et, ft, ap, dsk)
    return out.reshape(b, w, s)


def _s5_out_kernel(y_ref, h_ref, mod_ref, gpost_ref, wt_ref, o_ref):
    d = h_ref.shape[1]
    gy = jax.nn.gelu(y_ref[...]).astype(BF16)
    z = jnp.dot(wt_ref[...], gy, preferred_element_type=F32)
    v = (z[:d, :] * jax.nn.sigmoid(z[d:, :])).T
    o_ref[...] = h_ref[...] + mod_ref[5:6, :] * _rms(v, gpost_ref[...])


def _s5_out(y_t, h, mod, g_post, w_glu_t, tm):
    b, s, d = h.shape
    w = y_t.shape[1]
    tm = min(tm, s)
    return pl.pallas_call(
        _s5_out_kernel,
        name="s5_out",
        grid=(b, s // tm),
        in_specs=[pl.BlockSpec((None, w, tm), lambda bi, i: (bi, 0, i)),
                  pl.BlockSpec((None, tm, d), lambda bi, i: (bi, i, 0)),
                  pl.BlockSpec((None, N_MOD, d), lambda bi, i: (bi, 0, 0)),
                  _const_spec((1, d)), _const_spec(w_glu_t.shape)],
        out_specs=pl.BlockSpec((None, tm, d), lambda bi, i: (bi, i, 0)),
        out_shape=jax.ShapeDtypeStruct((b, s, d), F32),
        compiler_params=_params(2),
    )(y_t, h, mod, g_post.reshape(1, d), w_glu_t)


def _rope_angles(n, d_rot):
    t = jnp.arange(n)
    rows, cols = t // GRID_W, t % GRID_W
    d_axis = d_rot // 2
    inv = ROPE_BASE ** (-jnp.arange(0, d_axis, 2, dtype=F32) / d_axis)
    ang = jnp.concatenate([rows.astype(F32)[:, None] * inv, cols.astype(F32)[:, None] * inv], axis=-1)
    return jnp.cos(ang), jnp.sin(ang)


def _rope_tables(n, rotate):
    pad = LANES - MLA_NOPE - MLA_ROPE
    if rotate:
        ca, sa = _rope_angles(n, MLA_ROPE)
        cb, sb = _rope_angles(n, SWA_HEAD_DIM)
    else:
        ca, sa = jnp.ones((n, MLA_ROPE // 2), F32), jnp.zeros((n, MLA_ROPE // 2), F32)
        cb, sb = jnp.ones((n, SWA_HEAD_DIM // 2), F32), jnp.zeros((n, SWA_HEAD_DIM // 2), F32)
    one, zero = jnp.ones((n, MLA_NOPE), F32), jnp.zeros((n, MLA_NOPE), F32)
    zpad = jnp.zeros((n, pad), F32)
    return (jnp.concatenate([one, ca, ca, zpad], axis=1), jnp.concatenate([zero, sa, sa, zpad], axis=1),
            jnp.concatenate([cb] * 4, axis=1), jnp.concatenate([sb] * 4, axis=1))


def _rot_partner(w, half):
    return jnp.concatenate([-w[..., half:], w[..., :half]], axis=-1)


def _attn_weights(w_in, w_uq, w_ukv, w_out):
    d = w_in.shape[0]
    sizes = [MLA_Q_LORA, MLA_KV_LORA, MLA_ROPE, SWA_HEADS * SWA_HEAD_DIM,
             SWA_KV_HEADS * SWA_HEAD_DIM, SWA_KV_HEADS * SWA_HEAD_DIM]
    cq, ckv, kpe, qs, ks, vs = jnp.split(w_in, [int(v) for v in np.cumsum(sizes)[:-1]], axis=1)
    pad_a = LANES - MLA_NOPE - MLA_ROPE
    z = lambda *shape: jnp.zeros(shape, F32)
    kpe_blk = jnp.concatenate([z(d, MLA_NOPE), kpe, z(d, pad_a)], axis=1)
    kpe_sw = jnp.concatenate([z(d, MLA_NOPE), _rot_partner(kpe, MLA_ROPE // 2), z(d, pad_a)], axis=1)
    qs3 = qs.reshape(d, SWA_HEADS, SWA_HEAD_DIM)
    qs_sw = _rot_partner(qs3, SWA_HEAD_DIM // 2).reshape(d, -1)
    ks3 = ks.reshape(d, SWA_KV_HEADS, SWA_HEAD_DIM)
    ks_sw3 = _rot_partner(ks3, SWA_HEAD_DIM // 2)
    dup = lambda t: jnp.concatenate([t, t], axis=-1).reshape(d, -1)
    vs3 = vs.reshape(d, SWA_KV_HEADS, SWA_HEAD_DIM)
    vs_pad = jnp.concatenate([vs3, jnp.zeros_like(vs3)], axis=-1).reshape(d, -1)
    w1 = jnp.concatenate([cq, ckv, kpe_blk, kpe_sw, qs, qs_sw, dup(ks3), dup(ks_sw3), vs_pad], axis=1)
    assert w1.shape[1] == _W1_COLS

    ql = w_uq.shape[0]
    uq = w_uq.reshape(ql, MLA_HEADS, MLA_NOPE + MLA_ROPE)
    nope, pe = uq[..., :MLA_NOPE], uq[..., MLA_NOPE:]
    zq = jnp.zeros((ql, MLA_HEADS, pad_a), F32)
    wq_blk = jnp.concatenate([nope, pe, zq], axis=-1).reshape(ql, -1)
    wq_sw = jnp.concatenate([jnp.zeros_like(nope), _rot_partner(pe, MLA_ROPE // 2), zq], axis=-1).reshape(ql, -1)
    wq = jnp.concatenate([wq_blk, wq_sw], axis=1)

    kl = w_ukv.shape[0]
    ukv = w_ukv.reshape(kl, MLA_HEADS, MLA_NOPE + MLA_V)
    kn, vv = ukv[..., :MLA_NOPE], ukv[..., MLA_NOPE:]
    wk = jnp.concatenate([kn, jnp.zeros((kl, MLA_HEADS, LANES - MLA_NOPE), F32)], axis=-1).reshape(kl, -1)
    wv = jnp.concatenate([vv, jnp.zeros((kl, MLA_HEADS, LANES - MLA_V), F32)], axis=-1).reshape(kl, -1)
    wvt = wv.T

    dm = w_out.shape[1]
    na = MLA_HEADS * MLA_V
    oa = w_out[:na].reshape(MLA_HEADS, MLA_V, dm)
    ob = w_out[na:].reshape(SWA_HEADS, SWA_HEAD_DIM, dm)
    wa = jnp.concatenate([oa, jnp.zeros((MLA_HEADS, LANES - MLA_V, dm), F32)], axis=1).reshape(-1, dm)
    wb = jnp.concatenate([ob, jnp.zeros((SWA_HEADS, LANES - SWA_HEAD_DIM, dm), F32)], axis=1).reshape(-1, dm)
    return tuple(t.astype(BF16) for t in (w1, wq, wk, wvt, wa, wb))


def _complex_powers(a_re, a_im, n):
    pr, pi = jnp.ones_like(a_re)[None], jnp.zeros_like(a_im)[None]
    sr, si = a_re, a_im
    while pr.shape[0] < n + 1:
        nr, ni = pr * sr - pi * si, pr * si + pi * sr
        pr, pi = jnp.concatenate([pr, nr], 0), jnp.concatenate([pi, ni], 0)
        sr, si = sr * sr - si * si, 2.0 * sr * si
    return pr[:n + 1], pi[:n + 1]


def _s5_operators(lam_re, lam_im, b_re, b_im, c_re, c_im, log_step):
    hi = lax.Precision.HIGHEST
    ln = S5_CHUNK
    g, p = lam_re.shape[1:]
    hh = b_re.shape[-1]
    ks, ets, fts, aps = [], [], [], []
    for dr in range(2):
        lre = jnp.minimum(lam_re[dr], S5_MAX_RE)
        lim = lam_im[dr]
        dt = jnp.exp(log_step[dr])[:, None]
        mag = jnp.exp(lre * dt)
        a_re, a_im = mag * jnp.cos(lim * dt), mag * jnp.sin(lim * dt)
        den = lre * lre + lim * lim
        f_re = ((a_re - 1.0) * lre + a_im * lim) / den
        f_im = (a_im * lre - (a_re - 1.0) * lim) / den
        bb_re = f_re[..., None] * b_re[dr] - f_im[..., None] * b_im[dr]
        bb_im = f_re[..., None] * b_im[dr] + f_im[..., None] * b_re[dr]
        cr, ci = c_re[dr], c_im[dr]
        pr, pi = _complex_powers(a_re, a_im, ln)
        cb_re = jnp.einsum('gjp,gpi->gpji', cr, bb_re) - jnp.einsum('gjp,gpi->gpji', ci, bb_im)
        cb_im = jnp.einsum('gjp,gpi->gpji', cr, bb_im) + jnp.einsum('gjp,gpi->gpji', ci, bb_re)
        k = (jnp.einsum('dgp,gpji->gijd', pr[:ln], cb_re, precision=hi)
             - jnp.einsum('dgp,gpji->gijd', pi[:ln], cb_im, precision=hi))
        ks.append(k)
        er, ei = (pr[:ln][::-1], pi[:ln][::-1]) if dr == 0 else (pr[:ln], pi[:ln])
        e_re = jnp.einsum('mgp,gpi->gimp', er, bb_re) - jnp.einsum('mgp,gpi->gimp', ei, bb_im)
        e_im = jnp.einsum('mgp,gpi->gimp', er, bb_im) + jnp.einsum('mgp,gpi->gimp', ei, bb_re)
        ets.append(jnp.concatenate([e_re, e_im], axis=-1).reshape(g, hh * ln, 2 * p))
        fr, fi = (pr[1:], pi[1:]) if dr == 0 else (pr[1:][::-1], pi[1:][::-1])
        f_xre = jnp.einsum('gjp,lgp->gpjl', cr, fr) - jnp.einsum('gjp,lgp->gpjl', ci, fi)
        f_xim = -(jnp.einsum('gjp,lgp->gpjl', cr, fi) + jnp.einsum('gjp,lgp->gpjl', ci, fr))
        fts.append(jnp.concatenate([f_xre, f_xim], axis=1).reshape(g, 2 * p, hh * ln))
        aps.append(jnp.stack([jnp.concatenate([pr[ln], pr[ln]], axis=-1),
                              jnp.concatenate([-pi[ln], pi[ln]], axis=-1)], axis=1))
    kf, kb = ks
    k0 = kf[..., :1] + kb[..., :1]
    kk = jnp.concatenate([jnp.zeros_like(k0), kb[..., :0:-1], k0, kf[..., 1:]], axis=-1)
    et = jnp.concatenate(ets, axis=-1).astype(BF16)
    ft = jnp.concatenate(fts, axis=1).astype(BF16)
    ap = jnp.concatenate(aps, axis=-1)
    return kk, et, ft, ap


def kernel(x, c, ctx, c_ctx, mod_w, mod_b, norm_pre, norm_post, ffn_w13, ffn_w2,
           attn_w_in, mla_q_norm, mla_w_uq, mla_kv_norm, mla_w_ukv, swa_sink, attn_w_out,
           s5_w_in, s5_lambda_re, s5_lambda_im, s5_b_re, s5_b_im, s5_c_re, s5_c_im,
           s5_log_step, s5_d, s5_w_glu):
    b, n, d = x.shape
    n_ctx = ctx.shape[1]
    depth = mod_w.shape[0]
    tm = 512

    rows = -(-(b + 1) // 8) * 8
    cs = jnp.concatenate([c, c_ctx[None], jnp.zeros((rows - b - 1, d), F32)], axis=0)
    mods = _modulation(cs, mod_w, mod_b)
    w13 = ffn_w13.astype(BF16)
    w2 = ffn_w2.astype(BF16)

    h_lat, h_ctx = x, ctx
    for l in range(depth):
        last = l == depth - 1
        m_lat = mods[l, :b].reshape(b, N_MOD, d)
        m_ctx = jnp.broadcast_to(mods[l, b].reshape(1, N_MOD, d), (b, N_MOD, d))
        h_lat = _ffn(h_lat, m_lat, 0, norm_pre[l, 0], norm_post[l, 0], w13[l, 0], w2[l, 0], tm)
        h_ctx = _ffn(h_ctx, m_ctx, 0, norm_pre[l, 0], norm_post[l, 0], w13[l, 0], w2[l, 0], tm)
        if l % 2 == 0:
            e = l // 2
            w1, wq, wk, wvt, wa, wb = _attn_weights(attn_w_in[e], mla_w_uq[e], mla_w_ukv[e], attn_w_out[e])
            proj = functools.partial(_attn_project, g_pre=norm_pre[l, 1], w1=w1, q_norm=mla_q_norm[e],
                                     wq=wq, kv_norm=mla_kv_norm[e], wk=wk, wvt=wvt, tm=tm)
            qa_c, ka_c, va_c, qs_c, ks_c, vs_c = proj(h_ctx, m_ctx, tabs=_rope_tables(n_ctx, False))
            qa_l, ka_l, va_l, qs_l, ks_l, vs_l = proj(h_lat, m_lat, tabs=_rope_tables(n, True))
            o_a = _mla(qa_l, ka_c, va_c, ka_l, va_l, tq=256, tk=1024)
            o_b = _swa(swa_sink[e], qs_l, ks_c, vs_c, ks_l, vs_l)
            h_lat = _attn_out(o_a, o_b, h_lat, m_lat, norm_post[l, 1], wa, wb, tm)
            if not last:
                o_a_c = _mla(qa_c, ka_c, va_c, None, None, tq=256, tk=1024)
                o_b_c = _swa(swa_sink[e], qs_c, ks_c, vs_c, None, None)
                h_ctx = _attn_out(o_a_c, o_b_c, h_ctx, m_ctx, norm_post[l, 1], wa, wb, tm)
        else:
            o = l // 2
            assert last, "S5 context outputs are only needed when another layer follows"
            w_in_t = s5_w_in[o].T.astype(BF16)
            w_glu_t = s5_w_glu[o].T.astype(BF16)
            kk, et, ft, ap = _s5_operators(s5_lambda_re[o], s5_lambda_im[o], s5_b_re[o], s5_b_im[o],
                                             s5_c_re[o], s5_c_im[o], s5_log_step[o])
            width = s5_w_in.shape[2]
            g = width // S5_GROUP
            u_lat = _s5_in(h_lat, m_lat, norm_pre[l, 1], w_in_t, tm)
            u_ctx = _s5_in(h_ctx, m_ctx, norm_pre[l, 1], w_in_t, tm)
            ncc = n_ctx // S5_CHUNK
            nb = b // 2 if b % 2 == 0 else b
            uc = u_ctx.reshape(b // nb, nb, g, S5_GROUP, ncc, S5_CHUNK).transpose(2, 0, 4, 1, 3, 5)
            uc = uc.reshape(g, b // nb, ncc, nb, S5_GROUP * S5_CHUNK)
            dsk = jnp.repeat(s5_d[o].astype(F32), S5_CHUNK).reshape(g, 1, S5_GROUP * S5_CHUNK)
            y_t = _s5_core(u_lat, uc, kk, et, ft, ap, dsk, nb=nb)
            h_lat = _s5_out(y_t, h_lat, m_lat, norm_post[l, 1], w_glu_t, tm)
        h_lat = _ffn(h_lat, m_lat, 2, norm_pre[l, 2], norm_post[l, 2], w13[l, 1], w2[l, 1], tm)
        if not last:
            h_ctx = _ffn(h_ctx, m_ctx, 2, norm_pre[l, 2], norm_post[l, 2], w13[l, 1], w2[l, 1], tm)
    return h_lat
```

```python
import functools
import math

import numpy as np
import jax
import jax.numpy as jnp
from jax import lax
from jax.experimental import pallas as pl
from jax.experimental.pallas import tpu as pltpu

F32 = jnp.float32
BF16 = jnp.bfloat16

LANES = 128
VMEM_LIMIT = 56 * 1024 * 1024

N_MOD = 9
FFN_RES = 0.5
EPS = 1e-6
ROPE_BASE = 10000.0
GRID_W = 64
NEG_INF = -1e30

MLA_HEADS = 8
MLA_Q_LORA = 256
MLA_KV_LORA = 128
MLA_NOPE = 64
MLA_ROPE = 32
MLA_V = 64
MLA_SCALE = (MLA_NOPE + MLA_ROPE) ** -0.5

SWA_HEADS = 8
SWA_KV_HEADS = 2
SWA_GROUP = SWA_HEADS // SWA_KV_HEADS
SWA_HEAD_DIM = 64
SWA_WINDOW = 128
SWA_SCALE = SWA_HEAD_DIM ** -0.5
Q_BLOCK = 128

S5_GROUP = 16
S5_STATE = 64
S5_CHUNK = 128
S5_MAX_RE = -1e-4


def _params(n_grid):
    return pltpu.CompilerParams(dimension_semantics=("arbitrary",) * n_grid,
                                vmem_limit_bytes=VMEM_LIMIT)


def _rms(x, g):
    return x * lax.rsqrt(jnp.mean(x * x, axis=-1, keepdims=True) + EPS) * g


def _const_spec(shape):
    nd = len(shape)
    return pl.BlockSpec(shape, lambda *_: (0,) * nd, pipeline_mode=pl.Buffered(1))


def _dot_nt(a, b):
    return lax.dot_general(a, b, (((1,), (1,)), ((), ())), preferred_element_type=F32)


def _mod_kernel(c_ref, w_ref, b_ref, o_ref):
    a = jax.nn.silu(c_ref[...]).astype(BF16)
    o_ref[...] = jnp.dot(a, w_ref[...].astype(BF16), preferred_element_type=F32) + b_ref[...]


def _modulation(cs, mod_w, mod_b, tn=1024):
    depth, d, n = mod_w.shape
    r = cs.shape[0]
    return pl.pallas_call(
        _mod_kernel,
        name="modulation",
        grid=(depth, n // tn),
        in_specs=[pl.BlockSpec((r, d), lambda l, j: (0, 0)),
                  pl.BlockSpec((None, d, tn), lambda l, j: (l, 0, j)),
                  pl.BlockSpec((None, 1, tn), lambda l, j: (l, 0, j))],
        out_specs=pl.BlockSpec((None, r, tn), lambda l, j: (l, 0, j)),
        out_shape=jax.ShapeDtypeStruct((depth, r, n), F32),
        compiler_params=_params(2),
    )(cs, mod_w, mod_b.reshape(depth, 1, n))


def _ffn_kernel(h_ref, mod_ref, gpre_ref, gpost_ref, w13_ref, w2_ref, o_ref, acc_ref, *, j, fc):
    x = h_ref[...]
    shift = mod_ref[3 * j:3 * j + 1, :]
    scale = mod_ref[3 * j + 1:3 * j + 2, :]
    gate = mod_ref[3 * j + 2:3 * j + 3, :]
    a = (_rms(x, gpre_ref[...]) * (1.0 + scale) + shift).astype(BF16)
    f = w2_ref.shape[0]
    for c in range(f // fc):
        g = jnp.dot(a, w13_ref[:, c * fc:(c + 1) * fc], preferred_element_type=F32)
        u = jnp.dot(a, w13_ref[:, f + c * fc:f + (c + 1) * fc], preferred_element_type=F32)
        act = (jax.nn.silu(g) * u).astype(BF16)
        contrib = jnp.dot(act, w2_ref[c * fc:(c + 1) * fc, :], preferred_element_type=F32)
        if c == 0:
            acc_ref[...] = contrib
        else:
            acc_ref[...] += contrib
    o_ref[...] = x + FFN_RES * gate * _rms(acc_ref[...], gpost_ref[...])


def _ffn(h, mod, j, g_pre, g_post, w13, w2, tm):
    b, s, d = h.shape
    f = w2.shape[0]
    fc = 256 if f % 256 == 0 else f
    tm = min(tm, s)
    return pl.pallas_call(
        functools.partial(_ffn_kernel, j=j, fc=fc),
        name="ffn",
        grid=(b, s // tm),
        in_specs=[pl.BlockSpec((None, tm, d), lambda bi, i: (bi, i, 0)),
                  pl.BlockSpec((None, N_MOD, d), lambda bi, i: (bi, 0, 0)),
                  _const_spec((1, d)), _const_spec((1, d)),
                  _const_spec(w13.shape), _const_spec(w2.shape)],
        out_specs=pl.BlockSpec((None, tm, d), lambda bi, i: (bi, i, 0)),
        out_shape=jax.ShapeDtypeStruct((b, s, d), F32),
        scratch_shapes=[pltpu.VMEM((tm, d), F32)],
        compiler_params=_params(2),
    )(h, mod, g_pre.reshape(1, d), g_post.reshape(1, d), w13, w2)


_O_CQ = 0
_O_CKV = _O_CQ + MLA_Q_LORA
_O_KPE = _O_CKV + MLA_KV_LORA
_O_KPE_SW = _O_KPE + LANES
_O_QS = _O_KPE_SW + LANES
_O_QS_SW = _O_QS + SWA_HEADS * SWA_HEAD_DIM
_O_KS = _O_QS_SW + SWA_HEADS * SWA_HEAD_DIM
_O_KS_SW = _O_KS + SWA_KV_HEADS * LANES
_W1_COLS = _O_KS_SW + SWA_KV_HEADS * LANES
_HL = MLA_HEADS * LANES
_QS_W = SWA_HEADS * SWA_HEAD_DIM
_KS_W = SWA_KV_HEADS * LANES


def _attn_proj_kernel(h_ref, mod_ref, gpre_ref, w1_ref, qn_ref, wq_ref, kvn_ref, wk_ref, wvt_ref, wvst_ref,
                      ca_ref, sa_ref, cb_ref, sb_ref,
                      qa_ref, ka_ref, vat_ref, qs_ref, ks_ref, vst_ref):
    x = h_ref[...]
    shift = mod_ref[3:4, :]
    scale = mod_ref[4:5, :]
    a = (_rms(x, gpre_ref[...]) * (1.0 + scale) + shift).astype(BF16)
    p = jnp.dot(a, w1_ref[...], preferred_element_type=F32)
    ca, sa, cb, sb = ca_ref[...], sa_ref[...], cb_ref[...], sb_ref[...]

    cqn = _rms(p[:, _O_CQ:_O_CQ + MLA_Q_LORA], qn_ref[...]).astype(BF16)
    q2 = jnp.dot(cqn, wq_ref[...], preferred_element_type=F32)
    ckvn = _rms(p[:, _O_CKV:_O_CKV + MLA_KV_LORA], kvn_ref[...]).astype(BF16)
    kn = jnp.dot(ckvn, wk_ref[...], preferred_element_type=F32)
    kpe = p[:, _O_KPE:_O_KPE + LANES] * ca + p[:, _O_KPE_SW:_O_KPE_SW + LANES] * sa
    for hd in range(MLA_HEADS):
        lo, hi = hd * LANES, (hd + 1) * LANES
        q = q2[:, lo:hi] * ca + q2[:, _HL + lo:_HL + hi] * sa
        qa_ref[:, lo:hi] = (q * (MLA_SCALE * math.log2(math.e))).astype(BF16)
        ka_ref[:, lo:hi] = (kn[:, lo:hi] + kpe).astype(BF16)
    vt = _dot_nt(wvt_ref[...], ckvn)
    row = lax.broadcasted_iota(jnp.int32, (_HL, 1), 0)
    vat_ref[...] = (vt + (jnp.bitwise_and(row, LANES - 1) == MLA_V).astype(F32)).astype(BF16)

    for t in range(_QS_W // LANES):
        lo, hi = t * LANES, (t + 1) * LANES
        q = p[:, _O_QS + lo:_O_QS + hi] * cb + p[:, _O_QS_SW + lo:_O_QS_SW + hi] * sb
        qs_ref[:, lo:hi] = (q * (SWA_SCALE * math.log2(math.e))).astype(BF16)
    for t in range(SWA_KV_HEADS):
        lo, hi = t * LANES, (t + 1) * LANES
        k = p[:, _O_KS + lo:_O_KS + hi] * cb + p[:, _O_KS_SW + lo:_O_KS_SW + hi] * sb
        ks_ref[:, lo:hi] = k.astype(BF16)
    vst = _dot_nt(wvst_ref[...], a)
    row = lax.broadcasted_iota(jnp.int32, (_KS_W, 1), 0)
    vst_ref[...] = (vst + (jnp.bitwise_and(row, LANES - 1) == SWA_HEAD_DIM).astype(F32)).astype(BF16)


def _attn_project(h, mod, g_pre, w1, q_norm, wq, kv_norm, wk, wvt, wvst, tabs, tm):
    b, s, d = h.shape
    tm = min(tm, s)
    row = lambda bi, i: (bi, i, 0)
    tab = pl.BlockSpec((tm, LANES), lambda bi, i: (i, 0))
    tok = lambda w: (pl.BlockSpec((None, tm, w), row), jax.ShapeDtypeStruct((b, s, w), BF16))
    chan = lambda w: (pl.BlockSpec((None, w, tm), lambda bi, i: (bi, 0, i)), jax.ShapeDtypeStruct((b, w, s), BF16))
    outs = [tok(_HL), tok(_HL), chan(_HL), tok(_QS_W), tok(_KS_W), chan(_KS_W)]
    return pl.pallas_call(
        _attn_proj_kernel,
        name="attn_project",
        grid=(b, s // tm),
        in_specs=[pl.BlockSpec((None, tm, d), row),
                  pl.BlockSpec((None, N_MOD, d), lambda bi, i: (bi, 0, 0)),
                  _const_spec((1, d)), _const_spec(w1.shape),
                  _const_spec((1, MLA_Q_LORA)), _const_spec(wq.shape),
                  _const_spec((1, MLA_KV_LORA)), _const_spec(wk.shape), _const_spec(wvt.shape),
                  _const_spec(wvst.shape), tab, tab, tab, tab],
        out_specs=[o[0] for o in outs],
        out_shape=[o[1] for o in outs],
        compiler_params=_params(2),
    )(h, mod, g_pre.reshape(1, d), w1, q_norm.reshape(1, -1), wq, kv_norm.reshape(1, -1), wk, wvt, wvst, *tabs)


def _mla_kernel(*refs, n_lat_chunks, tk):
    if n_lat_chunks:
        q_ref, kc_ref, vct_ref, kl_ref, vlt_ref, o_ref = refs
    else:
        q_ref, kc_ref, vct_ref, o_ref = refs
    q = q_ref[...]

    def scores(j):
        k = kc_ref[...] if j == 0 else kl_ref[(j - 1) * tk:j * tk, :]
        return _dot_nt(k, q)

    def update(j, st, m, acc):
        vt = vct_ref[...] if j == 0 else vlt_ref[:, (j - 1) * tk:j * tk]
        m_blk = jnp.max(st, axis=0, keepdims=True)
        m_new = m_blk if m is None else jnp.maximum(m, m_blk)
        pt = jnp.exp2(st - m_new).astype(BF16)
        pv = jnp.dot(vt, pt, preferred_element_type=F32)
        acc = pv if acc is None else acc * jnp.exp2(m - m_new) + pv
        return m_new, acc

    m, acc = None, None
    st_next = scores(0)
    for j in range(n_lat_chunks + 1):
        st = st_next
        if j < n_lat_chunks:
            st_next = scores(j + 1)
        m, acc = update(j, st, m, acc)
    o_ref[...] = (acc / acc[MLA_V:MLA_V + 1, :]).T.astype(o_ref.dtype)


def _mla(q, kc, vct, kl, vlt, tq, tk):
    b, s, _ = q.shape
    n_ctx = kc.shape[1]
    tq = min(tq, s)
    qspec = pl.BlockSpec((None, tq, LANES), lambda bi, h, i: (bi, i, h))
    in_specs = [qspec,
                pl.BlockSpec((None, n_ctx, LANES), lambda bi, h, i: (bi, 0, h)),
                pl.BlockSpec((None, LANES, n_ctx), lambda bi, h, i: (bi, h, 0))]
    args = [q, kc, vct]
    n_lat_chunks = 0
    if kl is not None:
        n_lat = kl.shape[1]
        tk = min(tk, n_lat)
        n_lat_chunks = n_lat // tk
        in_specs += [pl.BlockSpec((None, n_lat, LANES), lambda bi, h, i: (bi, 0, h)),
                     pl.BlockSpec((None, LANES, n_lat), lambda bi, h, i: (bi, h, 0))]
        args += [kl, vlt]
    return pl.pallas_call(
        functools.partial(_mla_kernel, n_lat_chunks=n_lat_chunks, tk=tk),
        name="mla_lat" if n_lat_chunks else "mla_ctx",
        grid=(b, MLA_HEADS, s // tq),
        in_specs=in_specs,
        out_specs=qspec,
        out_shape=jax.ShapeDtypeStruct((b, s, _HL), BF16),
        compiler_params=_params(3),
    )(*args)


def _swa_kernel(*refs, windowed, n_blocks):
    if windowed:
        (sink_ref, q_ref, kc_ref, vct_ref, kp_ref, k0_ref, kn_ref,
         vpt_ref, v0t_ref, vnt_ref, o_ref) = refs
    else:
        sink_ref, q_ref, kc_ref, vct_ref, o_ref = refs
    i = pl.program_id(1)
    tq = q_ref.shape[0]
    cols = SWA_GROUP * tq
    lane = lax.broadcasted_iota(jnp.int32, (1, LANES), 1)
    half_mask = [(lane < SWA_HEAD_DIM), (lane >= SWA_HEAD_DIM)]
    if windowed:
        j = lax.broadcasted_iota(jnp.int32, (3 * tq, cols), 0)
        r = jnp.bitwise_and(lax.broadcasted_iota(jnp.int32, (3 * tq, cols), 1), tq - 1)
        j_lo = jnp.where(i > 0, r, tq)
        j_hi = jnp.where(i < n_blocks - 1, r + 2 * tq, 2 * tq - 1)
    for kv in range(SWA_KV_HEADS):
        lo, hi = kv * LANES, (kv + 1) * LANES
        qs, sk = [], []
        for g in range(SWA_GROUP):
            hd = kv * SWA_GROUP + g
            t = hd // 2
            qt = q_ref[:, t * LANES:(t + 1) * LANES]
            qs.append(jnp.where(half_mask[hd % 2], qt, jnp.zeros_like(qt)))
            sk.append(jnp.full((1, tq), sink_ref[hd] * math.log2(math.e), F32))
        q4 = jnp.concatenate(qs, axis=0)
        sink = jnp.concatenate(sk, axis=1)
        scores = [_dot_nt(kc_ref[:, lo:hi], q4)]
        vals = [vct_ref[lo:hi, :]]
        if windowed:
            kw = jnp.concatenate([kp_ref[:, lo:hi], k0_ref[:, lo:hi], kn_ref[:, lo:hi]], axis=0)
            sw = _dot_nt(kw, q4)
            sw = jnp.where(j >= j_lo, jnp.where(j <= j_hi, sw, NEG_INF), NEG_INF)
            scores.append(sw)
            vals.append(jnp.concatenate([vpt_ref[lo:hi, :], v0t_ref[lo:hi, :], vnt_ref[lo:hi, :]], axis=1))
        m = sink
        for s_ in scores:
            m = jnp.maximum(m, jnp.max(s_, axis=0, keepdims=True))
        acc = None
        for s_, vt in zip(scores, vals):
            pv = jnp.dot(vt, jnp.exp2(s_ - m).astype(BF16), preferred_element_type=F32)
            acc = pv if acc is None else acc + pv
        denom = acc[SWA_HEAD_DIM:SWA_HEAD_DIM + 1, :] + jnp.exp2(sink - m)
        o = acc / denom
        for g in range(SWA_GROUP):
            hd = kv * SWA_GROUP + g
            o_ref[:, hd * LANES:(hd + 1) * LANES] = o[:, g * tq:(g + 1) * tq].T.astype(o_ref.dtype)


def _swa(sink, q, kc, vct, kl, vlt):
    b, s, _ = q.shape
    n_ctx = kc.shape[1]
    windowed = kl is not None
    tq = Q_BLOCK if windowed else s
    nb = s // tq
    in_specs = [pl.BlockSpec(memory_space=pltpu.SMEM),
                pl.BlockSpec((None, tq, _QS_W), lambda bi, i: (bi, i, 0)),
                pl.BlockSpec((None, n_ctx, _KS_W), lambda bi, i: (bi, 0, 0)),
                pl.BlockSpec((None, _KS_W, n_ctx), lambda bi, i: (bi, 0, 0))]
    args = [sink, q, kc, vct]
    if windowed:
        blocks = (lambda i: jnp.maximum(i - 1, 0), lambda i: i, lambda i: jnp.minimum(i + 1, nb - 1))
        in_specs += [pl.BlockSpec((None, tq, _KS_W), lambda bi, i, f=f: (bi, f(i), 0)) for f in blocks]
        in_specs += [pl.BlockSpec((None, _KS_W, tq), lambda bi, i, f=f: (bi, 0, f(i))) for f in blocks]
        args += [kl, kl, kl, vlt, vlt, vlt]
    return pl.pallas_call(
        functools.partial(_swa_kernel, windowed=windowed, n_blocks=nb),
        name="swa_lat" if windowed else "swa_ctx",
        grid=(b, nb),
        in_specs=in_specs,
        out_specs=pl.BlockSpec((None, tq, SWA_HEADS * LANES), lambda bi, i: (bi, i, 0)),
        out_shape=jax.ShapeDtypeStruct((b, s, SWA_HEADS * LANES), BF16),
        compiler_params=_params(2),
    )(*args)


def _attn_out_kernel(oa_ref, ob_ref, h_ref, mod_ref, gpost_ref, wa_ref, wb_ref, o_ref):
    y = jnp.dot(oa_ref[...], wa_ref[...], preferred_element_type=F32)
    y = y + jnp.dot(ob_ref[...], wb_ref[...], preferred_element_type=F32)
    o_ref[...] = h_ref[...] + mod_ref[5:6, :] * _rms(y, gpost_ref[...])


def _attn_out(oa, ob, h, mod, g_post, wa, wb, tm):
    b, s, d = h.shape
    tm = min(tm, s)
    row = lambda bi, i: (bi, i, 0)
    return pl.pallas_call(
        _attn_out_kernel,
        name="attn_out",
        grid=(b, s // tm),
        in_specs=[pl.BlockSpec((None, tm, oa.shape[2]), row),
                  pl.BlockSpec((None, tm, ob.shape[2]), row),
                  pl.BlockSpec((None, tm, d), row),
                  pl.BlockSpec((None, N_MOD, d), lambda bi, i: (bi, 0, 0)),
                  _const_spec((1, d)), _const_spec(wa.shape), _const_spec(wb.shape)],
        out_specs=pl.BlockSpec((None, tm, d), row),
        out_shape=jax.ShapeDtypeStruct((b, s, d), F32),
        compiler_params=_params(2),
    )(oa, ob, h, mod, g_post.reshape(1, d), wa, wb)


def _s5_in_kernel(h_ref, mod_ref, gpre_ref, wt_ref, o_ref):
    x = h_ref[...]
    a = (_rms(x, gpre_ref[...]) * (1.0 + mod_ref[4:5, :]) + mod_ref[3:4, :]).astype(BF16)
    o_ref[...] = _dot_nt(wt_ref[...], a)


def _s5_in(h, mod, g_pre, w_in_t, tm):
    b, s, d = h.shape
    w = w_in_t.shape[0]
    tm = min(tm, s)
    return pl.pallas_call(
        _s5_in_kernel,
        name="s5_in",
        grid=(b, s // tm),
        in_specs=[pl.BlockSpec((None, tm, d), lambda bi, i: (bi, i, 0)),
                  pl.BlockSpec((None, N_MOD, d), lambda bi, i: (bi, 0, 0)),
                  _const_spec((1, d)), _const_spec(w_in_t.shape)],
        out_specs=pl.BlockSpec((None, w, tm), lambda bi, i: (bi, 0, i)),
        out_shape=jax.ShapeDtypeStruct((b, w, s), F32),
        compiler_params=_params(2),
    )(h, mod, g_pre.reshape(1, d), w_in_t)


def _cmul(x, p1, p2):
    return x * p1 + pltpu.roll(x, S5_STATE, 1) * p2


def _s5_core_kernel(u_ref, uc_ref, kk_ref, et_ref, ft_ref, ap_ref, dsk_ref, o_ref,
                    ef_ref, eb_ref, xf_ref, xb_ref, m_ref, *, nb, nc, ncc):
    h = S5_GROUP
    rows = nb * nc

    @pl.when(pl.program_id(1) == 0)
    def _():
        mi = lax.broadcasted_iota(jnp.int32, (S5_CHUNK, S5_CHUNK), 0)
        li = lax.broadcasted_iota(jnp.int32, (S5_CHUNK, S5_CHUNK), 1)
        causal = li >= mi

        def build(i, carry):
            r0 = pl.multiple_of(i * S5_CHUNK, S5_CHUNK)
            for j in range(h):
                lags = kk_ref[i, j:j + 1, :]
                fwd = pltpu.roll(jnp.broadcast_to(lags[:, S5_CHUNK:], (S5_CHUNK, S5_CHUNK)), 0, 1,
                                 stride=1, stride_axis=0)
                bwd = pltpu.roll(jnp.broadcast_to(lags[:, :S5_CHUNK], (S5_CHUNK, S5_CHUNK)), 0, 1,
                                 stride=1, stride_axis=0)
                m_ref[pl.ds(r0, S5_CHUNK), j * S5_CHUNK:(j + 1) * S5_CHUNK] = (
                    jnp.where(causal, fwd, bwd).astype(BF16))
            return carry

        lax.fori_loop(0, h, build, 0)

    u32 = jnp.concatenate([u_ref[:, i].reshape(rows, S5_CHUNK) for i in range(h)], axis=1)
    ub = u32.astype(BF16)
    et = et_ref[...]
    ein = jnp.dot(ub, et, preferred_element_type=F32)
    ef_ref[...] = ein[:, :LANES]
    eb_ref[...] = ein[:, LANES:]
    pf1, pf2 = ap_ref[0:1, :LANES], ap_ref[1:2, :LANES]
    pb1, pb2 = ap_ref[0:1, LANES:], ap_ref[1:2, LANES:]

    xf = jnp.zeros((nb, LANES), F32)
    xb = jnp.zeros((nb, LANES), F32)
    for c in range(ncc):
        ecf = jnp.dot(uc_ref[c].astype(BF16), et, preferred_element_type=F32)
        ecb = jnp.dot(uc_ref[ncc - 1 - c].astype(BF16), et, preferred_element_type=F32)
        xf = _cmul(xf, pf1, pf2) + ecf[:, :LANES]
        xb = _cmul(xb, pb1, pb2) + ecb[:, LANES:]

    def body(t, carry):
        xf, xb = carry
        cf = pl.ds(t, nb, stride=nc)
        cb = pl.ds(nc - 1 - t, nb, stride=nc)
        xf_ref[cf, :] = xf
        xb_ref[cb, :] = xb
        xf = _cmul(xf, pf1, pf2) + ef_ref[cf, :]
        xb = _cmul(xb, pb1, pb2) + eb_ref[cb, :]
        return xf, xb

    lax.fori_loop(0, nc, body, (xf, xb))
    xs = jnp.concatenate([xf_ref[...], xb_ref[...]], axis=1).astype(BF16)
    y = jnp.dot(ub, m_ref[...], preferred_element_type=F32)
    y = y + jnp.dot(xs, ft_ref[...], preferred_element_type=F32)
    y = y + u32 * dsk_ref[...]
    for j in range(h):
        o_ref[:, j] = y[:, j * S5_CHUNK:(j + 1) * S5_CHUNK].reshape(nb, nc, S5_CHUNK)


def _s5_core(u_t, uc, kk, et, ft, ap, dsk, nb):
    b, w, s = u_t.shape
    g = w // S5_GROUP
    nc = s // S5_CHUNK
    ncc = uc.shape[2]
    hl = S5_GROUP * S5_CHUNK
    u5 = u_t.reshape(b, g, S5_GROUP, nc, S5_CHUNK)
    blk = pl.BlockSpec((nb, None, S5_GROUP, nc, S5_CHUNK), lambda gi, bi: (bi, gi, 0, 0, 0))
    per_g = lambda shape: pl.BlockSpec((None,) + shape, lambda gi, bi: (gi,) + (0,) * len(shape))
    out = pl.pallas_call(
        functools.partial(_s5_core_kernel, nb=nb, nc=nc, ncc=ncc),
        name="s5_core",
        grid=(g, b // nb),
        in_specs=[blk,
                  pl.BlockSpec((None, None, ncc, nb, hl), lambda gi, bi: (gi, bi, 0, 0, 0)),
                  per_g((S5_GROUP, S5_GROUP, 2 * S5_CHUNK)),
                  per_g((hl, 2 * LANES)), per_g((2 * LANES, hl)),
                  per_g((2, 2 * LANES)), per_g((1, hl))],
        out_specs=blk,
        out_shape=jax.ShapeDtypeStruct(u5.shape, F32),
        scratch_shapes=[pltpu.VMEM((nb * nc, LANES), F32) for _ in range(4)]
        + [pltpu.VMEM((hl, hl), BF16)],
        compiler_params=_params(2),
    )(u5, uc, kk, et, ft, ap, dsk)
    return out.reshape(b, w, s)


def _s5_out_kernel(y_ref, h_ref, mod_ref, gpost_ref, wt_ref, o_ref):
    d = h_ref.shape[1]
    gy = jax.nn.gelu(y_ref[...]).astype(BF16)
    z = jnp.dot(wt_ref[...], gy, preferred_element_type=F32)
    v = (z[:d, :] * jax.nn.sigmoid(z[d:, :])).T
    o_ref[...] = h_ref[...] + mod_ref[5:6, :] * _rms(v, gpost_ref[...])


def _s5_out(y_t, h, mod, g_post, w_glu_t, tm):
    b, s, d = h.shape
    w = y_t.shape[1]
    tm = min(tm, s)
    return pl.pallas_call(
        _s5_out_kernel,
        name="s5_out",
        grid=(b, s // tm),
        in_specs=[pl.BlockSpec((None, w, tm), lambda bi, i: (bi, 0, i)),
                  pl.BlockSpec((None, tm, d), lambda bi, i: (bi, i, 0)),
                  pl.BlockSpec((None, N_MOD, d), lambda bi, i: (bi, 0, 0)),
                  _const_spec((1, d)), _const_spec(w_glu_t.shape)],
        out_specs=pl.BlockSpec((None, tm, d), lambda bi, i: (bi, i, 0)),
        out_shape=jax.ShapeDtypeStruct((b, s, d), F32),
        compiler_params=_params(2),
    )(y_t, h, mod, g_post.reshape(1, d), w_glu_t)


def _rope_angles(n, d_rot):
    t = jnp.arange(n)
    rows, cols = t // GRID_W, t % GRID_W
    d_axis = d_rot // 2
    inv = ROPE_BASE ** (-jnp.arange(0, d_axis, 2, dtype=F32) / d_axis)
    ang = jnp.concatenate([rows.astype(F32)[:, None] * inv, cols.astype(F32)[:, None] * inv], axis=-1)
    return jnp.cos(ang), jnp.sin(ang)


def _rope_tables(n, rotate):
    pad = LANES - MLA_NOPE - MLA_ROPE
    if rotate:
        ca, sa = _rope_angles(n, MLA_ROPE)
        cb, sb = _rope_angles(n, SWA_HEAD_DIM)
    else:
        ca, sa = jnp.ones((n, MLA_ROPE // 2), F32), jnp.zeros((n, MLA_ROPE // 2), F32)
        cb, sb = jnp.ones((n, SWA_HEAD_DIM // 2), F32), jnp.zeros((n, SWA_HEAD_DIM // 2), F32)
    one, zero = jnp.ones((n, MLA_NOPE), F32), jnp.zeros((n, MLA_NOPE), F32)
    zpad = jnp.zeros((n, pad), F32)
    return (jnp.concatenate([one, ca, ca, zpad], axis=1), jnp.concatenate([zero, sa, sa, zpad], axis=1),
            jnp.concatenate([cb] * 4, axis=1), jnp.concatenate([sb] * 4, axis=1))


def _rot_partner(w, half):
    return jnp.concatenate([-w[..., half:], w[..., :half]], axis=-1)


def _attn_weights(w_in, w_uq, w_ukv, w_out):
    d = w_in.shape[0]
    sizes = [MLA_Q_LORA, MLA_KV_LORA, MLA_ROPE, SWA_HEADS * SWA_HEAD_DIM,
             SWA_KV_HEADS * SWA_HEAD_DIM, SWA_KV_HEADS * SWA_HEAD_DIM]
    cq, ckv, kpe, qs, ks, vs = jnp.split(w_in, [int(v) for v in np.cumsum(sizes)[:-1]], axis=1)
    pad_a = LANES - MLA_NOPE - MLA_ROPE
    z = lambda *shape: jnp.zeros(shape, F32)
    kpe_blk = jnp.concatenate([z(d, MLA_NOPE), kpe, z(d, pad_a)], axis=1)
    kpe_sw = jnp.concatenate([z(d, MLA_NOPE), _rot_partner(kpe, MLA_ROPE // 2), z(d, pad_a)], axis=1)
    qs3 = qs.reshape(d, SWA_HEADS, SWA_HEAD_DIM)
    qs_sw = _rot_partner(qs3, SWA_HEAD_DIM // 2).reshape(d, -1)
    ks3 = ks.reshape(d, SWA_KV_HEADS, SWA_HEAD_DIM)
    ks_sw3 = _rot_partner(ks3, SWA_HEAD_DIM // 2)
    dup = lambda t: jnp.concatenate([t, t], axis=-1).reshape(d, -1)
    vs3 = vs.reshape(d, SWA_KV_HEADS, SWA_HEAD_DIM)
    vs_pad = jnp.concatenate([vs3, jnp.zeros_like(vs3)], axis=-1).reshape(d, -1)
    w1 = jnp.concatenate([cq, ckv, kpe_blk, kpe_sw, qs, qs_sw, dup(ks3), dup(ks_sw3)], axis=1)
    wvst = vs_pad.T
    assert w1.shape[1] == _W1_COLS

    ql = w_uq.shape[0]
    uq = w_uq.reshape(ql, MLA_HEADS, MLA_NOPE + MLA_ROPE)
    nope, pe = uq[..., :MLA_NOPE], uq[..., MLA_NOPE:]
    zq = jnp.zeros((ql, MLA_HEADS, pad_a), F32)
    wq_blk = jnp.concatenate([nope, pe, zq], axis=-1).reshape(ql, -1)
    wq_sw = jnp.concatenate([jnp.zeros_like(nope), _rot_partner(pe, MLA_ROPE // 2), zq], axis=-1).reshape(ql, -1)
    wq = jnp.concatenate([wq_blk, wq_sw], axis=1)

    kl = w_ukv.shape[0]
    ukv = w_ukv.reshape(kl, MLA_HEADS, MLA_NOPE + MLA_V)
    kn, vv = ukv[..., :MLA_NOPE], ukv[..., MLA_NOPE:]
    wk = jnp.concatenate([kn, jnp.zeros((kl, MLA_HEADS, LANES - MLA_NOPE), F32)], axis=-1).reshape(kl, -1)
    wv = jnp.concatenate([vv, jnp.zeros((kl, MLA_HEADS, LANES - MLA_V), F32)], axis=-1).reshape(kl, -1)
    wvt = wv.T

    dm = w_out.shape[1]
    na = MLA_HEADS * MLA_V
    oa = w_out[:na].reshape(MLA_HEADS, MLA_V, dm)
    ob = w_out[na:].reshape(SWA_HEADS, SWA_HEAD_DIM, dm)
    wa = jnp.concatenate([oa, jnp.zeros((MLA_HEADS, LANES - MLA_V, dm), F32)], axis=1).reshape(-1, dm)
    wb = jnp.concatenate([ob, jnp.zeros((SWA_HEADS, LANES - SWA_HEAD_DIM, dm), F32)], axis=1).reshape(-1, dm)
    return tuple(t.astype(BF16) for t in (w1, wq, wk, wvt, wvst, wa, wb))


def _complex_powers(a_re, a_im, n):
    pr, pi = jnp.ones_like(a_re)[None], jnp.zeros_like(a_im)[None]
    sr, si = a_re, a_im
    while pr.shape[0] < n + 1:
        nr, ni = pr * sr - pi * si, pr * si + pi * sr
        pr, pi = jnp.concatenate([pr, nr], 0), jnp.concatenate([pi, ni], 0)
        sr, si = sr * sr - si * si, 2.0 * sr * si
    return pr[:n + 1], pi[:n + 1]


def _s5_operators(lam_re, lam_im, b_re, b_im, c_re, c_im, log_step):
    hi = lax.Precision.HIGHEST
    ln = S5_CHUNK
    g, p = lam_re.shape[1:]
    hh = b_re.shape[-1]
    ks, ets, fts, aps = [], [], [], []
    for dr in range(2):
        lre = jnp.minimum(lam_re[dr], S5_MAX_RE)
        lim = lam_im[dr]
        dt = jnp.exp(log_step[dr])[:, None]
        mag = jnp.exp(lre * dt)
        a_re, a_im = mag * jnp.cos(lim * dt), mag * jnp.sin(lim * dt)
        den = lre * lre + lim * lim
        f_re = ((a_re - 1.0) * lre + a_im * lim) / den
        f_im = (a_im * lre - (a_re - 1.0) * lim) / den
        bb_re = f_re[..., None] * b_re[dr] - f_im[..., None] * b_im[dr]
        bb_im = f_re[..., None] * b_im[dr] + f_im[..., None] * b_re[dr]
        cr, ci = c_re[dr], c_im[dr]
        pr, pi = _complex_powers(a_re, a_im, ln)
        cb_re = jnp.einsum('gjp,gpi->gpji', cr, bb_re) - jnp.einsum('gjp,gpi->gpji', ci, bb_im)
        cb_im = jnp.einsum('gjp,gpi->gpji', cr, bb_im) + jnp.einsum('gjp,gpi->gpji', ci, bb_re)
        k = (jnp.einsum('dgp,gpji->gijd', pr[:ln], cb_re, precision=hi)
             - jnp.einsum('dgp,gpji->gijd', pi[:ln], cb_im, precision=hi))
        ks.append(k)
        er, ei = (pr[:ln][::-1], pi[:ln][::-1]) if dr == 0 else (pr[:ln], pi[:ln])
        e_re = jnp.einsum('mgp,gpi->gimp', er, bb_re) - jnp.einsum('mgp,gpi->gimp', ei, bb_im)
        e_im = jnp.einsum('mgp,gpi->gimp', er, bb_im) + jnp.einsum('mgp,gpi->gimp', ei, bb_re)
        ets.append(jnp.concatenate([e_re, e_im], axis=-1).reshape(g, hh * ln, 2 * p))
        fr, fi = (pr[1:], pi[1:]) if dr == 0 else (pr[1:][::-1], pi[1:][::-1])
        f_xre = jnp.einsum('gjp,lgp->gpjl', cr, fr) - jnp.einsum('gjp,lgp->gpjl', ci, fi)
        f_xim = -(jnp.einsum('gjp,lgp->gpjl', cr, fi) + jnp.einsum('gjp,lgp->gpjl', ci, fr))
        fts.append(jnp.concatenate([f_xre, f_xim], axis=1).reshape(g, 2 * p, hh * ln))
        aps.append(jnp.stack([jnp.concatenate([pr[ln], pr[ln]], axis=-1),
                              jnp.concatenate([-pi[ln], pi[ln]], axis=-1)], axis=1))
    kf, kb = ks
    k0 = kf[..., :1] + kb[..., :1]
    kk = jnp.concatenate([jnp.zeros_like(k0), kb[..., :0:-1], k0, kf[..., 1:]], axis=-1)
    et = jnp.concatenate(ets, axis=-1).astype(BF16)
    ft = jnp.concatenate(fts, axis=1).astype(BF16)
    ap = jnp.concatenate(aps, axis=-1)
    return kk, et, ft, ap


def kernel(x, c, ctx, c_ctx, mod_w, mod_b, norm_pre, norm_post, ffn_w13, ffn_w2,
           attn_w_in, mla_q_norm, mla_w_uq, mla_kv_norm, mla_w_ukv, swa_sink, attn_w_out,
           s5_w_in, s5_lambda_re, s5_lambda_im, s5_b_re, s5_b_im, s5_c_re, s5_c_im,
           s5_log_step, s5_d, s5_w_glu):
    b, n, d = x.shape
    n_ctx = ctx.shape[1]
    depth = mod_w.shape[0]
    tm = 512

    rows = -(-(b + 1) // 8) * 8
    cs = jnp.concatenate([c, c_ctx[None], jnp.zeros((rows - b - 1, d), F32)], axis=0)
    mods = _modulation(cs, mod_w, mod_b)
    w13 = ffn_w13.astype(BF16)
    w2 = ffn_w2.astype(BF16)

    h_lat, h_ctx = x, ctx
    for l in range(depth):
        last = l == depth - 1
        m_lat = mods[l, :b].reshape(b, N_MOD, d)
        m_ctx = jnp.broadcast_to(mods[l, b].reshape(1, N_MOD, d), (b, N_MOD, d))
        h_lat = _ffn(h_lat, m_lat, 0, norm_pre[l, 0], norm_post[l, 0], w13[l, 0], w2[l, 0], tm)
        h_ctx = _ffn(h_ctx, m_ctx, 0, norm_pre[l, 0], norm_post[l, 0], w13[l, 0], w2[l, 0], tm)
        if l % 2 == 0:
            e = l // 2
            w1, wq, wk, wvt, wvst, wa, wb = _attn_weights(attn_w_in[e], mla_w_uq[e], mla_w_ukv[e], attn_w_out[e])
            proj = functools.partial(_attn_project, g_pre=norm_pre[l, 1], w1=w1, q_norm=mla_q_norm[e],
                                     wq=wq, kv_norm=mla_kv_norm[e], wk=wk, wvt=wvt, wvst=wvst, tm=tm)
            qa_c, ka_c, va_c, qs_c, ks_c, vs_c = proj(h_ctx, m_ctx, tabs=_rope_tables(n_ctx, False))
            qa_l, ka_l, va_l, qs_l, ks_l, vs_l = proj(h_lat, m_lat, tabs=_rope_tables(n, True))
            o_a = _mla(qa_l, ka_c, va_c, ka_l, va_l, tq=1024, tk=512)
            o_b = _swa(swa_sink[e], qs_l, ks_c, vs_c, ks_l, vs_l)
            h_lat = _attn_out(o_a, o_b, h_lat, m_lat, norm_post[l, 1], wa, wb, tm)
            if not last:
                o_a_c = _mla(qa_c, ka_c, va_c, None, None, tq=1024, tk=512)
                o_b_c = _swa(swa_sink[e], qs_c, ks_c, vs_c, None, None)
                h_ctx = _attn_out(o_a_c, o_b_c, h_ctx, m_ctx, norm_post[l, 1], wa, wb, tm)
        else:
            o = l // 2
            assert last, "S5 context outputs are only needed when another layer follows"
            w_in_t = s5_w_in[o].T.astype(BF16)
            w_glu_t = s5_w_glu[o].T.astype(BF16)
            kk, et, ft, ap = _s5_operators(s5_lambda_re[o], s5_lambda_im[o], s5_b_re[o], s5_b_im[o],
                                             s5_c_re[o], s5_c_im[o], s5_log_step[o])
            width = s5_w_in.shape[2]
            g = width // S5_GROUP
            u_lat = _s5_in(h_lat, m_lat, norm_pre[l, 1], w_in_t, tm)
            u_ctx = _s5_in(h_ctx, m_ctx, norm_pre[l, 1], w_in_t, tm)
            ncc = n_ctx // S5_CHUNK
            nb = b // 2 if b % 2 == 0 else b
            uc = u_ctx.reshape(b // nb, nb, g, S5_GROUP, ncc, S5_CHUNK).transpose(2, 0, 4, 1, 3, 5)
            uc = uc.reshape(g, b // nb, ncc, nb, S5_GROUP * S5_CHUNK)
            dsk = jnp.repeat(s5_d[o].astype(F32), S5_CHUNK).reshape(g, 1, S5_GROUP * S5_CHUNK)
            y_t = _s5_core(u_lat, uc, kk, et, ft, ap, dsk, nb=nb)
            h_lat = _s5_out(y_t, h_lat, m_lat, norm_post[l, 1], w_glu_t, tm)
        h_lat = _ffn(h_lat, m_lat, 2, norm_pre[l, 2], norm_post[l, 2], w13[l, 1], w2[l, 1], tm)
        if not last:
            h_ctx = _ffn(h_ctx, m_ctx, 2, norm_pre[l, 2], norm_post[l, 2], w13[l, 1], w2[l, 1], tm)
    return h_lat
```

```python
import functools
import math

import numpy as np
import jax
import jax.numpy as jnp
from jax import lax
from jax.experimental import pallas as pl
from jax.experimental.pallas import tpu as pltpu

F32 = jnp.float32
BF16 = jnp.bfloat16

LANES = 128
VMEM_LIMIT = 56 * 1024 * 1024

N_MOD = 9
FFN_RES = 0.5
EPS = 1e-6
ROPE_BASE = 10000.0
GRID_W = 64
NEG_INF = -1e30

MLA_HEADS = 8
MLA_Q_LORA = 256
MLA_KV_LORA = 128
MLA_NOPE = 64
MLA_ROPE = 32
MLA_V = 64
MLA_SCALE = (MLA_NOPE + MLA_ROPE) ** -0.5

SWA_HEADS = 8
SWA_KV_HEADS = 2
SWA_GROUP = SWA_HEADS // SWA_KV_HEADS
SWA_HEAD_DIM = 64
SWA_WINDOW = 128
SWA_SCALE = SWA_HEAD_DIM ** -0.5
Q_BLOCK = 128

S5_GROUP = 16
S5_STATE = 64
S5_CHUNK = 128
S5_MAX_RE = -1e-4


def _params(n_grid):
    return pltpu.CompilerParams(dimension_semantics=("arbitrary",) * n_grid,
                                vmem_limit_bytes=VMEM_LIMIT)


def _rms(x, g):
    return x * lax.rsqrt(jnp.mean(x * x, axis=-1, keepdims=True) + EPS) * g


def _const_spec(shape):
    nd = len(shape)
    return pl.BlockSpec(shape, lambda *_: (0,) * nd, pipeline_mode=pl.Buffered(1))


def _dot_nt(a, b):
    return lax.dot_general(a, b, (((1,), (1,)), ((), ())), preferred_element_type=F32)


def _mod_kernel(c_ref, w_ref, b_ref, o_ref):
    a = jax.nn.silu(c_ref[...]).astype(BF16)
    o_ref[...] = jnp.dot(a, w_ref[...].astype(BF16), preferred_element_type=F32) + b_ref[...]


def _modulation(cs, mod_w, mod_b, tn=1024):
    depth, d, n = mod_w.shape
    r = cs.shape[0]
    return pl.pallas_call(
        _mod_kernel,
        name="modulation",
        grid=(depth, n // tn),
        in_specs=[pl.BlockSpec((r, d), lambda l, j: (0, 0)),
                  pl.BlockSpec((None, d, tn), lambda l, j: (l, 0, j)),
                  pl.BlockSpec((None, 1, tn), lambda l, j: (l, 0, j))],
        out_specs=pl.BlockSpec((None, r, tn), lambda l, j: (l, 0, j)),
        out_shape=jax.ShapeDtypeStruct((depth, r, n), F32),
        compiler_params=_params(2),
    )(cs, mod_w, mod_b.reshape(depth, 1, n))


def _ffn_kernel(h_ref, mod_ref, gpre_ref, gpost_ref, w13_ref, w2_ref, o_ref, acc_ref, *, j, fc):
    x = h_ref[...]
    shift = mod_ref[3 * j:3 * j + 1, :]
    scale = mod_ref[3 * j + 1:3 * j + 2, :]
    gate = mod_ref[3 * j + 2:3 * j + 3, :]
    a = (_rms(x, gpre_ref[...]) * (1.0 + scale) + shift).astype(BF16)
    f = w2_ref.shape[0]
    for c in range(f // fc):
        g = jnp.dot(a, w13_ref[:, c * fc:(c + 1) * fc], preferred_element_type=F32)
        u = jnp.dot(a, w13_ref[:, f + c * fc:f + (c + 1) * fc], preferred_element_type=F32)
        act = (jax.nn.silu(g) * u).astype(BF16)
        contrib = jnp.dot(act, w2_ref[c * fc:(c + 1) * fc, :], preferred_element_type=F32)
        if c == 0:
            acc_ref[...] = contrib
        else:
            acc_ref[...] += contrib
    o_ref[...] = x + FFN_RES * gate * _rms(acc_ref[...], gpost_ref[...])


def _ffn(h, mod, j, g_pre, g_post, w13, w2, tm):
    b, s, d = h.shape
    f = w2.shape[0]
    fc = 256 if f % 256 == 0 else f
    tm = min(tm, s)
    return pl.pallas_call(
        functools.partial(_ffn_kernel, j=j, fc=fc),
        name="ffn",
        grid=(b, s // tm),
        in_specs=[pl.BlockSpec((None, tm, d), lambda bi, i: (bi, i, 0)),
                  pl.BlockSpec((None, N_MOD, d), lambda bi, i: (bi, 0, 0)),
                  _const_spec((1, d)), _const_spec((1, d)),
                  _const_spec(w13.shape), _const_spec(w2.shape)],
        out_specs=pl.BlockSpec((None, tm, d), lambda bi, i: (bi, i, 0)),
        out_shape=jax.ShapeDtypeStruct((b, s, d), F32),
        scratch_shapes=[pltpu.VMEM((tm, d), F32)],
        compiler_params=_params(2),
    )(h, mod, g_pre.reshape(1, d), g_post.reshape(1, d), w13, w2)


_O_CQ = 0
_O_CKV = _O_CQ + MLA_Q_LORA
_O_KPE = _O_CKV + MLA_KV_LORA
_O_KPE_SW = _O_KPE + LANES
_O_QS = _O_KPE_SW + LANES
_O_QS_SW = _O_QS + SWA_HEADS * SWA_HEAD_DIM
_O_KS = _O_QS_SW + SWA_HEADS * SWA_HEAD_DIM
_O_KS_SW = _O_KS + SWA_KV_HEADS * LANES
_W1_COLS = _O_KS_SW + SWA_KV_HEADS * LANES
_HL = MLA_HEADS * LANES
_QS_W = SWA_HEADS * SWA_HEAD_DIM
_KS_W = SWA_KV_HEADS * LANES


def _attn_proj_kernel(h_ref, mod_ref, gpre_ref, w1_ref, qn_ref, wq_ref, kvn_ref, wk_ref, wvt_ref, wvst_ref,
                      ca_ref, sa_ref, cb_ref, sb_ref,
                      qa_ref, ka_ref, vat_ref, qs_ref, ks_ref, vst_ref):
    x = h_ref[...]
    shift = mod_ref[3:4, :]
    scale = mod_ref[4:5, :]
    a = (_rms(x, gpre_ref[...]) * (1.0 + scale) + shift).astype(BF16)
    p = jnp.dot(a, w1_ref[...], preferred_element_type=F32)
    ca, sa, cb, sb = ca_ref[...], sa_ref[...], cb_ref[...], sb_ref[...]

    cqn = _rms(p[:, _O_CQ:_O_CQ + MLA_Q_LORA], qn_ref[...]).astype(BF16)
    q2 = jnp.dot(cqn, wq_ref[...], preferred_element_type=F32)
    ckvn = _rms(p[:, _O_CKV:_O_CKV + MLA_KV_LORA], kvn_ref[...]).astype(BF16)
    kn = jnp.dot(ckvn, wk_ref[...], preferred_element_type=F32)
    kpe = p[:, _O_KPE:_O_KPE + LANES] * ca + p[:, _O_KPE_SW:_O_KPE_SW + LANES] * sa
    for hd in range(MLA_HEADS):
        lo, hi = hd * LANES, (hd + 1) * LANES
        q = q2[:, lo:hi] * ca + q2[:, _HL + lo:_HL + hi] * sa
        qa_ref[:, lo:hi] = (q * (MLA_SCALE * math.log2(math.e))).astype(BF16)
        ka_ref[:, lo:hi] = (kn[:, lo:hi] + kpe).astype(BF16)
    vt = _dot_nt(wvt_ref[...], ckvn)
    row = lax.broadcasted_iota(jnp.int32, (_HL, 1), 0)
    vat_ref[...] = (vt + (jnp.bitwise_and(row, LANES - 1) == MLA_V).astype(F32)).astype(BF16)

    for t in range(_QS_W // LANES):
        lo, hi = t * LANES, (t + 1) * LANES
        q = p[:, _O_QS + lo:_O_QS + hi] * cb + p[:, _O_QS_SW + lo:_O_QS_SW + hi] * sb
        qs_ref[:, lo:hi] = (q * (SWA_SCALE * math.log2(math.e))).astype(BF16)
    for t in range(SWA_KV_HEADS):
        lo, hi = t * LANES, (t + 1) * LANES
        k = p[:, _O_KS + lo:_O_KS + hi] * cb + p[:, _O_KS_SW + lo:_O_KS_SW + hi] * sb
        ks_ref[:, lo:hi] = k.astype(BF16)
    vst = _dot_nt(wvst_ref[...], a)
    row = lax.broadcasted_iota(jnp.int32, (_KS_W, 1), 0)
    vst_ref[...] = (vst + (jnp.bitwise_and(row, LANES - 1) == SWA_HEAD_DIM).astype(F32)).astype(BF16)


def _attn_project(h, mod, g_pre, w1, q_norm, wq, kv_norm, wk, wvt, wvst, tabs, tm):
    b, s, d = h.shape
    tm = min(tm, s)
    row = lambda bi, i: (bi, i, 0)
    tab = pl.BlockSpec((tm, LANES), lambda bi, i: (i, 0))
    tok = lambda w: (pl.BlockSpec((None, tm, w), row), jax.ShapeDtypeStruct((b, s, w), BF16))
    chan = lambda w: (pl.BlockSpec((None, w, tm), lambda bi, i: (bi, 0, i)), jax.ShapeDtypeStruct((b, w, s), BF16))
    outs = [tok(_HL), tok(_HL), chan(_HL), tok(_QS_W), tok(_KS_W), chan(_KS_W)]
    return pl.pallas_call(
        _attn_proj_kernel,
        name="attn_project",
        grid=(b, s // tm),
        in_specs=[pl.BlockSpec((None, tm, d), row),
                  pl.BlockSpec((None, N_MOD, d), lambda bi, i: (bi, 0, 0)),
                  _const_spec((1, d)), _const_spec(w1.shape),
                  _const_spec((1, MLA_Q_LORA)), _const_spec(wq.shape),
                  _const_spec((1, MLA_KV_LORA)), _const_spec(wk.shape), _const_spec(wvt.shape),
                  _const_spec(wvst.shape), tab, tab, tab, tab],
        out_specs=[o[0] for o in outs],
        out_shape=[o[1] for o in outs],
        compiler_params=_params(2),
    )(h, mod, g_pre.reshape(1, d), w1, q_norm.reshape(1, -1), wq, kv_norm.reshape(1, -1), wk, wvt, wvst, *tabs)


def _mla_kernel(*refs, n_lat_chunks, tk):
    if n_lat_chunks:
        q_ref, kc_ref, vct_ref, kl_ref, vlt_ref, o_ref = refs
    else:
        q_ref, kc_ref, vct_ref, o_ref = refs
    q = q_ref[...]

    def scores(j):
        k = kc_ref[...] if j == 0 else kl_ref[(j - 1) * tk:j * tk, :]
        return _dot_nt(k, q)

    def update(j, st, m, acc):
        vt = vct_ref[...] if j == 0 else vlt_ref[:, (j - 1) * tk:j * tk]
        m_blk = jnp.max(st, axis=0, keepdims=True)
        m_new = m_blk if m is None else jnp.maximum(m, m_blk)
        pt = jnp.exp2(st - m_new).astype(BF16)
        pv = jnp.dot(vt, pt, preferred_element_type=F32)
        acc = pv if acc is None else acc * jnp.exp2(m - m_new) + pv
        return m_new, acc

    m, acc = None, None
    st_next = scores(0)
    for j in range(n_lat_chunks + 1):
        st = st_next
        if j < n_lat_chunks:
            st_next = scores(j + 1)
        m, acc = update(j, st, m, acc)
    o_ref[...] = (acc / acc[MLA_V:MLA_V + 1, :]).T.astype(o_ref.dtype)


def _mla(q, kc, vct, kl, vlt, tq, tk):
    b, s, _ = q.shape
    n_ctx = kc.shape[1]
    tq = min(tq, s)
    qspec = pl.BlockSpec((None, tq, LANES), lambda bi, h, i: (bi, i, h))
    in_specs = [qspec,
                pl.BlockSpec((None, n_ctx, LANES), lambda bi, h, i: (bi, 0, h)),
                pl.BlockSpec((None, LANES, n_ctx), lambda bi, h, i: (bi, h, 0))]
    args = [q, kc, vct]
    n_lat_chunks = 0
    if kl is not None:
        n_lat = kl.shape[1]
        tk = min(tk, n_lat)
        n_lat_chunks = n_lat // tk
        in_specs += [pl.BlockSpec((None, n_lat, LANES), lambda bi, h, i: (bi, 0, h)),
                     pl.BlockSpec((None, LANES, n_lat), lambda bi, h, i: (bi, h, 0))]
        args += [kl, vlt]
    return pl.pallas_call(
        functools.partial(_mla_kernel, n_lat_chunks=n_lat_chunks, tk=tk),
        name="mla_lat" if n_lat_chunks else "mla_ctx",
        grid=(b, MLA_HEADS, s // tq),
        in_specs=in_specs,
        out_specs=qspec,
        out_shape=jax.ShapeDtypeStruct((b, s, _HL), BF16),
        compiler_params=_params(3),
    )(*args)


def _swa_kernel(*refs, windowed, n_blocks):
    if windowed:
        (sink_ref, q_ref, kc_ref, vct_ref, kp_ref, k0_ref, kn_ref,
         vpt_ref, v0t_ref, vnt_ref, o_ref) = refs
    else:
        sink_ref, q_ref, kc_ref, vct_ref, o_ref = refs
    i = pl.program_id(1)
    tq = q_ref.shape[0]
    cols = SWA_GROUP * tq
    lane = lax.broadcasted_iota(jnp.int32, (1, LANES), 1)
    half_mask = [(lane < SWA_HEAD_DIM), (lane >= SWA_HEAD_DIM)]
    if windowed:
        j = lax.broadcasted_iota(jnp.int32, (3 * tq, cols), 0)
        r = jnp.bitwise_and(lax.broadcasted_iota(jnp.int32, (3 * tq, cols), 1), tq - 1)
        j_lo = jnp.where(i > 0, r, tq)
        j_hi = jnp.where(i < n_blocks - 1, r + 2 * tq, 2 * tq - 1)
    for kv in range(SWA_KV_HEADS):
        lo, hi = kv * LANES, (kv + 1) * LANES
        qs, sk = [], []
        for g in range(SWA_GROUP):
            hd = kv * SWA_GROUP + g
            t = hd // 2
            qt = q_ref[:, t * LANES:(t + 1) * LANES]
            qs.append(jnp.where(half_mask[hd % 2], qt, jnp.zeros_like(qt)))
            sk.append(jnp.full((1, tq), sink_ref[hd] * math.log2(math.e), F32))
        q4 = jnp.concatenate(qs, axis=0)
        sink = jnp.concatenate(sk, axis=1)
        scores = [_dot_nt(kc_ref[:, lo:hi], q4)]
        vals = [vct_ref[lo:hi, :]]
        if windowed:
            kw = jnp.concatenate([kp_ref[:, lo:hi], k0_ref[:, lo:hi], kn_ref[:, lo:hi]], axis=0)
            sw = _dot_nt(kw, q4)
            sw = jnp.where(j >= j_lo, jnp.where(j <= j_hi, sw, NEG_INF), NEG_INF)
            scores.append(sw)
            vals.append(jnp.concatenate([vpt_ref[lo:hi, :], v0t_ref[lo:hi, :], vnt_ref[lo:hi, :]], axis=1))
        m = sink
        for s_ in scores:
            m = jnp.maximum(m, jnp.max(s_, axis=0, keepdims=True))
        acc = None
        for s_, vt in zip(scores, vals):
            pv = jnp.dot(vt, jnp.exp2(s_ - m).astype(BF16), preferred_element_type=F32)
            acc = pv if acc is None else acc + pv
        denom = acc[SWA_HEAD_DIM:SWA_HEAD_DIM + 1, :] + jnp.exp2(sink - m)
        o = acc / denom
        for g in range(SWA_GROUP):
            hd = kv * SWA_GROUP + g
            o_ref[:, hd * LANES:(hd + 1) * LANES] = o[:, g * tq:(g + 1) * tq].T.astype(o_ref.dtype)


def _swa(sink, q, kc, vct, kl, vlt):
    b, s, _ = q.shape
    n_ctx = kc.shape[1]
    windowed = kl is not None
    tq = Q_BLOCK if windowed else s
    nb = s // tq
    in_specs = [pl.BlockSpec(memory_space=pltpu.SMEM),
                pl.BlockSpec((None, tq, _QS_W), lambda bi, i: (bi, i, 0)),
                pl.BlockSpec((None, n_ctx, _KS_W), lambda bi, i: (bi, 0, 0)),
                pl.BlockSpec((None, _KS_W, n_ctx), lambda bi, i: (bi, 0, 0))]
    args = [sink, q, kc, vct]
    if windowed:
        blocks = (lambda i: jnp.maximum(i - 1, 0), lambda i: i, lambda i: jnp.minimum(i + 1, nb - 1))
        in_specs += [pl.BlockSpec((None, tq, _KS_W), lambda bi, i, f=f: (bi, f(i), 0)) for f in blocks]
        in_specs += [pl.BlockSpec((None, _KS_W, tq), lambda bi, i, f=f: (bi, 0, f(i))) for f in blocks]
        args += [kl, kl, kl, vlt, vlt, vlt]
    return pl.pallas_call(
        functools.partial(_swa_kernel, windowed=windowed, n_blocks=nb),
        name="swa_lat" if windowed else "swa_ctx",
        grid=(b, nb),
        in_specs=in_specs,
        out_specs=pl.BlockSpec((None, tq, SWA_HEADS * LANES), lambda bi, i: (bi, i, 0)),
        out_shape=jax.ShapeDtypeStruct((b, s, SWA_HEADS * LANES), BF16),
        compiler_params=_params(2),
    )(*args)


def _attn_out_kernel(oa_ref, ob_ref, h_ref, mod_ref, gpost_ref, wa_ref, wb_ref, o_ref):
    y = jnp.dot(oa_ref[...], wa_ref[...], preferred_element_type=F32)
    y = y + jnp.dot(ob_ref[...], wb_ref[...], preferred_element_type=F32)
    o_ref[...] = h_ref[...] + mod_ref[5:6, :] * _rms(y, gpost_ref[...])


def _attn_out(oa, ob, h, mod, g_post, wa, wb, tm):
    b, s, d = h.shape
    tm = min(tm, s)
    row = lambda bi, i: (bi, i, 0)
    return pl.pallas_call(
        _attn_out_kernel,
        name="attn_out",
        grid=(b, s // tm),
        in_specs=[pl.BlockSpec((None, tm, oa.shape[2]), row),
                  pl.BlockSpec((None, tm, ob.shape[2]), row),
                  pl.BlockSpec((None, tm, d), row),
                  pl.BlockSpec((None, N_MOD, d), lambda bi, i: (bi, 0, 0)),
                  _const_spec((1, d)), _const_spec(wa.shape), _const_spec(wb.shape)],
        out_specs=pl.BlockSpec((None, tm, d), row),
        out_shape=jax.ShapeDtypeStruct((b, s, d), F32),
        compiler_params=_params(2),
    )(oa, ob, h, mod, g_post.reshape(1, d), wa, wb)


def _s5_in_kernel(h_ref, mod_ref, gpre_ref, wt_ref, o_ref):
    x = h_ref[...]
    a = (_rms(x, gpre_ref[...]) * (1.0 + mod_ref[4:5, :]) + mod_ref[3:4, :]).astype(BF16)
    o_ref[...] = _dot_nt(wt_ref[...], a)


def _s5_in(h, mod, g_pre, w_in_t, tm):
    b, s, d = h.shape
    w = w_in_t.shape[0]
    tm = min(tm, s)
    return pl.pallas_call(
        _s5_in_kernel,
        name="s5_in",
        grid=(b, s // tm),
        in_specs=[pl.BlockSpec((None, tm, d), lambda bi, i: (bi, i, 0)),
                  pl.BlockSpec((None, N_MOD, d), lambda bi, i: (bi, 0, 0)),
                  _const_spec((1, d)), _const_spec(w_in_t.shape)],
        out_specs=pl.BlockSpec((None, w, tm), lambda bi, i: (bi, 0, i)),
        out_shape=jax.ShapeDtypeStruct((b, w, s), F32),
        compiler_params=_params(2),
    )(h, mod, g_pre.reshape(1, d), w_in_t)


def _s5_core_kernel(u_ref, uc_ref, kk_ref, et_ref, ft_ref, ap_ref, dsk_ref, o_ref,
                    er_ref, ei_ref, xrf_ref, xrb_ref, xif_ref, xib_ref, m_ref, *, nb, nc, ncc):
    h = S5_GROUP
    rows = nb * nc

    mi = lax.broadcasted_iota(jnp.int32, (S5_CHUNK, S5_CHUNK), 0)
    li = lax.broadcasted_iota(jnp.int32, (S5_CHUNK, S5_CHUNK), 1)
    causal = li >= mi

    def build(i, j):
        lags = kk_ref[i, j:j + 1, :]
        fwd = pltpu.roll(jnp.broadcast_to(lags[:, S5_CHUNK:], (S5_CHUNK, S5_CHUNK)), 0, 1,
                         stride=1, stride_axis=0)
        bwd = pltpu.roll(jnp.broadcast_to(lags[:, :S5_CHUNK], (S5_CHUNK, S5_CHUNK)), 0, 1,
                         stride=1, stride_axis=0)
        m_ref[i * S5_CHUNK:(i + 1) * S5_CHUNK, j * S5_CHUNK:(j + 1) * S5_CHUNK] = (
            jnp.where(causal, fwd, bwd).astype(BF16))

    u32 = jnp.concatenate([u_ref[:, i].reshape(rows, S5_CHUNK) for i in range(h)], axis=1)
    ub = u32.astype(BF16)
    et = et_ref[...]
    ein = jnp.dot(ub, et, preferred_element_type=F32)
    er_ref[...] = ein[:, :LANES]
    ei_ref[...] = ein[:, LANES:]
    ar, ai = ap_ref[0:1, :], ap_ref[1:2, :]
    fwd_lane = lax.broadcasted_iota(jnp.int32, (1, LANES), 1) < S5_STATE

    def step(xr, xi, e_r, e_i):
        return xr * ar - xi * ai + e_r, xr * ai + xi * ar + e_i

    xr = jnp.zeros((nb, LANES), F32)
    xi = jnp.zeros((nb, LANES), F32)
    for c in range(ncc):
        ecf = jnp.dot(uc_ref[c].astype(BF16), et, preferred_element_type=F32)
        ecb = jnp.dot(uc_ref[ncc - 1 - c].astype(BF16), et, preferred_element_type=F32)
        xr, xi = step(xr, xi, jnp.where(fwd_lane, ecf[:, :LANES], ecb[:, :LANES]),
                      jnp.where(fwd_lane, ecf[:, LANES:], ecb[:, LANES:]))

    for t in range(nc):
        cf = pl.ds(t, nb, stride=nc)
        cb = pl.ds(nc - 1 - t, nb, stride=nc)
        xrf_ref[cf, :] = xr
        xrb_ref[cb, :] = xr
        xif_ref[cf, :] = xi
        xib_ref[cb, :] = xi
        xr, xi = step(xr, xi, jnp.where(fwd_lane, er_ref[cf, :], er_ref[cb, :]),
                      jnp.where(fwd_lane, ei_ref[cf, :], ei_ref[cb, :]))
    xs = jnp.concatenate([jnp.where(fwd_lane, xrf_ref[...], xrb_ref[...]),
                          jnp.where(fwd_lane, xif_ref[...], xib_ref[...])], axis=1).astype(BF16)

    cw = 2 * S5_CHUNK
    for jp in range(h // 2):
        for i in range(h):
            build(i, 2 * jp)
            build(i, 2 * jp + 1)
        cs = slice(jp * cw, (jp + 1) * cw)
        y = jnp.dot(ub, m_ref[:, cs], preferred_element_type=F32)
        y = y + jnp.dot(xs, ft_ref[:, cs], preferred_element_type=F32)
        y = y + u32[:, cs] * dsk_ref[:, cs]
        o_ref[:, 2 * jp] = y[:, :S5_CHUNK].reshape(nb, nc, S5_CHUNK)
        o_ref[:, 2 * jp + 1] = y[:, S5_CHUNK:].reshape(nb, nc, S5_CHUNK)


def _s5_core(u_t, uc, kk, et, ft, ap, dsk):
    b, w, s = u_t.shape
    g = w // S5_GROUP
    nc = s // S5_CHUNK
    ncc = uc.shape[1]
    hl = S5_GROUP * S5_CHUNK
    u5 = u_t.reshape(b, g, S5_GROUP, nc, S5_CHUNK)
    blk = pl.BlockSpec((b, None, S5_GROUP, nc, S5_CHUNK), lambda gi: (0, gi, 0, 0, 0))
    per_g = lambda shape: pl.BlockSpec((None,) + shape, lambda gi: (gi,) + (0,) * len(shape))
    out = pl.pallas_call(
        functools.partial(_s5_core_kernel, nb=b, nc=nc, ncc=ncc),
        name="s5_core",
        grid=(g,),
        in_specs=[blk, per_g((ncc, b, hl)),
                  per_g((S5_GROUP, S5_GROUP, 2 * S5_CHUNK)),
                  per_g((hl, 2 * LANES)), per_g((2 * LANES, hl)),
                  per_g((2, LANES)), per_g((1, hl))],
        out_specs=blk,
        out_shape=jax.ShapeDtypeStruct(u5.shape, F32),
        scratch_shapes=[pltpu.VMEM((b * nc, LANES), F32) for _ in range(6)]
        + [pltpu.VMEM((hl, hl), BF16)],
        compiler_params=_params(1),
    )(u5, uc, kk, et, ft, ap, dsk)
    return out.reshape(b, w, s)


def _s5_out_kernel(y_ref, h_ref, mod_ref, gpost_ref, wt_ref, o_ref):
    d = h_ref.shape[1]
    gy = jax.nn.gelu(y_ref[...]).astype(BF16)
    z = jnp.dot(wt_ref[...], gy, preferred_element_type=F32)
    v = (z[:d, :] * jax.nn.sigmoid(z[d:, :])).T
    o_ref[...] = h_ref[...] + mod_ref[5:6, :] * _rms(v, gpost_ref[...])


def _s5_out(y_t, h, mod, g_post, w_glu_t, tm):
    b, s, d = h.shape
    w = y_t.shape[1]
    tm = min(tm, s)
    return pl.pallas_call(
        _s5_out_kernel,
        name="s5_out",
        grid=(b, s // tm),
        in_specs=[pl.BlockSpec((None, w, tm), lambda bi, i: (bi, 0, i)),
                  pl.BlockSpec((None, tm, d), lambda bi, i: (bi, i, 0)),
                  pl.BlockSpec((None, N_MOD, d), lambda bi, i: (bi, 0, 0)),
                  _const_spec((1, d)), _const_spec(w_glu_t.shape)],
        out_specs=pl.BlockSpec((None, tm, d), lambda bi, i: (bi, i, 0)),
        out_shape=jax.ShapeDtypeStruct((b, s, d), F32),
        compiler_params=_params(2),
    )(y_t, h, mod, g_post.reshape(1, d), w_glu_t)


def _rope_angles(n, d_rot):
    d_axis = d_rot // 2
    inv = ROPE_BASE ** (-jnp.arange(0, d_axis, 2, dtype=F32) / d_axis)
    n_rows = n // GRID_W
    row_ang = jnp.arange(n_rows, dtype=F32)[:, None] * inv
    col_ang = jnp.arange(GRID_W, dtype=F32)[:, None] * inv

    def grid(fr, fc):
        r = jnp.broadcast_to(fr[:, None, :], (n_rows, GRID_W, fr.shape[1]))
        c = jnp.broadcast_to(fc[None, :, :], (n_rows, GRID_W, fc.shape[1]))
        return jnp.concatenate([r, c], axis=-1).reshape(n, -1)

    return grid(jnp.cos(row_ang), jnp.cos(col_ang)), grid(jnp.sin(row_ang), jnp.sin(col_ang))


def _rope_tables(n, rotate):
    pad = LANES - MLA_NOPE - MLA_ROPE
    if rotate:
        ca, sa = _rope_angles(n, MLA_ROPE)
        cb, sb = _rope_angles(n, SWA_HEAD_DIM)
    else:
        ca, sa = jnp.ones((n, MLA_ROPE // 2), F32), jnp.zeros((n, MLA_ROPE // 2), F32)
        cb, sb = jnp.ones((n, SWA_HEAD_DIM // 2), F32), jnp.zeros((n, SWA_HEAD_DIM // 2), F32)
    one, zero = jnp.ones((n, MLA_NOPE), F32), jnp.zeros((n, MLA_NOPE), F32)
    zpad = jnp.zeros((n, pad), F32)
    return (jnp.concatenate([one, ca, ca, zpad], axis=1), jnp.concatenate([zero, sa, sa, zpad], axis=1),
            jnp.concatenate([cb] * 4, axis=1), jnp.concatenate([sb] * 4, axis=1))


def _rot_partner(w, half):
    return jnp.concatenate([-w[..., half:], w[..., :half]], axis=-1)


def _attn_weights(w_in, w_uq, w_ukv, w_out):
    d = w_in.shape[0]
    sizes = [MLA_Q_LORA, MLA_KV_LORA, MLA_ROPE, SWA_HEADS * SWA_HEAD_DIM,
             SWA_KV_HEADS * SWA_HEAD_DIM, SWA_KV_HEADS * SWA_HEAD_DIM]
    cq, ckv, kpe, qs, ks, vs = jnp.split(w_in, [int(v) for v in np.cumsum(sizes)[:-1]], axis=1)
    pad_a = LANES - MLA_NOPE - MLA_ROPE
    z = lambda *shape: jnp.zeros(shape, F32)
    kpe_blk = jnp.concatenate([z(d, MLA_NOPE), kpe, z(d, pad_a)], axis=1)
    kpe_sw = jnp.concatenate([z(d, MLA_NOPE), _rot_partner(kpe, MLA_ROPE // 2), z(d, pad_a)], axis=1)
    qs3 = qs.reshape(d, SWA_HEADS, SWA_HEAD_DIM)
    qs_sw = _rot_partner(qs3, SWA_HEAD_DIM // 2).reshape(d, -1)
    ks3 = ks.reshape(d, SWA_KV_HEADS, SWA_HEAD_DIM)
    ks_sw3 = _rot_partner(ks3, SWA_HEAD_DIM // 2)
    dup = lambda t: jnp.concatenate([t, t], axis=-1).reshape(d, -1)
    vs3 = vs.reshape(d, SWA_KV_HEADS, SWA_HEAD_DIM)
    vs_pad = jnp.concatenate([vs3, jnp.zeros_like(vs3)], axis=-1).reshape(d, -1)
    w1 = jnp.concatenate([cq, ckv, kpe_blk, kpe_sw, qs, qs_sw, dup(ks3), dup(ks_sw3)], axis=1)
    wvst = vs_pad.T
    assert w1.shape[1] == _W1_COLS

    ql = w_uq.shape[0]
    uq = w_uq.reshape(ql, MLA_HEADS, MLA_NOPE + MLA_ROPE)
    nope, pe = uq[..., :MLA_NOPE], uq[..., MLA_NOPE:]
    zq = jnp.zeros((ql, MLA_HEADS, pad_a), F32)
    wq_blk = jnp.concatenate([nope, pe, zq], axis=-1).reshape(ql, -1)
    wq_sw = jnp.concatenate([jnp.zeros_like(nope), _rot_partner(pe, MLA_ROPE // 2), zq], axis=-1).reshape(ql, -1)
    wq = jnp.concatenate([wq_blk, wq_sw], axis=1)

    kl = w_ukv.shape[0]
    ukv = w_ukv.reshape(kl, MLA_HEADS, MLA_NOPE + MLA_V)
    kn, vv = ukv[..., :MLA_NOPE], ukv[..., MLA_NOPE:]
    wk = jnp.concatenate([kn, jnp.zeros((kl, MLA_HEADS, LANES - MLA_NOPE), F32)], axis=-1).reshape(kl, -1)
    wv = jnp.concatenate([vv, jnp.zeros((kl, MLA_HEADS, LANES - MLA_V), F32)], axis=-1).reshape(kl, -1)
    wvt = wv.T

    dm = w_out.shape[1]
    na = MLA_HEADS * MLA_V
    oa = w_out[:na].reshape(MLA_HEADS, MLA_V, dm)
    ob = w_out[na:].reshape(SWA_HEADS, SWA_HEAD_DIM, dm)
    wa = jnp.concatenate([oa, jnp.zeros((MLA_HEADS, LANES - MLA_V, dm), F32)], axis=1).reshape(-1, dm)
    wb = jnp.concatenate([ob, jnp.zeros((SWA_HEADS, LANES - SWA_HEAD_DIM, dm), F32)], axis=1).reshape(-1, dm)
    return tuple(t.astype(BF16) for t in (w1, wq, wk, wvt, wvst, wa, wb))


def _complex_powers(a_re, a_im, n):
    pr, pi = jnp.ones_like(a_re)[None], jnp.zeros_like(a_im)[None]
    sr, si = a_re, a_im
    while pr.shape[0] < n + 1:
        nr, ni = pr * sr - pi * si, pr * si + pi * sr
        pr, pi = jnp.concatenate([pr, nr], 0), jnp.concatenate([pi, ni], 0)
        sr, si = sr * sr - si * si, 2.0 * sr * si
    return pr[:n + 1], pi[:n + 1]


def _s5_operators(lam_re, lam_im, b_re, b_im, c_re, c_im, log_step):
    hi = lax.Precision.HIGHEST
    ln = S5_CHUNK
    g, p = lam_re.shape[1:]
    hh = b_re.shape[-1]
    ks, ets, fts, aps = [], [], [], []
    for dr in range(2):
        lre = jnp.minimum(lam_re[dr], S5_MAX_RE)
        lim = lam_im[dr]
        dt = jnp.exp(log_step[dr])[:, None]
        mag = jnp.exp(lre * dt)
        a_re, a_im = mag * jnp.cos(lim * dt), mag * jnp.sin(lim * dt)
        den = lre * lre + lim * lim
        f_re = ((a_re - 1.0) * lre + a_im * lim) / den
        f_im = (a_im * lre - (a_re - 1.0) * lim) / den
        bb_re = f_re[..., None] * b_re[dr] - f_im[..., None] * b_im[dr]
        bb_im = f_re[..., None] * b_im[dr] + f_im[..., None] * b_re[dr]
        cr, ci = c_re[dr], c_im[dr]
        pr, pi = _complex_powers(a_re, a_im, ln)
        cb_re = jnp.einsum('gjp,gpi->gpji', cr, bb_re) - jnp.einsum('gjp,gpi->gpji', ci, bb_im)
        cb_im = jnp.einsum('gjp,gpi->gpji', cr, bb_im) + jnp.einsum('gjp,gpi->gpji', ci, bb_re)
        k = (jnp.einsum('dgp,gpji->gijd', pr[:ln], cb_re, precision=hi)
             - jnp.einsum('dgp,gpji->gijd', pi[:ln], cb_im, precision=hi))
        ks.append(k)
        er, ei = (pr[:ln][::-1], pi[:ln][::-1]) if dr == 0 else (pr[:ln], pi[:ln])
        e_re = jnp.einsum('mgp,gpi->gimp', er, bb_re) - jnp.einsum('mgp,gpi->gimp', ei, bb_im)
        e_im = jnp.einsum('mgp,gpi->gimp', er, bb_im) + jnp.einsum('mgp,gpi->gimp', ei, bb_re)
        ets.append((e_re.reshape(g, hh * ln, p), e_im.reshape(g, hh * ln, p)))
        fr, fi = (pr[1:], pi[1:]) if dr == 0 else (pr[1:][::-1], pi[1:][::-1])
        f_xre = jnp.einsum('gjp,lgp->gpjl', cr, fr) - jnp.einsum('gjp,lgp->gpjl', ci, fi)
        f_xim = -(jnp.einsum('gjp,lgp->gpjl', cr, fi) + jnp.einsum('gjp,lgp->gpjl', ci, fr))
        fts.append((f_xre.reshape(g, p, hh * ln), f_xim.reshape(g, p, hh * ln)))
        aps.append((pr[ln], pi[ln]))
    kf, kb = ks
    k0 = kf[..., :1] + kb[..., :1]
    kk = jnp.concatenate([jnp.zeros_like(k0), kb[..., :0:-1], k0, kf[..., 1:]], axis=-1)
    et = jnp.concatenate([ets[0][0], ets[1][0], ets[0][1], ets[1][1]], axis=-1).astype(BF16)
    ft = jnp.concatenate([fts[0][0], fts[1][0], fts[0][1], fts[1][1]], axis=1).astype(BF16)
    ap = jnp.stack([jnp.concatenate([aps[0][0], aps[1][0]], axis=-1),
                    jnp.concatenate([aps[0][1], aps[1][1]], axis=-1)], axis=1)
    return kk, et, ft, ap


def kernel(x, c, ctx, c_ctx, mod_w, mod_b, norm_pre, norm_post, ffn_w13, ffn_w2,
           attn_w_in, mla_q_norm, mla_w_uq, mla_kv_norm, mla_w_ukv, swa_sink, attn_w_out,
           s5_w_in, s5_lambda_re, s5_lambda_im, s5_b_re, s5_b_im, s5_c_re, s5_c_im,
           s5_log_step, s5_d, s5_w_glu):
    b, n, d = x.shape
    n_ctx = ctx.shape[1]
    depth = mod_w.shape[0]
    tm = 512

    rows = -(-(b + 1) // 8) * 8
    cs = jnp.concatenate([c, c_ctx[None], jnp.zeros((rows - b - 1, d), F32)], axis=0)
    mods = _modulation(cs, mod_w, mod_b)
    w13 = ffn_w13.astype(BF16)
    w2 = ffn_w2.astype(BF16)

    h_lat, h_ctx = x, ctx
    for l in range(depth):
        last = l == depth - 1
        m_lat = mods[l, :b].reshape(b, N_MOD, d)
        m_ctx = jnp.broadcast_to(mods[l, b].reshape(1, N_MOD, d), (b, N_MOD, d))
        h_lat = _ffn(h_lat, m_lat, 0, norm_pre[l, 0], norm_post[l, 0], w13[l, 0], w2[l, 0], tm)
        h_ctx = _ffn(h_ctx, m_ctx, 0, norm_pre[l, 0], norm_post[l, 0], w13[l, 0], w2[l, 0], tm)
        if l % 2 == 0:
            e = l // 2
            w1, wq, wk, wvt, wvst, wa, wb = _attn_weights(attn_w_in[e], mla_w_uq[e], mla_w_ukv[e], attn_w_out[e])
            proj = functools.partial(_attn_project, g_pre=norm_pre[l, 1], w1=w1, q_norm=mla_q_norm[e],
                                     wq=wq, kv_norm=mla_kv_norm[e], wk=wk, wvt=wvt, wvst=wvst, tm=tm)
            qa_c, ka_c, va_c, qs_c, ks_c, vs_c = proj(h_ctx, m_ctx, tabs=_rope_tables(n_ctx, False))
            qa_l, ka_l, va_l, qs_l, ks_l, vs_l = proj(h_lat, m_lat, tabs=_rope_tables(n, True))
            o_a = _mla(qa_l, ka_c, va_c, ka_l, va_l, tq=1024, tk=512)
            o_b = _swa(swa_sink[e], qs_l, ks_c, vs_c, ks_l, vs_l)
            h_lat = _attn_out(o_a, o_b, h_lat, m_lat, norm_post[l, 1], wa, wb, tm)
            if not last:
                o_a_c = _mla(qa_c, ka_c, va_c, None, None, tq=1024, tk=512)
                o_b_c = _swa(swa_sink[e], qs_c, ks_c, vs_c, None, None)
                h_ctx = _attn_out(o_a_c, o_b_c, h_ctx, m_ctx, norm_post[l, 1], wa, wb, tm)
        else:
            o = l // 2
            assert last, "S5 context outputs are only needed when another layer follows"
            w_in_t = s5_w_in[o].T.astype(BF16)
            w_glu_t = s5_w_glu[o].T.astype(BF16)
            kk, et, ft, ap = _s5_operators(s5_lambda_re[o], s5_lambda_im[o], s5_b_re[o], s5_b_im[o],
                                             s5_c_re[o], s5_c_im[o], s5_log_step[o])
            width = s5_w_in.shape[2]
            g = width // S5_GROUP
            u_lat = _s5_in(h_lat, m_lat, norm_pre[l, 1], w_in_t, tm)
            u_ctx = _s5_in(h_ctx, m_ctx, norm_pre[l, 1], w_in_t, tm)
            ncc = n_ctx // S5_CHUNK
            uc = u_ctx.reshape(b, g, S5_GROUP, ncc, S5_CHUNK).transpose(1, 3, 0, 2, 4)
            uc = uc.reshape(g, ncc, b, S5_GROUP * S5_CHUNK)
            dsk = jnp.repeat(s5_d[o].astype(F32), S5_CHUNK).reshape(g, 1, S5_GROUP * S5_CHUNK)
            y_t = _s5_core(u_lat, uc, kk, et, ft, ap, dsk)
            h_lat = _s5_out(y_t, h_lat, m_lat, norm_post[l, 1], w_glu_t, tm)
        h_lat = _ffn(h_lat, m_lat, 2, norm_pre[l, 2], norm_post[l, 2], w13[l, 1], w2[l, 1], tm)
        if not last:
            h_ctx = _ffn(h_ctx, m_ctx, 2, norm_pre[l, 2], norm_post[l, 2], w13[l, 1], w2[l, 1], tm)
    return h_lat
```

```python
import functools
import math

import numpy as np
import jax
import jax.numpy as jnp
from jax import lax
from jax.experimental import pallas as pl
from jax.experimental.pallas import tpu as pltpu

F32 = jnp.float32
BF16 = jnp.bfloat16

LANES = 128
VMEM_LIMIT = 56 * 1024 * 1024

N_MOD = 9
FFN_RES = 0.5
EPS = 1e-6
ROPE_BASE = 10000.0
GRID_W = 64
NEG_INF = -1e30

MLA_HEADS = 8
MLA_Q_LORA = 256
MLA_KV_LORA = 128
MLA_NOPE = 64
MLA_ROPE = 32
MLA_V = 64
MLA_SCALE = (MLA_NOPE + MLA_ROPE) ** -0.5

SWA_HEADS = 8
SWA_KV_HEADS = 2
SWA_GROUP = SWA_HEADS // SWA_KV_HEADS
SWA_HEAD_DIM = 64
SWA_WINDOW = 128
SWA_SCALE = SWA_HEAD_DIM ** -0.5
Q_BLOCK = 128

S5_GROUP = 16
S5_STATE = 64
S5_CHUNK = 128
S5_MAX_RE = -1e-4


def _params(n_grid):
    return pltpu.CompilerParams(dimension_semantics=("arbitrary",) * n_grid,
                                vmem_limit_bytes=VMEM_LIMIT)


def _rms(x, g):
    return x * lax.rsqrt(jnp.mean(x * x, axis=-1, keepdims=True) + EPS) * g


def _const_spec(shape):
    nd = len(shape)
    return pl.BlockSpec(shape, lambda *_: (0,) * nd, pipeline_mode=pl.Buffered(1))


def _dot_nt(a, b):
    return lax.dot_general(a, b, (((1,), (1,)), ((), ())), preferred_element_type=F32)


def _mod_kernel(c_ref, w_ref, b_ref, o_ref):
    a = jax.nn.silu(c_ref[...]).astype(BF16)
    o_ref[...] = jnp.dot(a, w_ref[...].astype(BF16), preferred_element_type=F32) + b_ref[...]


def _modulation(cs, mod_w, mod_b, tn=1024):
    depth, d, n = mod_w.shape
    r = cs.shape[0]
    return pl.pallas_call(
        _mod_kernel,
        name="modulation",
        grid=(depth, n // tn),
        in_specs=[pl.BlockSpec((r, d), lambda l, j: (0, 0)),
                  pl.BlockSpec((None, d, tn), lambda l, j: (l, 0, j)),
                  pl.BlockSpec((None, 1, tn), lambda l, j: (l, 0, j))],
        out_specs=pl.BlockSpec((None, r, tn), lambda l, j: (l, 0, j)),
        out_shape=jax.ShapeDtypeStruct((depth, r, n), F32),
        compiler_params=_params(2),
    )(cs, mod_w, mod_b.reshape(depth, 1, n))


class _Stage:
    def __init__(self, body, ins, outs=()):
        self.body, self.ins, self.outs = body, list(ins), list(outs)


def _ffn_kernel(*refs, j, fc, pre, post):
    h_ref, mod_ref, gpre_ref, gpost_ref, w13_ref, w2_ref = refs[:6]
    n_pre = len(pre.ins) if pre else 0
    n_post = len(post.ins) if post else 0
    pre_refs = refs[6:6 + n_pre]
    post_refs = refs[6 + n_pre:6 + n_pre + n_post]
    o_ref = refs[6 + n_pre + n_post]
    post_outs = refs[7 + n_pre + n_post:-1]
    acc_ref = refs[-1]
    x = h_ref[...]
    if pre:
        x = pre.body(x, mod_ref, *pre_refs)
    shift = mod_ref[3 * j:3 * j + 1, :]
    scale = mod_ref[3 * j + 1:3 * j + 2, :]
    gate = mod_ref[3 * j + 2:3 * j + 3, :]
    a = (_rms(x, gpre_ref[...]) * (1.0 + scale) + shift).astype(BF16)
    f = w2_ref.shape[0]
    for c in range(f // fc):
        g = jnp.dot(a, w13_ref[:, c * fc:(c + 1) * fc], preferred_element_type=F32)
        u = jnp.dot(a, w13_ref[:, f + c * fc:f + (c + 1) * fc], preferred_element_type=F32)
        act = (jax.nn.silu(g) * u).astype(BF16)
        contrib = jnp.dot(act, w2_ref[c * fc:(c + 1) * fc, :], preferred_element_type=F32)
        if c == 0:
            acc_ref[...] = contrib
        else:
            acc_ref[...] += contrib
    out = x + FFN_RES * gate * _rms(acc_ref[...], gpost_ref[...])
    o_ref[...] = out
    if post:
        post.body(out, mod_ref, *post_refs, *post_outs)


def _ffn(h, mod, j, g_pre, g_post, w13, w2, tm, pre=None, post=None):
    b, s, d = h.shape
    f = w2.shape[0]
    fc = 256 if f % 256 == 0 else f
    tm = min(tm, s)
    pre = pre(b, s, tm) if pre else None
    post = post(b, s, tm) if post else None
    extra = (pre.ins if pre else []) + (post.ins if post else [])
    outs = [(pl.BlockSpec((None, tm, d), lambda bi, i: (bi, i, 0)), jax.ShapeDtypeStruct((b, s, d), F32))]
    outs += post.outs if post else []
    res = pl.pallas_call(
        functools.partial(_ffn_kernel, j=j, fc=fc, pre=pre, post=post),
        name="ffn",
        grid=(b, s // tm),
        in_specs=[pl.BlockSpec((None, tm, d), lambda bi, i: (bi, i, 0)),
                  pl.BlockSpec((None, N_MOD, d), lambda bi, i: (bi, 0, 0)),
                  _const_spec((1, d)), _const_spec((1, d)),
                  _const_spec(w13.shape), _const_spec(w2.shape)] + [e[0] for e in extra],
        out_specs=[o[0] for o in outs],
        out_shape=[o[1] for o in outs],
        scratch_shapes=[pltpu.VMEM((tm, d), F32)],
        compiler_params=_params(2),
    )(h, mod, g_pre.reshape(1, d), g_post.reshape(1, d), w13, w2, *[e[1] for e in extra])
    return res if post else res[0]


_O_CQ = 0
_O_CKV = _O_CQ + MLA_Q_LORA
_O_KPE = _O_CKV + MLA_KV_LORA
_O_KPE_SW = _O_KPE + LANES
_O_QS = _O_KPE_SW + LANES
_O_QS_SW = _O_QS + SWA_HEADS * SWA_HEAD_DIM
_O_KS = _O_QS_SW + SWA_HEADS * SWA_HEAD_DIM
_O_KS_SW = _O_KS + SWA_KV_HEADS * LANES
_W1_COLS = _O_KS_SW + SWA_KV_HEADS * LANES
_HL = MLA_HEADS * LANES
_QS_W = SWA_HEADS * SWA_HEAD_DIM
_KS_W = SWA_KV_HEADS * LANES


def _attn_proj_body(x, mod_ref, gpre_ref, w1_ref, qn_ref, wq_ref, kvn_ref, wk_ref, wvt_ref, wvst_ref,
                    ca_ref, sa_ref, cb_ref, sb_ref,
                    qa_ref, ka_ref, vat_ref, qs_ref, ks_ref, vst_ref):
    shift = mod_ref[3:4, :]
    scale = mod_ref[4:5, :]
    a = (_rms(x, gpre_ref[...]) * (1.0 + scale) + shift).astype(BF16)
    p = jnp.dot(a, w1_ref[...], preferred_element_type=F32)
    ca, sa, cb, sb = ca_ref[...], sa_ref[...], cb_ref[...], sb_ref[...]

    cqn = _rms(p[:, _O_CQ:_O_CQ + MLA_Q_LORA], qn_ref[...]).astype(BF16)
    q2 = jnp.dot(cqn, wq_ref[...], preferred_element_type=F32)
    ckvn = _rms(p[:, _O_CKV:_O_CKV + MLA_KV_LORA], kvn_ref[...]).astype(BF16)
    kn = jnp.dot(ckvn, wk_ref[...], preferred_element_type=F32)
    kpe = p[:, _O_KPE:_O_KPE + LANES] * ca + p[:, _O_KPE_SW:_O_KPE_SW + LANES] * sa
    for hd in range(MLA_HEADS):
        lo, hi = hd * LANES, (hd + 1) * LANES
        q = q2[:, lo:hi] * ca + q2[:, _HL + lo:_HL + hi] * sa
        qa_ref[:, lo:hi] = (q * (MLA_SCALE * math.log2(math.e))).astype(BF16)
        ka_ref[:, lo:hi] = (kn[:, lo:hi] + kpe).astype(BF16)
    vt = _dot_nt(wvt_ref[...], ckvn)
    row = lax.broadcasted_iota(jnp.int32, (_HL, 1), 0)
    vat_ref[...] = (vt + (jnp.bitwise_and(row, LANES - 1) == MLA_V).astype(F32)).astype(BF16)

    for t in range(_QS_W // LANES):
        lo, hi = t * LANES, (t + 1) * LANES
        q = p[:, _O_QS + lo:_O_QS + hi] * cb + p[:, _O_QS_SW + lo:_O_QS_SW + hi] * sb
        qs_ref[:, lo:hi] = (q * (SWA_SCALE * math.log2(math.e))).astype(BF16)
    for t in range(SWA_KV_HEADS):
        lo, hi = t * LANES, (t + 1) * LANES
        k = p[:, _O_KS + lo:_O_KS + hi] * cb + p[:, _O_KS_SW + lo:_O_KS_SW + hi] * sb
        ks_ref[:, lo:hi] = k.astype(BF16)
    vst = _dot_nt(wvst_ref[...], a)
    row = lax.broadcasted_iota(jnp.int32, (_KS_W, 1), 0)
    vst_ref[...] = (vst + (jnp.bitwise_and(row, LANES - 1) == SWA_HEAD_DIM).astype(F32)).astype(BF16)


def _const_in(a):
    return (_const_spec(a.shape), a)


def _attn_proj_stage(g_pre, w1, q_norm, wq, kv_norm, wk, wvt, wvst, tabs):
    def make(b, s, tm):
        row = lambda bi, i: (bi, i, 0)
        tab = pl.BlockSpec((tm, LANES), lambda bi, i: (i, 0))
        tok = lambda w: (pl.BlockSpec((None, tm, w), row), jax.ShapeDtypeStruct((b, s, w), BF16))
        chan = lambda w: (pl.BlockSpec((None, w, tm), lambda bi, i: (bi, 0, i)),
                          jax.ShapeDtypeStruct((b, w, s), BF16))
        ins = [_const_in(g_pre.reshape(1, -1)), _const_in(w1), _const_in(q_norm.reshape(1, -1)), _const_in(wq),
               _const_in(kv_norm.reshape(1, -1)), _const_in(wk), _const_in(wvt), _const_in(wvst)]
        ins += [(tab, t) for t in tabs]
        return _Stage(_attn_proj_body, ins,
                      [tok(_HL), tok(_HL), chan(_HL), tok(_QS_W), tok(_KS_W), chan(_KS_W)])
    return make


def _mla_kernel(*refs, n_lat_chunks, tk):
    if n_lat_chunks:
        q_ref, kc_ref, vct_ref, kl_ref, vlt_ref, o_ref = refs
    else:
        q_ref, kc_ref, vct_ref, o_ref = refs
    q = q_ref[...]

    def scores(j):
        k = kc_ref[...] if j == 0 else kl_ref[(j - 1) * tk:j * tk, :]
        return _dot_nt(k, q)

    def update(j, st, m, acc):
        vt = vct_ref[...] if j == 0 else vlt_ref[:, (j - 1) * tk:j * tk]
        m_blk = jnp.max(st, axis=0, keepdims=True)
        m_new = m_blk if m is None else jnp.maximum(m, m_blk)
        pt = jnp.exp2(st - m_new).astype(BF16)
        pv = jnp.dot(vt, pt, preferred_element_type=F32)
        acc = pv if acc is None else acc * jnp.exp2(m - m_new) + pv
        return m_new, acc

    m, acc = None, None
    st_next = scores(0)
    for j in range(n_lat_chunks + 1):
        st = st_next
        if j < n_lat_chunks:
            st_next = scores(j + 1)
        m, acc = update(j, st, m, acc)
    o_ref[...] = (acc / acc[MLA_V:MLA_V + 1, :]).T.astype(o_ref.dtype)


def _mla(q, kc, vct, kl, vlt, tq, tk):
    b, s, _ = q.shape
    n_ctx = kc.shape[1]
    tq = min(tq, s)
    qspec = pl.BlockSpec((None, tq, LANES), lambda bi, h, i: (bi, i, h))
    in_specs = [qspec,
                pl.BlockSpec((None, n_ctx, LANES), lambda bi, h, i: (bi, 0, h)),
                pl.BlockSpec((None, LANES, n_ctx), lambda bi, h, i: (bi, h, 0))]
    args = [q, kc, vct]
    n_lat_chunks = 0
    if kl is not None:
        n_lat = kl.shape[1]
        tk = min(tk, n_lat)
        n_lat_chunks = n_lat // tk
        in_specs += [pl.BlockSpec((None, n_lat, LANES), lambda bi, h, i: (bi, 0, h)),
                     pl.BlockSpec((None, LANES, n_lat), lambda bi, h, i: (bi, h, 0))]
        args += [kl, vlt]
    return pl.pallas_call(
        functools.partial(_mla_kernel, n_lat_chunks=n_lat_chunks, tk=tk),
        name="mla_lat" if n_lat_chunks else "mla_ctx",
        grid=(b, MLA_HEADS, s // tq),
        in_specs=in_specs,
        out_specs=qspec,
        out_shape=jax.ShapeDtypeStruct((b, s, _HL), BF16),
        compiler_params=_params(3),
    )(*args)


def _swa_kernel(*refs, windowed, n_blocks):
    if windowed:
        (sink_ref, q_ref, kc_ref, vct_ref, kp_ref, k0_ref, kn_ref,
         vpt_ref, v0t_ref, vnt_ref, o_ref) = refs
    else:
        sink_ref, q_ref, kc_ref, vct_ref, o_ref = refs
    i = pl.program_id(1)
    tq = q_ref.shape[0]
    cols = SWA_GROUP * tq
    lane = lax.broadcasted_iota(jnp.int32, (1, LANES), 1)
    half_mask = [(lane < SWA_HEAD_DIM), (lane >= SWA_HEAD_DIM)]
    if windowed:
        j = lax.broadcasted_iota(jnp.int32, (3 * tq, cols), 0)
        r = jnp.bitwise_and(lax.broadcasted_iota(jnp.int32, (3 * tq, cols), 1), tq - 1)
        j_lo = jnp.where(i > 0, r, tq)
        j_hi = jnp.where(i < n_blocks - 1, r + 2 * tq, 2 * tq - 1)
    staged = []
    for kv in range(SWA_KV_HEADS):
        lo, hi = kv * LANES, (kv + 1) * LANES
        qs, sk = [], []
        for g in range(SWA_GROUP):
            hd = kv * SWA_GROUP + g
            t = hd // 2
            qt = q_ref[:, t * LANES:(t + 1) * LANES]
            qs.append(jnp.where(half_mask[hd % 2], qt, jnp.zeros_like(qt)))
            sk.append(jnp.full((1, tq), sink_ref[hd] * math.log2(math.e), F32))
        q4 = jnp.concatenate(qs, axis=0)
        scores = [_dot_nt(kc_ref[:, lo:hi], q4)]
        if windowed:
            kw = jnp.concatenate([kp_ref[:, lo:hi], k0_ref[:, lo:hi], kn_ref[:, lo:hi]], axis=0)
            scores.append(_dot_nt(kw, q4))
        staged.append((scores, jnp.concatenate(sk, axis=1)))
    for kv in range(SWA_KV_HEADS):
        lo, hi = kv * LANES, (kv + 1) * LANES
        scores, sink = staged[kv]
        vals = [vct_ref[lo:hi, :]]
        if windowed:
            scores[1] = jnp.where(j >= j_lo, jnp.where(j <= j_hi, scores[1], NEG_INF), NEG_INF)
            vals.append(jnp.concatenate([vpt_ref[lo:hi, :], v0t_ref[lo:hi, :], vnt_ref[lo:hi, :]], axis=1))
        m = sink
        for s_ in scores:
            m = jnp.maximum(m, jnp.max(s_, axis=0, keepdims=True))
        acc = None
        for s_, vt in zip(scores, vals):
            pv = jnp.dot(vt, jnp.exp2(s_ - m).astype(BF16), preferred_element_type=F32)
            acc = pv if acc is None else acc + pv
        denom = acc[SWA_HEAD_DIM:SWA_HEAD_DIM + 1, :] + jnp.exp2(sink - m)
        o = acc / denom
        for g in range(SWA_GROUP):
            hd = kv * SWA_GROUP + g
            o_ref[:, hd * LANES:(hd + 1) * LANES] = o[:, g * tq:(g + 1) * tq].T.astype(o_ref.dtype)


def _swa(sink, q, kc, vct, kl, vlt):
    b, s, _ = q.shape
    n_ctx = kc.shape[1]
    windowed = kl is not None
    tq = Q_BLOCK if windowed else s
    nb = s // tq
    in_specs = [pl.BlockSpec(memory_space=pltpu.SMEM),
                pl.BlockSpec((None, tq, _QS_W), lambda bi, i: (bi, i, 0)),
                pl.BlockSpec((None, n_ctx, _KS_W), lambda bi, i: (bi, 0, 0)),
                pl.BlockSpec((None, _KS_W, n_ctx), lambda bi, i: (bi, 0, 0))]
    args = [sink, q, kc, vct]
    if windowed:
        blocks = (lambda i: jnp.maximum(i - 1, 0), lambda i: i, lambda i: jnp.minimum(i + 1, nb - 1))
        in_specs += [pl.BlockSpec((None, tq, _KS_W), lambda bi, i, f=f: (bi, f(i), 0)) for f in blocks]
        in_specs += [pl.BlockSpec((None, _KS_W, tq), lambda bi, i, f=f: (bi, 0, f(i))) for f in blocks]
        args += [kl, kl, kl, vlt, vlt, vlt]
    return pl.pallas_call(
        functools.partial(_swa_kernel, windowed=windowed, n_blocks=nb),
        name="swa_lat" if windowed else "swa_ctx",
        grid=(b, nb),
        in_specs=in_specs,
        out_specs=pl.BlockSpec((None, tq, SWA_HEADS * LANES), lambda bi, i: (bi, i, 0)),
        out_shape=jax.ShapeDtypeStruct((b, s, SWA_HEADS * LANES), BF16),
        compiler_params=_params(2),
    )(*args)


def _attn_out_body(x, mod_ref, oa_ref, ob_ref, gpost_ref, wa_ref, wb_ref):
    y = jnp.dot(oa_ref[...], wa_ref[...], preferred_element_type=F32)
    y = y + jnp.dot(ob_ref[...], wb_ref[...], preferred_element_type=F32)
    return x + mod_ref[5:6, :] * _rms(y, gpost_ref[...])


def _attn_out_stage(oa, ob, g_post, wa, wb):
    def make(b, s, tm):
        row = lambda bi, i: (bi, i, 0)
        return _Stage(_attn_out_body,
                      [(pl.BlockSpec((None, tm, oa.shape[2]), row), oa),
                       (pl.BlockSpec((None, tm, ob.shape[2]), row), ob),
                       _const_in(g_post.reshape(1, -1)), _const_in(wa), _const_in(wb)])
    return make


def _s5_in_body(x, mod_ref, gpre_ref, wt_ref, o_ref):
    a = (_rms(x, gpre_ref[...]) * (1.0 + mod_ref[4:5, :]) + mod_ref[3:4, :]).astype(BF16)
    o_ref[...] = _dot_nt(wt_ref[...], a)


def _s5_in_stage(g_pre, w_in_t):
    def make(b, s, tm):
        w = w_in_t.shape[0]
        return _Stage(_s5_in_body, [_const_in(g_pre.reshape(1, -1)), _const_in(w_in_t)],
                      [(pl.BlockSpec((None, w, tm), lambda bi, i: (bi, 0, i)),
                        jax.ShapeDtypeStruct((b, w, s), F32))])
    return make


def _s5_core_kernel(u_ref, uc_ref, kk_ref, et_ref, ft_ref, ap_ref, dsk_ref, o_ref,
                    er_ref, ei_ref, xrf_ref, xrb_ref, xif_ref, xib_ref, m_ref, *, nb, nc, ncc):
    h = S5_GROUP
    rows = nb * nc

    mi = lax.broadcasted_iota(jnp.int32, (S5_CHUNK, S5_CHUNK), 0)
    li = lax.broadcasted_iota(jnp.int32, (S5_CHUNK, S5_CHUNK), 1)
    causal = li >= mi

    def build(i, j):
        lags = kk_ref[i, j:j + 1, :]
        fwd = pltpu.roll(jnp.broadcast_to(lags[:, S5_CHUNK:], (S5_CHUNK, S5_CHUNK)), 0, 1,
                         stride=1, stride_axis=0)
        bwd = pltpu.roll(jnp.broadcast_to(lags[:, :S5_CHUNK], (S5_CHUNK, S5_CHUNK)), 0, 1,
                         stride=1, stride_axis=0)
        m_ref[i * S5_CHUNK:(i + 1) * S5_CHUNK, j * S5_CHUNK:(j + 1) * S5_CHUNK] = (
            jnp.where(causal, fwd, bwd).astype(BF16))

    u32 = jnp.concatenate([u_ref[:, i].reshape(rows, S5_CHUNK) for i in range(h)], axis=1)
    ub = u32.astype(BF16)
    et = et_ref[...]
    ein = jnp.dot(ub, et, preferred_element_type=F32)
    er_ref[...] = ein[:, :LANES]
    ei_ref[...] = ein[:, LANES:]
    ar, ai = ap_ref[0:1, :], ap_ref[1:2, :]
    fwd_lane = lax.broadcasted_iota(jnp.int32, (1, LANES), 1) < S5_STATE

    def step(xr, xi, e_r, e_i):
        return xr * ar - xi * ai + e_r, xr * ai + xi * ar + e_i

    xr = jnp.zeros((nb, LANES), F32)
    xi = jnp.zeros((nb, LANES), F32)
    for c in range(ncc):
        ecf = jnp.dot(uc_ref[c].astype(BF16), et, preferred_element_type=F32)
        ecb = jnp.dot(uc_ref[ncc - 1 - c].astype(BF16), et, preferred_element_type=F32)
        xr, xi = step(xr, xi, jnp.where(fwd_lane, ecf[:, :LANES], ecb[:, :LANES]),
                      jnp.where(fwd_lane, ecf[:, LANES:], ecb[:, LANES:]))

    for t in range(nc):
        cf = pl.ds(t, nb, stride=nc)
        cb = pl.ds(nc - 1 - t, nb, stride=nc)
        xrf_ref[cf, :] = xr
        xrb_ref[cb, :] = xr
        xif_ref[cf, :] = xi
        xib_ref[cb, :] = xi
        xr, xi = step(xr, xi, jnp.where(fwd_lane, er_ref[cf, :], er_ref[cb, :]),
                      jnp.where(fwd_lane, ei_ref[cf, :], ei_ref[cb, :]))
    xs = jnp.concatenate([jnp.where(fwd_lane, xrf_ref[...], xrb_ref[...]),
                          jnp.where(fwd_lane, xif_ref[...], xib_ref[...])], axis=1).astype(BF16)

    cw = 2 * S5_CHUNK
    for jp in range(h // 2):
        for i in range(h):
            build(i, 2 * jp)
            build(i, 2 * jp + 1)
        cs = slice(jp * cw, (jp + 1) * cw)
        y = jnp.dot(ub, m_ref[:, cs], preferred_element_type=F32)
        y = y + jnp.dot(xs, ft_ref[:, cs], preferred_element_type=F32)
        y = y + u32[:, cs] * dsk_ref[:, cs]
        o_ref[:, 2 * jp] = y[:, :S5_CHUNK].reshape(nb, nc, S5_CHUNK)
        o_ref[:, 2 * jp + 1] = y[:, S5_CHUNK:].reshape(nb, nc, S5_CHUNK)


def _s5_core(u_t, uc, kk, et, ft, ap, dsk):
    b, w, s = u_t.shape
    g = w // S5_GROUP
    nc = s // S5_CHUNK
    ncc = uc.shape[1]
    hl = S5_GROUP * S5_CHUNK
    u5 = u_t.reshape(b, g, S5_GROUP, nc, S5_CHUNK)
    blk = pl.BlockSpec((b, None, S5_GROUP, nc, S5_CHUNK), lambda gi: (0, gi, 0, 0, 0))
    per_g = lambda shape: pl.BlockSpec((None,) + shape, lambda gi: (gi,) + (0,) * len(shape))
    out = pl.pallas_call(
        functools.partial(_s5_core_kernel, nb=b, nc=nc, ncc=ncc),
        name="s5_core",
        grid=(g,),
        in_specs=[blk, per_g((ncc, b, hl)),
                  per_g((S5_GROUP, S5_GROUP, 2 * S5_CHUNK)),
                  per_g((hl, 2 * LANES)), per_g((2 * LANES, hl)),
                  per_g((2, LANES)), per_g((1, hl))],
        out_specs=blk,
        out_shape=jax.ShapeDtypeStruct(u5.shape, F32),
        scratch_shapes=[pltpu.VMEM((b * nc, LANES), F32) for _ in range(6)]
        + [pltpu.VMEM((hl, hl), BF16)],
        compiler_params=_params(1),
    )(u5, uc, kk, et, ft, ap, dsk)
    return out.reshape(b, w, s)


def _s5_out_body(x, mod_ref, y_ref, gpost_ref, wt_ref):
    d = x.shape[1]
    gy = jax.nn.gelu(y_ref[...]).astype(BF16)
    z = jnp.dot(wt_ref[...], gy, preferred_element_type=F32)
    v = (z[:d, :] * jax.nn.sigmoid(z[d:, :])).T
    return x + mod_ref[5:6, :] * _rms(v, gpost_ref[...])


def _s5_out_stage(y_t, g_post, w_glu_t):
    def make(b, s, tm):
        w = y_t.shape[1]
        return _Stage(_s5_out_body,
                      [(pl.BlockSpec((None, w, tm), lambda bi, i: (bi, 0, i)), y_t),
                       _const_in(g_post.reshape(1, -1)), _const_in(w_glu_t)])
    return make


def _rope_angles(n, d_rot):
    d_axis = d_rot // 2
    inv = ROPE_BASE ** (-jnp.arange(0, d_axis, 2, dtype=F32) / d_axis)
    n_rows = n // GRID_W
    row_ang = jnp.arange(n_rows, dtype=F32)[:, None] * inv
    col_ang = jnp.arange(GRID_W, dtype=F32)[:, None] * inv

    def grid(fr, fc):
        r = jnp.broadcast_to(fr[:, None, :], (n_rows, GRID_W, fr.shape[1]))
        c = jnp.broadcast_to(fc[None, :, :], (n_rows, GRID_W, fc.shape[1]))
        return jnp.concatenate([r, c], axis=-1).reshape(n, -1)

    return grid(jnp.cos(row_ang), jnp.cos(col_ang)), grid(jnp.sin(row_ang), jnp.sin(col_ang))


def _rope_tables(n, rotate):
    pad = LANES - MLA_NOPE - MLA_ROPE
    if rotate:
        ca, sa = _rope_angles(n, MLA_ROPE)
        cb, sb = _rope_angles(n, SWA_HEAD_DIM)
    else:
        ca, sa = jnp.ones((n, MLA_ROPE // 2), F32), jnp.zeros((n, MLA_ROPE // 2), F32)
        cb, sb = jnp.ones((n, SWA_HEAD_DIM // 2), F32), jnp.zeros((n, SWA_HEAD_DIM // 2), F32)
    one, zero = jnp.ones((n, MLA_NOPE), F32), jnp.zeros((n, MLA_NOPE), F32)
    zpad = jnp.zeros((n, pad), F32)
    return (jnp.concatenate([one, ca, ca, zpad], axis=1), jnp.concatenate([zero, sa, sa, zpad], axis=1),
            jnp.concatenate([cb] * 4, axis=1), jnp.concatenate([sb] * 4, axis=1))


def _rot_partner(w, half):
    return jnp.concatenate([-w[..., half:], w[..., :half]], axis=-1)


def _attn_weights(w_in, w_uq, w_ukv, w_out):
    d = w_in.shape[0]
    sizes = [MLA_Q_LORA, MLA_KV_LORA, MLA_ROPE, SWA_HEADS * SWA_HEAD_DIM,
             SWA_KV_HEADS * SWA_HEAD_DIM, SWA_KV_HEADS * SWA_HEAD_DIM]
    cq, ckv, kpe, qs, ks, vs = jnp.split(w_in, [int(v) for v in np.cumsum(sizes)[:-1]], axis=1)
    pad_a = LANES - MLA_NOPE - MLA_ROPE
    z = lambda *shape: jnp.zeros(shape, F32)
    kpe_blk = jnp.concatenate([z(d, MLA_NOPE), kpe, z(d, pad_a)], axis=1)
    kpe_sw = jnp.concatenate([z(d, MLA_NOPE), _rot_partner(kpe, MLA_ROPE // 2), z(d, pad_a)], axis=1)
    qs3 = qs.reshape(d, SWA_HEADS, SWA_HEAD_DIM)
    qs_sw = _rot_partner(qs3, SWA_HEAD_DIM // 2).reshape(d, -1)
    ks3 = ks.reshape(d, SWA_KV_HEADS, SWA_HEAD_DIM)
    ks_sw3 = _rot_partner(ks3, SWA_HEAD_DIM // 2)
    dup = lambda t: jnp.concatenate([t, t], axis=-1).reshape(d, -1)
    vs3 = vs.reshape(d, SWA_KV_HEADS, SWA_HEAD_DIM)
    vs_pad = jnp.concatenate([vs3, jnp.zeros_like(vs3)], axis=-1).reshape(d, -1)
    w1 = jnp.concatenate([cq, ckv, kpe_blk, kpe_sw, qs, qs_sw, dup(ks3), dup(ks_sw3)], axis=1)
    wvst = vs_pad.T
    assert w1.shape[1] == _W1_COLS

    ql = w_uq.shape[0]
    uq = w_uq.reshape(ql, MLA_HEADS, MLA_NOPE + MLA_ROPE)
    nope, pe = uq[..., :MLA_NOPE], uq[..., MLA_NOPE:]
    zq = jnp.zeros((ql, MLA_HEADS, pad_a), F32)
    wq_blk = jnp.concatenate([nope, pe, zq], axis=-1).reshape(ql, -1)
    wq_sw = jnp.concatenate([jnp.zeros_like(nope), _rot_partner(pe, MLA_ROPE // 2), zq], axis=-1).reshape(ql, -1)
    wq = jnp.concatenate([wq_blk, wq_sw], axis=1)

    kl = w_ukv.shape[0]
    ukv = w_ukv.reshape(kl, MLA_HEADS, MLA_NOPE + MLA_V)
    kn, vv = ukv[..., :MLA_NOPE], ukv[..., MLA_NOPE:]
    wk = jnp.concatenate([kn, jnp.zeros((kl, MLA_HEADS, LANES - MLA_NOPE), F32)], axis=-1).reshape(kl, -1)
    wv = jnp.concatenate([vv, jnp.zeros((kl, MLA_HEADS, LANES - MLA_V), F32)], axis=-1).reshape(kl, -1)
    wvt = wv.T

    dm = w_out.shape[1]
    na = MLA_HEADS * MLA_V
    oa = w_out[:na].reshape(MLA_HEADS, MLA_V, dm)
    ob = w_out[na:].reshape(SWA_HEADS, SWA_HEAD_DIM, dm)
    wa = jnp.concatenate([oa, jnp.zeros((MLA_HEADS, LANES - MLA_V, dm), F32)], axis=1).reshape(-1, dm)
    wb = jnp.concatenate([ob, jnp.zeros((SWA_HEADS, LANES - SWA_HEAD_DIM, dm), F32)], axis=1).reshape(-1, dm)
    return tuple(t.astype(BF16) for t in (w1, wq, wk, wvt, wvst, wa, wb))


def _complex_powers(a_re, a_im, n):
    pr, pi = jnp.ones_like(a_re)[None], jnp.zeros_like(a_im)[None]
    sr, si = a_re, a_im
    while pr.shape[0] < n + 1:
        nr, ni = pr * sr - pi * si, pr * si + pi * sr
        pr, pi = jnp.concatenate([pr, nr], 0), jnp.concatenate([pi, ni], 0)
        sr, si = sr * sr - si * si, 2.0 * sr * si
    return pr[:n + 1], pi[:n + 1]


def _s5_operators(lam_re, lam_im, b_re, b_im, c_re, c_im, log_step):
    hi = lax.Precision.HIGHEST
    ln = S5_CHUNK
    g, p = lam_re.shape[1:]
    hh = b_re.shape[-1]
    ks, ets, fts, aps = [], [], [], []
    for dr in range(2):
        lre = jnp.minimum(lam_re[dr], S5_MAX_RE)
        lim = lam_im[dr]
        dt = jnp.exp(log_step[dr])[:, None]
        mag = jnp.exp(lre * dt)
        a_re, a_im = mag * jnp.cos(lim * dt), mag * jnp.sin(lim * dt)
        den = lre * lre + lim * lim
        f_re = ((a_re - 1.0) * lre + a_im * lim) / den
        f_im = (a_im * lre - (a_re - 1.0) * lim) / den
        bb_re = f_re[..., None] * b_re[dr] - f_im[..., None] * b_im[dr]
        bb_im = f_re[..., None] * b_im[dr] + f_im[..., None] * b_re[dr]
        cr, ci = c_re[dr], c_im[dr]
        pr, pi = _complex_powers(a_re, a_im, ln)
        cb_re = jnp.einsum('gjp,gpi->gpji', cr, bb_re) - jnp.einsum('gjp,gpi->gpji', ci, bb_im)
        cb_im = jnp.einsum('gjp,gpi->gpji', cr, bb_im) + jnp.einsum('gjp,gpi->gpji', ci, bb_re)
        k = (jnp.einsum('dgp,gpji->gijd', pr[:ln], cb_re, precision=hi)
             - jnp.einsum('dgp,gpji->gijd', pi[:ln], cb_im, precision=hi))
        ks.append(k)
        er, ei = (pr[:ln][::-1], pi[:ln][::-1]) if dr == 0 else (pr[:ln], pi[:ln])
        e_re = jnp.einsum('mgp,gpi->gimp', er, bb_re) - jnp.einsum('mgp,gpi->gimp', ei, bb_im)
        e_im = jnp.einsum('mgp,gpi->gimp', er, bb_im) + jnp.einsum('mgp,gpi->gimp', ei, bb_re)
        ets.append((e_re.reshape(g, hh * ln, p), e_im.reshape(g, hh * ln, p)))
        fr, fi = (pr[1:], pi[1:]) if dr == 0 else (pr[1:][::-1], pi[1:][::-1])
        f_xre = jnp.einsum('gjp,lgp->gpjl', cr, fr) - jnp.einsum('gjp,lgp->gpjl', ci, fi)
        f_xim = -(jnp.einsum('gjp,lgp->gpjl', cr, fi) + jnp.einsum('gjp,lgp->gpjl', ci, fr))
        fts.append((f_xre.reshape(g, p, hh * ln), f_xim.reshape(g, p, hh * ln)))
        aps.append((pr[ln], pi[ln]))
    kf, kb = ks
    k0 = kf[..., :1] + kb[..., :1]
    kk = jnp.concatenate([jnp.zeros_like(k0), kb[..., :0:-1], k0, kf[..., 1:]], axis=-1)
    et = jnp.concatenate([ets[0][0], ets[1][0], ets[0][1], ets[1][1]], axis=-1).astype(BF16)
    ft = jnp.concatenate([fts[0][0], fts[1][0], fts[0][1], fts[1][1]], axis=1).astype(BF16)
    ap = jnp.stack([jnp.concatenate([aps[0][0], aps[1][0]], axis=-1),
                    jnp.concatenate([aps[0][1], aps[1][1]], axis=-1)], axis=1)
    return kk, et, ft, ap


def kernel(x, c, ctx, c_ctx, mod_w, mod_b, norm_pre, norm_post, ffn_w13, ffn_w2,
           attn_w_in, mla_q_norm, mla_w_uq, mla_kv_norm, mla_w_ukv, swa_sink, attn_w_out,
           s5_w_in, s5_lambda_re, s5_lambda_im, s5_b_re, s5_b_im, s5_c_re, s5_c_im,
           s5_log_step, s5_d, s5_w_glu):
    b, n, d = x.shape
    n_ctx = ctx.shape[1]
    depth = mod_w.shape[0]
    tm = 512

    rows = -(-(b + 1) // 8) * 8
    cs = jnp.concatenate([c, c_ctx[None], jnp.zeros((rows - b - 1, d), F32)], axis=0)
    mods = _modulation(cs, mod_w, mod_b)
    w13 = ffn_w13.astype(BF16)
    w2 = ffn_w2.astype(BF16)

    h_lat, h_ctx = x, ctx
    for l in range(depth):
        last = l == depth - 1
        m_lat = mods[l, :b].reshape(b, N_MOD, d)
        m_ctx = jnp.broadcast_to(mods[l, b].reshape(1, N_MOD, d), (b, N_MOD, d))
        ffn1 = functools.partial(_ffn, j=0, g_pre=norm_pre[l, 0], g_post=norm_post[l, 0],
                                 w13=w13[l, 0], w2=w2[l, 0], tm=tm)
        ffn2 = functools.partial(_ffn, j=2, g_pre=norm_pre[l, 2], g_post=norm_post[l, 2],
                                 w13=w13[l, 1], w2=w2[l, 1], tm=tm)
        if l % 2 == 0:
            e = l // 2
            w1, wq, wk, wvt, wvst, wa, wb = _attn_weights(attn_w_in[e], mla_w_uq[e], mla_w_ukv[e], attn_w_out[e])
            proj = functools.partial(_attn_proj_stage, norm_pre[l, 1], w1, mla_q_norm[e], wq,
                                     mla_kv_norm[e], wk, wvt, wvst)
            h_ctx, qa_c, ka_c, va_c, qs_c, ks_c, vs_c = ffn1(h_ctx, m_ctx, post=proj(_rope_tables(n_ctx, False)))
            h_lat, qa_l, ka_l, va_l, qs_l, ks_l, vs_l = ffn1(h_lat, m_lat, post=proj(_rope_tables(n, True)))
            o_a = _mla(qa_l, ka_c, va_c, ka_l, va_l, tq=1024, tk=512)
            o_b = _swa(swa_sink[e], qs_l, ks_c, vs_c, ks_l, vs_l)
            h_lat = ffn2(h_lat, m_lat, pre=_attn_out_stage(o_a, o_b, norm_post[l, 1], wa, wb))
            if not last:
                o_a_c = _mla(qa_c, ka_c, va_c, None, None, tq=1024, tk=512)
                o_b_c = _swa(swa_sink[e], qs_c, ks_c, vs_c, None, None)
                h_ctx = ffn2(h_ctx, m_ctx, pre=_attn_out_stage(o_a_c, o_b_c, norm_post[l, 1], wa, wb))
        else:
            o = l // 2
            assert last, "S5 context outputs are only needed when another layer follows"
            w_in_t = s5_w_in[o].T.astype(BF16)
            w_glu_t = s5_w_glu[o].T.astype(BF16)
            kk, et, ft, ap = _s5_operators(s5_lambda_re[o], s5_lambda_im[o], s5_b_re[o], s5_b_im[o],
                                             s5_c_re[o], s5_c_im[o], s5_log_step[o])
            width = s5_w_in.shape[2]
            g = width // S5_GROUP
            s5_in = _s5_in_stage(norm_pre[l, 1], w_in_t)
            h_lat, u_lat = ffn1(h_lat, m_lat, post=s5_in)
            h_ctx, u_ctx = ffn1(h_ctx, m_ctx, post=s5_in)
            ncc = n_ctx // S5_CHUNK
            uc = u_ctx.reshape(b, g, S5_GROUP, ncc, S5_CHUNK).transpose(1, 3, 0, 2, 4)
            uc = uc.reshape(g, ncc, b, S5_GROUP * S5_CHUNK)
            dsk = jnp.repeat(s5_d[o].astype(F32), S5_CHUNK).reshape(g, 1, S5_GROUP * S5_CHUNK)
            y_t = _s5_core(u_lat, uc, kk, et, ft, ap, dsk)
            h_lat = ffn2(h_lat, m_lat, pre=_s5_out_stage(y_t, norm_post[l, 1], w_glu_t))
    return h_lat
```

```python
import functools
import math

import numpy as np
import jax
import jax.numpy as jnp
from jax import lax
from jax.experimental import pallas as pl
from jax.experimental.pallas import tpu as pltpu

F32 = jnp.float32
BF16 = jnp.bfloat16

LANES = 128
VMEM_LIMIT = 56 * 1024 * 1024

N_MOD = 9
FFN_RES = 0.5
EPS = 1e-6
ROPE_BASE = 10000.0
GRID_W = 64
NEG_INF = -1e30

MLA_HEADS = 8
MLA_Q_LORA = 256
MLA_KV_LORA = 128
MLA_NOPE = 64
MLA_ROPE = 32
MLA_V = 64
MLA_SCALE = (MLA_NOPE + MLA_ROPE) ** -0.5

SWA_HEADS = 8
SWA_KV_HEADS = 2
SWA_GROUP = SWA_HEADS // SWA_KV_HEADS
SWA_HEAD_DIM = 64
SWA_WINDOW = 128
SWA_SCALE = SWA_HEAD_DIM ** -0.5
Q_BLOCK = 128

S5_GROUP = 16
S5_STATE = 64
S5_CHUNK = 128
S5_MAX_RE = -1e-4


def _params(n_grid):
    return pltpu.CompilerParams(dimension_semantics=("arbitrary",) * n_grid,
                                vmem_limit_bytes=VMEM_LIMIT)


def _rms(x, g):
    return x * lax.rsqrt(jnp.mean(x * x, axis=-1, keepdims=True) + EPS) * g


def _const_spec(shape):
    nd = len(shape)
    return pl.BlockSpec(shape, lambda *_: (0,) * nd, pipeline_mode=pl.Buffered(1))


def _dot_nt(a, b):
    return lax.dot_general(a, b, (((1,), (1,)), ((), ())), preferred_element_type=F32)


def _mod_kernel(c_ref, w_ref, b_ref, o_ref):
    a = jax.nn.silu(c_ref[...]).astype(BF16)
    o_ref[...] = jnp.dot(a, w_ref[...].astype(BF16), preferred_element_type=F32) + b_ref[...]


def _modulation(cs, mod_w, mod_b, tn=1024):
    depth, d, n = mod_w.shape
    r = cs.shape[0]
    return pl.pallas_call(
        _mod_kernel,
        name="modulation",
        grid=(depth, n // tn),
        in_specs=[pl.BlockSpec((r, d), lambda l, j: (0, 0)),
                  pl.BlockSpec((None, d, tn), lambda l, j: (l, 0, j)),
                  pl.BlockSpec((None, 1, tn), lambda l, j: (l, 0, j))],
        out_specs=pl.BlockSpec((None, r, tn), lambda l, j: (l, 0, j)),
        out_shape=jax.ShapeDtypeStruct((depth, r, n), F32),
        compiler_params=_params(2),
    )(cs, mod_w, mod_b.reshape(depth, 1, n))


class _Stage:
    def __init__(self, body, ins, outs=()):
        self.body, self.ins, self.outs = body, list(ins), list(outs)


def _ffn_kernel(*refs, j, fc, pre, post):
    h_ref, mod_ref, gpre_ref, gpost_ref, w13_ref, w2_ref = refs[:6]
    n_pre = len(pre.ins) if pre else 0
    n_post = len(post.ins) if post else 0
    pre_refs = refs[6:6 + n_pre]
    post_refs = refs[6 + n_pre:6 + n_pre + n_post]
    o_ref = refs[6 + n_pre + n_post]
    post_outs = refs[7 + n_pre + n_post:-1]
    acc_ref = refs[-1]
    x = h_ref[...]
    if pre:
        x = pre.body(x, mod_ref, *pre_refs)
    shift = mod_ref[3 * j:3 * j + 1, :]
    scale = mod_ref[3 * j + 1:3 * j + 2, :]
    gate = mod_ref[3 * j + 2:3 * j + 3, :]
    a = (_rms(x, gpre_ref[...]) * (1.0 + scale) + shift).astype(BF16)
    f = w2_ref.shape[0]
    def up(c):
        return (jnp.dot(a, w13_ref[:, c * fc:(c + 1) * fc], preferred_element_type=F32),
                jnp.dot(a, w13_ref[:, f + c * fc:f + (c + 1) * fc], preferred_element_type=F32))

    n_chunks = f // fc
    nxt = up(0)
    for c in range(n_chunks):
        g, u = nxt
        if c + 1 < n_chunks:
            nxt = up(c + 1)
        act = (jax.nn.silu(g) * u).astype(BF16)
        contrib = jnp.dot(act, w2_ref[c * fc:(c + 1) * fc, :], preferred_element_type=F32)
        if c == 0:
            acc_ref[...] = contrib
        else:
            acc_ref[...] += contrib
    out = x + FFN_RES * gate * _rms(acc_ref[...], gpost_ref[...])
    o_ref[...] = out
    if post:
        post.body(out, mod_ref, *post_refs, *post_outs)


def _ffn(h, mod, j, g_pre, g_post, w13, w2, tm, pre=None, post=None):
    b, s, d = h.shape
    f = w2.shape[0]
    fc = 256 if f % 256 == 0 else f
    tm = min(tm, s)
    pre = pre(b, s, tm) if pre else None
    post = post(b, s, tm) if post else None
    extra = (pre.ins if pre else []) + (post.ins if post else [])
    outs = [(pl.BlockSpec((None, tm, d), lambda bi, i: (bi, i, 0)), jax.ShapeDtypeStruct((b, s, d), F32))]
    outs += post.outs if post else []
    res = pl.pallas_call(
        functools.partial(_ffn_kernel, j=j, fc=fc, pre=pre, post=post),
        name="ffn",
        grid=(b, s // tm),
        in_specs=[pl.BlockSpec((None, tm, d), lambda bi, i: (bi, i, 0)),
                  pl.BlockSpec((None, N_MOD, d), lambda bi, i: (bi, 0, 0)),
                  _const_spec((1, d)), _const_spec((1, d)),
                  _const_spec(w13.shape), _const_spec(w2.shape)] + [e[0] for e in extra],
        out_specs=[o[0] for o in outs],
        out_shape=[o[1] for o in outs],
        scratch_shapes=[pltpu.VMEM((tm, d), F32)],
        compiler_params=_params(2),
    )(h, mod, g_pre.reshape(1, d), g_post.reshape(1, d), w13, w2, *[e[1] for e in extra])
    return res if post else res[0]


_O_CQ = 0
_O_CKV = _O_CQ + MLA_Q_LORA
_O_KPE = _O_CKV + MLA_KV_LORA
_O_KPE_SW = _O_KPE + LANES
_O_QS = _O_KPE_SW + LANES
_O_QS_SW = _O_QS + SWA_HEADS * SWA_HEAD_DIM
_O_KS = _O_QS_SW + SWA_HEADS * SWA_HEAD_DIM
_O_KS_SW = _O_KS + SWA_KV_HEADS * LANES
_W1_COLS = _O_KS_SW + SWA_KV_HEADS * LANES
_HL = MLA_HEADS * LANES
_QS_W = SWA_HEADS * SWA_HEAD_DIM
_KS_W = SWA_KV_HEADS * LANES


def _attn_proj_body(x, mod_ref, gpre_ref, w1_ref, qn_ref, wq_ref, kvn_ref, wk_ref, wvt_ref, wvst_ref,
                    ca_ref, sa_ref, cb_ref, sb_ref,
                    qa_ref, ka_ref, vat_ref, qs_ref, ks_ref, vst_ref):
    shift = mod_ref[3:4, :]
    scale = mod_ref[4:5, :]
    a = (_rms(x, gpre_ref[...]) * (1.0 + scale) + shift).astype(BF16)
    p = jnp.dot(a, w1_ref[...], preferred_element_type=F32)
    ca, sa, cb, sb = ca_ref[...], sa_ref[...], cb_ref[...], sb_ref[...]

    cqn = _rms(p[:, _O_CQ:_O_CQ + MLA_Q_LORA], qn_ref[...]).astype(BF16)
    q2 = jnp.dot(cqn, wq_ref[...], preferred_element_type=F32)
    ckvn = _rms(p[:, _O_CKV:_O_CKV + MLA_KV_LORA], kvn_ref[...]).astype(BF16)
    kn = jnp.dot(ckvn, wk_ref[...], preferred_element_type=F32)
    kpe = p[:, _O_KPE:_O_KPE + LANES] * ca + p[:, _O_KPE_SW:_O_KPE_SW + LANES] * sa
    for hd in range(MLA_HEADS):
        lo, hi = hd * LANES, (hd + 1) * LANES
        q = q2[:, lo:hi] * ca + q2[:, _HL + lo:_HL + hi] * sa
        qa_ref[:, lo:hi] = (q * (MLA_SCALE * math.log2(math.e))).astype(BF16)
        ka_ref[:, lo:hi] = (kn[:, lo:hi] + kpe).astype(BF16)
    vt = _dot_nt(wvt_ref[...], ckvn)
    row = lax.broadcasted_iota(jnp.int32, (_HL, 1), 0)
    vat_ref[...] = (vt + (jnp.bitwise_and(row, LANES - 1) == MLA_V).astype(F32)).astype(BF16)

    for t in range(_QS_W // LANES):
        lo, hi = t * LANES, (t + 1) * LANES
        q = p[:, _O_QS + lo:_O_QS + hi] * cb + p[:, _O_QS_SW + lo:_O_QS_SW + hi] * sb
        qs_ref[:, lo:hi] = (q * (SWA_SCALE * math.log2(math.e))).astype(BF16)
    for t in range(SWA_KV_HEADS):
        lo, hi = t * LANES, (t + 1) * LANES
        k = p[:, _O_KS + lo:_O_KS + hi] * cb + p[:, _O_KS_SW + lo:_O_KS_SW + hi] * sb
        ks_ref[:, lo:hi] = k.astype(BF16)
    vst = _dot_nt(wvst_ref[...], a)
    row = lax.broadcasted_iota(jnp.int32, (_KS_W, 1), 0)
    vst_ref[...] = (vst + (jnp.bitwise_and(row, LANES - 1) == SWA_HEAD_DIM).astype(F32)).astype(BF16)


def _const_in(a):
    return (_const_spec(a.shape), a)


def _attn_proj_stage(g_pre, w1, q_norm, wq, kv_norm, wk, wvt, wvst, tabs):
    def make(b, s, tm):
        row = lambda bi, i: (bi, i, 0)
        tab = pl.BlockSpec((tm, LANES), lambda bi, i: (i, 0))
        tok = lambda w: (pl.BlockSpec((None, tm, w), row), jax.ShapeDtypeStruct((b, s, w), BF16))
        chan = lambda w: (pl.BlockSpec((None, w, tm), lambda bi, i: (bi, 0, i)),
                          jax.ShapeDtypeStruct((b, w, s), BF16))
        ins = [_const_in(g_pre.reshape(1, -1)), _const_in(w1), _const_in(q_norm.reshape(1, -1)), _const_in(wq),
               _const_in(kv_norm.reshape(1, -1)), _const_in(wk), _const_in(wvt), _const_in(wvst)]
        ins += [(tab, t) for t in tabs]
        return _Stage(_attn_proj_body, ins,
                      [tok(_HL), tok(_HL), chan(_HL), tok(_QS_W), tok(_KS_W), chan(_KS_W)])
    return make


def _mla_kernel(*refs, n_lat_chunks, tk):
    if n_lat_chunks:
        q_ref, kc_ref, vct_ref, kl_ref, vlt_ref, o_ref = refs
    else:
        q_ref, kc_ref, vct_ref, o_ref = refs
    q = q_ref[...]

    def scores(j):
        k = kc_ref[...] if j == 0 else kl_ref[(j - 1) * tk:j * tk, :]
        return _dot_nt(k, q)

    def update(j, st, m, acc):
        vt = vct_ref[...] if j == 0 else vlt_ref[:, (j - 1) * tk:j * tk]
        m_blk = jnp.max(st, axis=0, keepdims=True)
        m_new = m_blk if m is None else jnp.maximum(m, m_blk)
        pt = jnp.exp2(st - m_new).astype(BF16)
        pv = jnp.dot(vt, pt, preferred_element_type=F32)
        acc = pv if acc is None else acc * jnp.exp2(m - m_new) + pv
        return m_new, acc

    m, acc = None, None
    st_next = scores(0)
    for j in range(n_lat_chunks + 1):
        st = st_next
        if j < n_lat_chunks:
            st_next = scores(j + 1)
        m, acc = update(j, st, m, acc)
    o_ref[...] = (acc / acc[MLA_V:MLA_V + 1, :]).T.astype(o_ref.dtype)


def _mla(q, kc, vct, kl, vlt, tq, tk):
    b, s, _ = q.shape
    n_ctx = kc.shape[1]
    tq = min(tq, s)
    qspec = pl.BlockSpec((None, tq, LANES), lambda bi, h, i: (bi, i, h))
    in_specs = [qspec,
                pl.BlockSpec((None, n_ctx, LANES), lambda bi, h, i: (bi, 0, h)),
                pl.BlockSpec((None, LANES, n_ctx), lambda bi, h, i: (bi, h, 0))]
    args = [q, kc, vct]
    n_lat_chunks = 0
    if kl is not None:
        n_lat = kl.shape[1]
        tk = min(tk, n_lat)
        n_lat_chunks = n_lat // tk
        in_specs += [pl.BlockSpec((None, n_lat, LANES), lambda bi, h, i: (bi, 0, h)),
                     pl.BlockSpec((None, LANES, n_lat), lambda bi, h, i: (bi, h, 0))]
        args += [kl, vlt]
    return pl.pallas_call(
        functools.partial(_mla_kernel, n_lat_chunks=n_lat_chunks, tk=tk),
        name="mla_lat" if n_lat_chunks else "mla_ctx",
        grid=(b, MLA_HEADS, s // tq),
        in_specs=in_specs,
        out_specs=qspec,
        out_shape=jax.ShapeDtypeStruct((b, s, _HL), BF16),
        compiler_params=_params(3),
    )(*args)


def _swa_kernel(*refs, windowed, n_blocks):
    if windowed:
        (sink_ref, q_ref, kc_ref, vct_ref, kp_ref, k0_ref, kn_ref,
         vpt_ref, v0t_ref, vnt_ref, o_ref) = refs
    else:
        sink_ref, q_ref, kc_ref, vct_ref, o_ref = refs
    i = pl.program_id(1)
    tq = q_ref.shape[0]
    cols = SWA_GROUP * tq
    lane = lax.broadcasted_iota(jnp.int32, (1, LANES), 1)
    half_mask = [(lane < SWA_HEAD_DIM), (lane >= SWA_HEAD_DIM)]
    if windowed:
        w = SWA_WINDOW
        j = lax.broadcasted_iota(jnp.int32, (tq + 2 * w, cols), 0)
        r = jnp.bitwise_and(lax.broadcasted_iota(jnp.int32, (tq + 2 * w, cols), 1), tq - 1)
        j_lo = jnp.where(i > 0, r, jnp.maximum(r, w))
        j_hi = jnp.where(i < n_blocks - 1, r + 2 * w, jnp.minimum(r + 2 * w, w + tq - 1))
    staged = []
    for kv in range(SWA_KV_HEADS):
        lo, hi = kv * LANES, (kv + 1) * LANES
        qs, sk = [], []
        for g in range(SWA_GROUP):
            hd = kv * SWA_GROUP + g
            t = hd // 2
            qt = q_ref[:, t * LANES:(t + 1) * LANES]
            qs.append(jnp.where(half_mask[hd % 2], qt, jnp.zeros_like(qt)))
            sk.append(jnp.full((1, tq), sink_ref[hd] * math.log2(math.e), F32))
        q4 = jnp.concatenate(qs, axis=0)
        scores = [_dot_nt(kc_ref[:, lo:hi], q4)]
        if windowed:
            kw = jnp.concatenate([kp_ref[:, lo:hi], k0_ref[:, lo:hi], kn_ref[:, lo:hi]], axis=0)
            scores.append(_dot_nt(kw, q4))
        staged.append((scores, jnp.concatenate(sk, axis=1)))
    for kv in range(SWA_KV_HEADS):
        lo, hi = kv * LANES, (kv + 1) * LANES
        scores, sink = staged[kv]
        vals = [vct_ref[lo:hi, :]]
        if windowed:
            scores[1] = jnp.where(j >= j_lo, jnp.where(j <= j_hi, scores[1], NEG_INF), NEG_INF)
            vals.append(jnp.concatenate([vpt_ref[lo:hi, :], v0t_ref[lo:hi, :], vnt_ref[lo:hi, :]], axis=1))
        m = sink
        for s_ in scores:
            m = jnp.maximum(m, jnp.max(s_, axis=0, keepdims=True))
        acc = None
        for s_, vt in zip(scores, vals):
            pv = jnp.dot(vt, jnp.exp2(s_ - m).astype(BF16), preferred_element_type=F32)
            acc = pv if acc is None else acc + pv
        denom = acc[SWA_HEAD_DIM:SWA_HEAD_DIM + 1, :] + jnp.exp2(sink - m)
        o = acc / denom
        for g in range(SWA_GROUP):
            hd = kv * SWA_GROUP + g
            o_ref[:, hd * LANES:(hd + 1) * LANES] = o[:, g * tq:(g + 1) * tq].T.astype(o_ref.dtype)


def _swa(sink, q, kc, vct, kl, vlt):
    b, s, _ = q.shape
    n_ctx = kc.shape[1]
    windowed = kl is not None
    tq = (2 * Q_BLOCK if s % (2 * Q_BLOCK) == 0 else Q_BLOCK) if windowed else s
    nb = s // tq
    w = SWA_WINDOW
    per = tq // w
    in_specs = [pl.BlockSpec(memory_space=pltpu.SMEM),
                pl.BlockSpec((None, tq, _QS_W), lambda bi, i: (bi, i, 0)),
                pl.BlockSpec((None, n_ctx, _KS_W), lambda bi, i: (bi, 0, 0)),
                pl.BlockSpec((None, _KS_W, n_ctx), lambda bi, i: (bi, 0, 0))]
    args = [sink, q, kc, vct]
    if windowed:
        blocks = ((w, lambda i: jnp.maximum(i * per - 1, 0)), (tq, lambda i: i),
                  (w, lambda i: jnp.minimum((i + 1) * per, s // w - 1)))
        in_specs += [pl.BlockSpec((None, n, _KS_W), lambda bi, i, f=f: (bi, f(i), 0)) for n, f in blocks]
        in_specs += [pl.BlockSpec((None, _KS_W, n), lambda bi, i, f=f: (bi, 0, f(i))) for n, f in blocks]
        args += [kl, kl, kl, vlt, vlt, vlt]
    return pl.pallas_call(
        functools.partial(_swa_kernel, windowed=windowed, n_blocks=nb),
        name="swa_lat" if windowed else "swa_ctx",
        grid=(b, nb),
        in_specs=in_specs,
        out_specs=pl.BlockSpec((None, tq, SWA_HEADS * LANES), lambda bi, i: (bi, i, 0)),
        out_shape=jax.ShapeDtypeStruct((b, s, SWA_HEADS * LANES), BF16),
        compiler_params=_params(2),
    )(*args)


def _attn_out_body(x, mod_ref, oa_ref, ob_ref, gpost_ref, wa_ref, wb_ref):
    y = jnp.dot(oa_ref[...], wa_ref[...], preferred_element_type=F32)
    y = y + jnp.dot(ob_ref[...], wb_ref[...], preferred_element_type=F32)
    return x + mod_ref[5:6, :] * _rms(y, gpost_ref[...])


def _attn_out_stage(oa, ob, g_post, wa, wb):
    def make(b, s, tm):
        row = lambda bi, i: (bi, i, 0)
        return _Stage(_attn_out_body,
                      [(pl.BlockSpec((None, tm, oa.shape[2]), row), oa),
                       (pl.BlockSpec((None, tm, ob.shape[2]), row), ob),
                       _const_in(g_post.reshape(1, -1)), _const_in(wa), _const_in(wb)])
    return make


def _s5_in_body(x, mod_ref, gpre_ref, wt_ref, o_ref):
    a = (_rms(x, gpre_ref[...]) * (1.0 + mod_ref[4:5, :]) + mod_ref[3:4, :]).astype(BF16)
    o_ref[...] = _dot_nt(wt_ref[...], a)


def _s5_in_stage(g_pre, w_in_t):
    def make(b, s, tm):
        w = w_in_t.shape[0]
        return _Stage(_s5_in_body, [_const_in(g_pre.reshape(1, -1)), _const_in(w_in_t)],
                      [(pl.BlockSpec((None, w, tm), lambda bi, i: (bi, 0, i)),
                        jax.ShapeDtypeStruct((b, w, s), F32))])
    return make


def _s5_core_kernel(u_ref, uc_ref, kk_ref, et_ref, ft_ref, ap_ref, dsk_ref, o_ref,
                    er_ref, ei_ref, xrf_ref, xrb_ref, xif_ref, xib_ref, m_ref, *, nb, nc, ncc):
    h = S5_GROUP
    rows = nb * nc

    mi = lax.broadcasted_iota(jnp.int32, (S5_CHUNK, S5_CHUNK), 0)
    li = lax.broadcasted_iota(jnp.int32, (S5_CHUNK, S5_CHUNK), 1)
    causal = li >= mi

    def build(i, j):
        lags = jnp.broadcast_to(kk_ref[i, j:j + 1, :], (S5_CHUNK, S5_CHUNK))
        rolled = pltpu.roll(lags, 0, 1, stride=1, stride_axis=0)
        bits = jnp.where(causal, rolled & jnp.uint32(0xFFFF0000), rolled << jnp.uint32(16))
        m_ref[i * S5_CHUNK:(i + 1) * S5_CHUNK, j * S5_CHUNK:(j + 1) * S5_CHUNK] = (
            lax.bitcast_convert_type(bits, F32).astype(BF16))

    u32 = jnp.concatenate([u_ref[:, i].reshape(rows, S5_CHUNK) for i in range(h)], axis=1)
    ub = u32.astype(BF16)
    et = et_ref[...]
    ein = jnp.dot(ub, et, preferred_element_type=F32)
    er_ref[...] = ein[:, :LANES]
    ei_ref[...] = ein[:, LANES:]
    ar, ai = ap_ref[0:1, :], ap_ref[1:2, :]
    fwd_lane = lax.broadcasted_iota(jnp.int32, (1, LANES), 1) < S5_STATE

    def step(xr, xi, e_r, e_i):
        return xr * ar - xi * ai + e_r, xr * ai + xi * ar + e_i

    xr = jnp.zeros((nb, LANES), F32)
    xi = jnp.zeros((nb, LANES), F32)
    for c in range(ncc):
        ecf = jnp.dot(uc_ref[c].astype(BF16), et, preferred_element_type=F32)
        ecb = jnp.dot(uc_ref[ncc - 1 - c].astype(BF16), et, preferred_element_type=F32)
        xr, xi = step(xr, xi, jnp.where(fwd_lane, ecf[:, :LANES], ecb[:, :LANES]),
                      jnp.where(fwd_lane, ecf[:, LANES:], ecb[:, LANES:]))

    for t in range(nc):
        cf = pl.ds(t, nb, stride=nc)
        cb = pl.ds(nc - 1 - t, nb, stride=nc)
        xrf_ref[cf, :] = xr
        xrb_ref[cb, :] = xr
        xif_ref[cf, :] = xi
        xib_ref[cb, :] = xi
        xr, xi = step(xr, xi, jnp.where(fwd_lane, er_ref[cf, :], er_ref[cb, :]),
                      jnp.where(fwd_lane, ei_ref[cf, :], ei_ref[cb, :]))
    xs = jnp.concatenate([jnp.where(fwd_lane, xrf_ref[...], xrb_ref[...]),
                          jnp.where(fwd_lane, xif_ref[...], xib_ref[...])], axis=1).astype(BF16)

    cw = 2 * S5_CHUNK
    for jp in range(h // 2):
        for i in range(h):
            build(i, 2 * jp)
            build(i, 2 * jp + 1)
        cs = slice(jp * cw, (jp + 1) * cw)
        y = jnp.dot(ub, m_ref[:, cs], preferred_element_type=F32)
        y = y + jnp.dot(xs, ft_ref[:, cs], preferred_element_type=F32)
        y = y + u32[:, cs] * dsk_ref[:, cs]
        o_ref[:, 2 * jp] = y[:, :S5_CHUNK].reshape(nb, nc, S5_CHUNK)
        o_ref[:, 2 * jp + 1] = y[:, S5_CHUNK:].reshape(nb, nc, S5_CHUNK)


def _s5_core(u_t, uc, kk, et, ft, ap, dsk):
    b, w, s = u_t.shape
    g = w // S5_GROUP
    nc = s // S5_CHUNK
    ncc = uc.shape[1]
    hl = S5_GROUP * S5_CHUNK
    u5 = u_t.reshape(b, g, S5_GROUP, nc, S5_CHUNK)
    blk = pl.BlockSpec((b, None, S5_GROUP, nc, S5_CHUNK), lambda gi: (0, gi, 0, 0, 0))
    per_g = lambda shape: pl.BlockSpec((None,) + shape, lambda gi: (gi,) + (0,) * len(shape))
    out = pl.pallas_call(
        functools.partial(_s5_core_kernel, nb=b, nc=nc, ncc=ncc),
        name="s5_core",
        grid=(g,),
        in_specs=[blk, per_g((ncc, b, hl)),
                  per_g((S5_GROUP, S5_GROUP, S5_CHUNK)),
                  per_g((hl, 2 * LANES)), per_g((2 * LANES, hl)),
                  per_g((2, LANES)), per_g((1, hl))],
        out_specs=blk,
        out_shape=jax.ShapeDtypeStruct(u5.shape, F32),
        scratch_shapes=[pltpu.VMEM((b * nc, LANES), F32) for _ in range(6)]
        + [pltpu.VMEM((hl, hl), BF16)],
        compiler_params=_params(1),
    )(u5, uc, kk, et, ft, ap, dsk)
    return out.reshape(b, w, s)


def _s5_out_body(x, mod_ref, y_ref, gpost_ref, wt_ref):
    d = x.shape[1]
    gy = jax.nn.gelu(y_ref[...]).astype(BF16)
    z = jnp.dot(wt_ref[...], gy, preferred_element_type=F32)
    v = (z[:d, :] * jax.nn.sigmoid(z[d:, :])).T
    return x + mod_ref[5:6, :] * _rms(v, gpost_ref[...])


def _s5_out_stage(y_t, g_post, w_glu_t):
    def make(b, s, tm):
        w = y_t.shape[1]
        return _Stage(_s5_out_body,
                      [(pl.BlockSpec((None, w, tm), lambda bi, i: (bi, 0, i)), y_t),
                       _const_in(g_post.reshape(1, -1)), _const_in(w_glu_t)])
    return make


def _rope_angles(n, d_rot):
    d_axis = d_rot // 2
    inv = ROPE_BASE ** (-jnp.arange(0, d_axis, 2, dtype=F32) / d_axis)
    n_rows = n // GRID_W
    row_ang = jnp.arange(n_rows, dtype=F32)[:, None] * inv
    col_ang = jnp.arange(GRID_W, dtype=F32)[:, None] * inv

    def grid(fr, fc):
        r = jnp.broadcast_to(fr[:, None, :], (n_rows, GRID_W, fr.shape[1]))
        c = jnp.broadcast_to(fc[None, :, :], (n_rows, GRID_W, fc.shape[1]))
        return jnp.concatenate([r, c], axis=-1).reshape(n, -1)

    return grid(jnp.cos(row_ang), jnp.cos(col_ang)), grid(jnp.sin(row_ang), jnp.sin(col_ang))


def _rope_tables(n, rotate):
    pad = LANES - MLA_NOPE - MLA_ROPE
    if rotate:
        ca, sa = _rope_angles(n, MLA_ROPE)
        cb, sb = _rope_angles(n, SWA_HEAD_DIM)
    else:
        ca, sa = jnp.ones((n, MLA_ROPE // 2), F32), jnp.zeros((n, MLA_ROPE // 2), F32)
        cb, sb = jnp.ones((n, SWA_HEAD_DIM // 2), F32), jnp.zeros((n, SWA_HEAD_DIM // 2), F32)
    one, zero = jnp.ones((n, MLA_NOPE), F32), jnp.zeros((n, MLA_NOPE), F32)
    zpad = jnp.zeros((n, pad), F32)
    return (jnp.concatenate([one, ca, ca, zpad], axis=1), jnp.concatenate([zero, sa, sa, zpad], axis=1),
            jnp.concatenate([cb] * 4, axis=1), jnp.concatenate([sb] * 4, axis=1))


def _rot_partner(w, half):
    return jnp.concatenate([-w[..., half:], w[..., :half]], axis=-1)


def _attn_weights(w_in, w_uq, w_ukv, w_out):
    d = w_in.shape[0]
    sizes = [MLA_Q_LORA, MLA_KV_LORA, MLA_ROPE, SWA_HEADS * SWA_HEAD_DIM,
             SWA_KV_HEADS * SWA_HEAD_DIM, SWA_KV_HEADS * SWA_HEAD_DIM]
    cq, ckv, kpe, qs, ks, vs = jnp.split(w_in, [int(v) for v in np.cumsum(sizes)[:-1]], axis=1)
    pad_a = LANES - MLA_NOPE - MLA_ROPE
    z = lambda *shape: jnp.zeros(shape, F32)
    kpe_blk = jnp.concatenate([z(d, MLA_NOPE), kpe, z(d, pad_a)], axis=1)
    kpe_sw = jnp.concatenate([z(d, MLA_NOPE), _rot_partner(kpe, MLA_ROPE // 2), z(d, pad_a)], axis=1)
    qs3 = qs.reshape(d, SWA_HEADS, SWA_HEAD_DIM)
    qs_sw = _rot_partner(qs3, SWA_HEAD_DIM // 2).reshape(d, -1)
    ks3 = ks.reshape(d, SWA_KV_HEADS, SWA_HEAD_DIM)
    ks_sw3 = _rot_partner(ks3, SWA_HEAD_DIM // 2)
    dup = lambda t: jnp.concatenate([t, t], axis=-1).reshape(d, -1)
    vs3 = vs.reshape(d, SWA_KV_HEADS, SWA_HEAD_DIM)
    vs_pad = jnp.concatenate([vs3, jnp.zeros_like(vs3)], axis=-1).reshape(d, -1)
    w1 = jnp.concatenate([cq, ckv, kpe_blk, kpe_sw, qs, qs_sw, dup(ks3), dup(ks_sw3)], axis=1)
    wvst = vs_pad.T
    assert w1.shape[1] == _W1_COLS

    ql = w_uq.shape[0]
    uq = w_uq.reshape(ql, MLA_HEADS, MLA_NOPE + MLA_ROPE)
    nope, pe = uq[..., :MLA_NOPE], uq[..., MLA_NOPE:]
    zq = jnp.zeros((ql, MLA_HEADS, pad_a), F32)
    wq_blk = jnp.concatenate([nope, pe, zq], axis=-1).reshape(ql, -1)
    wq_sw = jnp.concatenate([jnp.zeros_like(nope), _rot_partner(pe, MLA_ROPE // 2), zq], axis=-1).reshape(ql, -1)
    wq = jnp.concatenate([wq_blk, wq_sw], axis=1)

    kl = w_ukv.shape[0]
    ukv = w_ukv.reshape(kl, MLA_HEADS, MLA_NOPE + MLA_V)
    kn, vv = ukv[..., :MLA_NOPE], ukv[..., MLA_NOPE:]
    wk = jnp.concatenate([kn, jnp.zeros((kl, MLA_HEADS, LANES - MLA_NOPE), F32)], axis=-1).reshape(kl, -1)
    wv = jnp.concatenate([vv, jnp.zeros((kl, MLA_HEADS, LANES - MLA_V), F32)], axis=-1).reshape(kl, -1)
    wvt = wv.T

    dm = w_out.shape[1]
    na = MLA_HEADS * MLA_V
    oa = w_out[:na].reshape(MLA_HEADS, MLA_V, dm)
    ob = w_out[na:].reshape(SWA_HEADS, SWA_HEAD_DIM, dm)
    wa = jnp.concatenate([oa, jnp.zeros((MLA_HEADS, LANES - MLA_V, dm), F32)], axis=1).reshape(-1, dm)
    wb = jnp.concatenate([ob, jnp.zeros((SWA_HEADS, LANES - SWA_HEAD_DIM, dm), F32)], axis=1).reshape(-1, dm)
    return tuple(t.astype(BF16) for t in (w1, wq, wk, wvt, wvst, wa, wb))


def _complex_powers(a_re, a_im, n):
    pr, pi = jnp.ones_like(a_re)[None], jnp.zeros_like(a_im)[None]
    sr, si = a_re, a_im
    while pr.shape[0] < n + 1:
        nr, ni = pr * sr - pi * si, pr * si + pi * sr
        pr, pi = jnp.concatenate([pr, nr], 0), jnp.concatenate([pi, ni], 0)
        sr, si = sr * sr - si * si, 2.0 * sr * si
    return pr[:n + 1], pi[:n + 1]


def _s5_operators(lam_re, lam_im, b_re, b_im, c_re, c_im, log_step):
    hi = lax.Precision.HIGHEST
    ln = S5_CHUNK
    g, p = lam_re.shape[1:]
    hh = b_re.shape[-1]
    ks, ets, fts, aps = [], [], [], []
    for dr in range(2):
        lre = jnp.minimum(lam_re[dr], S5_MAX_RE)
        lim = lam_im[dr]
        dt = jnp.exp(log_step[dr])[:, None]
        mag = jnp.exp(lre * dt)
        a_re, a_im = mag * jnp.cos(lim * dt), mag * jnp.sin(lim * dt)
        den = lre * lre + lim * lim
        f_re = ((a_re - 1.0) * lre + a_im * lim) / den
        f_im = (a_im * lre - (a_re - 1.0) * lim) / den
        bb_re = f_re[..., None] * b_re[dr] - f_im[..., None] * b_im[dr]
        bb_im = f_re[..., None] * b_im[dr] + f_im[..., None] * b_re[dr]
        cr, ci = c_re[dr], c_im[dr]
        pr, pi = _complex_powers(a_re, a_im, ln)
        cb_re = jnp.einsum('gjp,gpi->gpji', cr, bb_re) - jnp.einsum('gjp,gpi->gpji', ci, bb_im)
        cb_im = jnp.einsum('gjp,gpi->gpji', cr, bb_im) + jnp.einsum('gjp,gpi->gpji', ci, bb_re)
        k = (jnp.einsum('dgp,gpji->gijd', pr[:ln], cb_re, precision=hi)
             - jnp.einsum('dgp,gpji->gijd', pi[:ln], cb_im, precision=hi))
        ks.append(k)
        er, ei = (pr[:ln][::-1], pi[:ln][::-1]) if dr == 0 else (pr[:ln], pi[:ln])
        e_re = jnp.einsum('mgp,gpi->gimp', er, bb_re) - jnp.einsum('mgp,gpi->gimp', ei, bb_im)
        e_im = jnp.einsum('mgp,gpi->gimp', er, bb_im) + jnp.einsum('mgp,gpi->gimp', ei, bb_re)
        ets.append((e_re.reshape(g, hh * ln, p), e_im.reshape(g, hh * ln, p)))
        fr, fi = (pr[1:], pi[1:]) if dr == 0 else (pr[1:][::-1], pi[1:][::-1])
        f_xre = jnp.einsum('gjp,lgp->gpjl', cr, fr) - jnp.einsum('gjp,lgp->gpjl', ci, fi)
        f_xim = -(jnp.einsum('gjp,lgp->gpjl', cr, fi) + jnp.einsum('gjp,lgp->gpjl', ci, fr))
        fts.append((f_xre.reshape(g, p, hh * ln), f_xim.reshape(g, p, hh * ln)))
        aps.append((pr[ln], pi[ln]))
    kf, kb = ks
    k0 = kf[..., :1] + kb[..., :1]
    kk = jnp.concatenate([jnp.zeros_like(k0), kb[..., :0:-1], k0, kf[..., 1:]], axis=-1)
    kbits = lax.bitcast_convert_type(kk.astype(BF16), jnp.uint16).astype(jnp.uint32)
    kk = (kbits[..., ln:] << 16) | kbits[..., :ln]
    et =jnp.concatenate([ets[0][0], ets[1][0], ets[0][1], ets[1][1]], axis=-1).astype(BF16)
    ft = jnp.concatenate([fts[0][0], fts[1][0], fts[0][1], fts[1][1]], axis=1).astype(BF16)
    ap = jnp.stack([jnp.concatenate([aps[0][0], aps[1][0]], axis=-1),
                    jnp.concatenate([aps[0][1], aps[1][1]], axis=-1)], axis=1)
    return kk, et, ft, ap


def kernel(x, c, ctx, c_ctx, mod_w, mod_b, norm_pre, norm_post, ffn_w13, ffn_w2,
           attn_w_in, mla_q_norm, mla_w_uq, mla_kv_norm, mla_w_ukv, swa_sink, attn_w_out,
           s5_w_in, s5_lambda_re, s5_lambda_im, s5_b_re, s5_b_im, s5_c_re, s5_c_im,
           s5_log_step, s5_d, s5_w_glu):
    b, n, d = x.shape
    n_ctx = ctx.shape[1]
    depth = mod_w.shape[0]
    tm = 512

    rows = -(-(b + 1) // 8) * 8
    cs = jnp.concatenate([c, c_ctx[None], jnp.zeros((rows - b - 1, d), F32)], axis=0)
    mods = _modulation(cs, mod_w, mod_b)
    w13 = ffn_w13.astype(BF16)
    w2 = ffn_w2.astype(BF16)

    h_lat, h_ctx = x, ctx
    for l in range(depth):
        last = l == depth - 1
        m_lat = mods[l, :b].reshape(b, N_MOD, d)
        m_ctx = jnp.broadcast_to(mods[l, b].reshape(1, N_MOD, d), (b, N_MOD, d))
        ffn1 = functools.partial(_ffn, j=0, g_pre=norm_pre[l, 0], g_post=norm_post[l, 0],
                                 w13=w13[l, 0], w2=w2[l, 0], tm=tm)
        ffn2 = functools.partial(_ffn, j=2, g_pre=norm_pre[l, 2], g_post=norm_post[l, 2],
                                 w13=w13[l, 1], w2=w2[l, 1], tm=tm)
        if l % 2 == 0:
            e = l // 2
            w1, wq, wk, wvt, wvst, wa, wb = _attn_weights(attn_w_in[e], mla_w_uq[e], mla_w_ukv[e], attn_w_out[e])
            proj = functools.partial(_attn_proj_stage, norm_pre[l, 1], w1, mla_q_norm[e], wq,
                                     mla_kv_norm[e], wk, wvt, wvst)
            h_ctx, qa_c, ka_c, va_c, qs_c, ks_c, vs_c = ffn1(h_ctx, m_ctx, post=proj(_rope_tables(n_ctx, False)))
            h_lat, qa_l, ka_l, va_l, qs_l, ks_l, vs_l = ffn1(h_lat, m_lat, post=proj(_rope_tables(n, True)))
            o_a = _mla(qa_l, ka_c, va_c, ka_l, va_l, tq=1024, tk=512)
            o_b = _swa(swa_sink[e], qs_l, ks_c, vs_c, ks_l, vs_l)
            h_lat = ffn2(h_lat, m_lat, pre=_attn_out_stage(o_a, o_b, norm_post[l, 1], wa, wb))
            if not last:
                o_a_c = _mla(qa_c, ka_c, va_c, None, None, tq=1024, tk=512)
                o_b_c = _swa(swa_sink[e], qs_c, ks_c, vs_c, None, None)
                h_ctx = ffn2(h_ctx, m_ctx, pre=_attn_out_stage(o_a_c, o_b_c, norm_post[l, 1], wa, wb))
        else:
            o = l // 2
            assert last, "S5 context outputs are only needed when another layer follows"
            w_in_t = s5_w_in[o].T.astype(BF16)
            w_glu_t = s5_w_glu[o].T.astype(BF16)
            kk, et, ft, ap = _s5_operators(s5_lambda_re[o], s5_lambda_im[o], s5_b_re[o], s5_b_im[o],
                                             s5_c_re[o], s5_c_im[o], s5_log_step[o])
            width = s5_w_in.shape[2]
            g = width // S5_GROUP
            s5_in = _s5_in_stage(norm_pre[l, 1], w_in_t)
            h_lat, u_lat = ffn1(h_lat, m_lat, post=s5_in)
            h_ctx, u_ctx = ffn1(h_ctx, m_ctx, post=s5_in)
            ncc = n_ctx // S5_CHUNK
            uc = u_ctx.reshape(b, g, S5_GROUP, ncc, S5_CHUNK).transpose(1, 3, 0, 2, 4)
            uc = uc.reshape(g, ncc, b, S5_GROUP * S5_CHUNK)
            dsk = jnp.repeat(s5_d[o].astype(F32), S5_CHUNK).reshape(g, 1, S5_GROUP * S5_CHUNK)
            y_t = _s5_core(u_lat, uc, kk, et, ft, ap, dsk)
            h_lat = ffn2(h_lat, m_lat, pre=_s5_out_stage(y_t, norm_post[l, 1], w_glu_t))
    return h_lat
```

```python
import functools
import math

import numpy as np
import jax
import jax.numpy as jnp
from jax import lax
from jax.experimental import pallas as pl
from jax.experimental.pallas import tpu as pltpu

F32 = jnp.float32
BF16 = jnp.bfloat16

LANES = 128
VMEM_LIMIT = 56 * 1024 * 1024

N_MOD = 9
FFN_RES = 0.5
EPS = 1e-6
ROPE_BASE = 10000.0
GRID_W = 64
NEG_INF = -1e30

MLA_HEADS = 8
MLA_Q_LORA = 256
MLA_KV_LORA = 128
MLA_NOPE = 64
MLA_ROPE = 32
MLA_V = 64
MLA_SCALE = (MLA_NOPE + MLA_ROPE) ** -0.5

SWA_HEADS = 8
SWA_KV_HEADS = 2
SWA_GROUP = SWA_HEADS // SWA_KV_HEADS
SWA_HEAD_DIM = 64
SWA_WINDOW = 128
SWA_SCALE = SWA_HEAD_DIM ** -0.5
Q_BLOCK = 128

S5_GROUP = 16
S5_STATE = 64
S5_CHUNK = 128
S5_MAX_RE = -1e-4


def _params(n_grid):
    return pltpu.CompilerParams(dimension_semantics=("arbitrary",) * n_grid,
                                vmem_limit_bytes=VMEM_LIMIT)


def _rms(x, g):
    return x * lax.rsqrt(jnp.mean(x * x, axis=-1, keepdims=True) + EPS) * g


def _const_spec(shape):
    nd = len(shape)
    return pl.BlockSpec(shape, lambda *_: (0,) * nd, pipeline_mode=pl.Buffered(1))


def _dot_nt(a, b):
    return lax.dot_general(a, b, (((1,), (1,)), ((), ())), preferred_element_type=F32)


def _mod_kernel(c_ref, w_ref, b_ref, o_ref):
    a = jax.nn.silu(c_ref[...]).astype(BF16)
    o_ref[...] = jnp.dot(a, w_ref[...].astype(BF16), preferred_element_type=F32) + b_ref[...]


def _modulation(cs, mod_w, mod_b, tn=1024):
    depth, d, n = mod_w.shape
    r = cs.shape[0]
    return pl.pallas_call(
        _mod_kernel,
        name="modulation",
        grid=(depth, n // tn),
        in_specs=[pl.BlockSpec((r, d), lambda l, j: (0, 0)),
                  pl.BlockSpec((None, d, tn), lambda l, j: (l, 0, j)),
                  pl.BlockSpec((None, 1, tn), lambda l, j: (l, 0, j))],
        out_specs=pl.BlockSpec((None, r, tn), lambda l, j: (l, 0, j)),
        out_shape=jax.ShapeDtypeStruct((depth, r, n), F32),
        compiler_params=_params(2),
    )(cs, mod_w, mod_b.reshape(depth, 1, n))


class _Stage:
    def __init__(self, body, ins, outs=()):
        self.body, self.ins, self.outs = body, list(ins), list(outs)


def _ffn_kernel(*refs, j, fc, pre, post):
    h_ref, mod_ref, gpre_ref, gpost_ref, w13_ref, w2_ref = refs[:6]
    n_pre = len(pre.ins) if pre else 0
    n_post = len(post.ins) if post else 0
    pre_refs = refs[6:6 + n_pre]
    post_refs = refs[6 + n_pre:6 + n_pre + n_post]
    o_ref = refs[6 + n_pre + n_post]
    post_outs = refs[7 + n_pre + n_post:-1]
    acc_ref = refs[-1]
    x = h_ref[...]
    if pre:
        x = pre.body(x, mod_ref, *pre_refs)
    shift = mod_ref[3 * j:3 * j + 1, :]
    scale = mod_ref[3 * j + 1:3 * j + 2, :]
    gate = mod_ref[3 * j + 2:3 * j + 3, :]
    a = (_rms(x, gpre_ref[...]) * (1.0 + scale) + shift).astype(BF16)
    f = w2_ref.shape[0]
    for c in range(f // fc):
        g = jnp.dot(a, w13_ref[:, c * fc:(c + 1) * fc], preferred_element_type=F32)
        u = jnp.dot(a, w13_ref[:, f + c * fc:f + (c + 1) * fc], preferred_element_type=F32)
        act = (jax.nn.silu(g) * u).astype(BF16)
        contrib = jnp.dot(act, w2_ref[c * fc:(c + 1) * fc, :], preferred_element_type=F32)
        if c == 0:
            acc_ref[...] = contrib
        else:
            acc_ref[...] += contrib
    out = x + FFN_RES * gate * _rms(acc_ref[...], gpost_ref[...])
    o_ref[...] = out
    if post:
        post.body(out, mod_ref, *post_refs, *post_outs)


def _ffn(h, mod, j, g_pre, g_post, w13, w2, tm, pre=None, post=None):
    b, s, d = h.shape
    f = w2.shape[0]
    fc = 256 if f % 256 == 0 else f
    tm = min(tm, s)
    pre = pre(b, s, tm) if pre else None
    post = post(b, s, tm) if post else None
    extra = (pre.ins if pre else []) + (post.ins if post else [])
    outs = [(pl.BlockSpec((None, tm, d), lambda bi, i: (bi, i, 0)), jax.ShapeDtypeStruct((b, s, d), F32))]
    outs += post.outs if post else []
    res = pl.pallas_call(
        functools.partial(_ffn_kernel, j=j, fc=fc, pre=pre, post=post),
        name="ffn",
        grid=(b, s // tm),
        in_specs=[pl.BlockSpec((None, tm, d), lambda bi, i: (bi, i, 0)),
                  pl.BlockSpec((None, N_MOD, d), lambda bi, i: (bi, 0, 0)),
                  _const_spec((1, d)), _const_spec((1, d)),
                  _const_spec(w13.shape), _const_spec(w2.shape)] + [e[0] for e in extra],
        out_specs=[o[0] for o in outs],
        out_shape=[o[1] for o in outs],
        scratch_shapes=[pltpu.VMEM((tm, d), F32)],
        compiler_params=_params(2),
    )(h, mod, g_pre.reshape(1, d), g_post.reshape(1, d), w13, w2, *[e[1] for e in extra])
    return res if post else res[0]


_O_CQ = 0
_O_CKV = _O_CQ + MLA_Q_LORA
_O_KPE = _O_CKV + MLA_KV_LORA
_O_KPE_SW = _O_KPE + LANES
_O_QS = _O_KPE_SW + LANES
_O_QS_SW = _O_QS + SWA_HEADS * SWA_HEAD_DIM
_O_KS = _O_QS_SW + SWA_HEADS * SWA_HEAD_DIM
_O_KS_SW = _O_KS + SWA_KV_HEADS * LANES
_W1_COLS = _O_KS_SW + SWA_KV_HEADS * LANES
_HL = MLA_HEADS * LANES
_QS_W = SWA_HEADS * SWA_HEAD_DIM
_KS_W = SWA_KV_HEADS * LANES


def _attn_proj_body(x, mod_ref, gpre_ref, w1_ref, qn_ref, wq_ref, kvn_ref, wk_ref, wvt_ref, wvst_ref,
                    ca_ref, sa_ref, cb_ref, sb_ref,
                    qa_ref, ka_ref, vat_ref, qs_ref, ks_ref, vst_ref):
    shift = mod_ref[3:4, :]
    scale = mod_ref[4:5, :]
    a = (_rms(x, gpre_ref[...]) * (1.0 + scale) + shift).astype(BF16)
    p = jnp.dot(a, w1_ref[...], preferred_element_type=F32)
    ca, sa, cb, sb = ca_ref[...], sa_ref[...], cb_ref[...], sb_ref[...]

    cqn = _rms(p[:, _O_CQ:_O_CQ + MLA_Q_LORA], qn_ref[...]).astype(BF16)
    q2 = jnp.dot(cqn, wq_ref[...], preferred_element_type=F32)
    ckvn = _rms(p[:, _O_CKV:_O_CKV + MLA_KV_LORA], kvn_ref[...]).astype(BF16)
    kn = jnp.dot(ckvn, wk_ref[...], preferred_element_type=F32)
    kpe = p[:, _O_KPE:_O_KPE + LANES] * ca + p[:, _O_KPE_SW:_O_KPE_SW + LANES] * sa
    for hd in range(MLA_HEADS):
        lo, hi = hd * LANES, (hd + 1) * LANES
        q = q2[:, lo:hi] * ca + q2[:, _HL + lo:_HL + hi] * sa
        qa_ref[:, lo:hi] = (q * (MLA_SCALE * math.log2(math.e))).astype(BF16)
        ka_ref[:, lo:hi] = (kn[:, lo:hi] + kpe).astype(BF16)
    vt = _dot_nt(wvt_ref[...], ckvn)
    row = lax.broadcasted_iota(jnp.int32, (_HL, 1), 0)
    vat_ref[...] = (vt + (jnp.bitwise_and(row, LANES - 1) == MLA_V).astype(F32)).astype(BF16)

    for t in range(_QS_W // LANES):
        lo, hi = t * LANES, (t + 1) * LANES
        q = p[:, _O_QS + lo:_O_QS + hi] * cb + p[:, _O_QS_SW + lo:_O_QS_SW + hi] * sb
        qs_ref[:, lo:hi] = (q * (SWA_SCALE * math.log2(math.e))).astype(BF16)
    for t in range(SWA_KV_HEADS):
        lo, hi = t * LANES, (t + 1) * LANES
        k = p[:, _O_KS + lo:_O_KS + hi] * cb + p[:, _O_KS_SW + lo:_O_KS_SW + hi] * sb
        ks_ref[:, lo:hi] = k.astype(BF16)
    vst = _dot_nt(wvst_ref[...], a)
    row = lax.broadcasted_iota(jnp.int32, (_KS_W, 1), 0)
    vst_ref[...] = (vst + (jnp.bitwise_and(row, LANES - 1) == SWA_HEAD_DIM).astype(F32)).astype(BF16)


def _const_in(a):
    return (_const_spec(a.shape), a)


def _attn_proj_stage(g_pre, w1, q_norm, wq, kv_norm, wk, wvt, wvst, tabs):
    def make(b, s, tm):
        row = lambda bi, i: (bi, i, 0)
        tab = pl.BlockSpec((tm, LANES), lambda bi, i: (i, 0))
        tok = lambda w: (pl.BlockSpec((None, tm, w), row), jax.ShapeDtypeStruct((b, s, w), BF16))
        chan = lambda w: (pl.BlockSpec((None, w, tm), lambda bi, i: (bi, 0, i)),
                          jax.ShapeDtypeStruct((b, w, s), BF16))
        ins = [_const_in(g_pre.reshape(1, -1)), _const_in(w1), _const_in(q_norm.reshape(1, -1)), _const_in(wq),
               _const_in(kv_norm.reshape(1, -1)), _const_in(wk), _const_in(wvt), _const_in(wvst)]
        ins += [(tab, t) for t in tabs]
        return _Stage(_attn_proj_body, ins,
                      [tok(_HL), tok(_HL), chan(_HL), tok(_QS_W), tok(_KS_W), chan(_KS_W)])
    return make


def _mla_kernel(*refs, n_lat_chunks, tk):
    if n_lat_chunks:
        q_ref, kc_ref, vct_ref, kl_ref, vlt_ref, o_ref = refs
    else:
        q_ref, kc_ref, vct_ref, o_ref = refs
    q = q_ref[...]

    def scores(j):
        k = kc_ref[...] if j == 0 else kl_ref[(j - 1) * tk:j * tk, :]
        return _dot_nt(k, q)

    def update(j, st, m, acc):
        vt = vct_ref[...] if j == 0 else vlt_ref[:, (j - 1) * tk:j * tk]
        m_blk = jnp.max(st, axis=0, keepdims=True)
        m_new = m_blk if m is None else jnp.maximum(m, m_blk)
        pt = jnp.exp2(st - m_new).astype(BF16)
        pv = jnp.dot(vt, pt, preferred_element_type=F32)
        acc = pv if acc is None else acc * jnp.exp2(m - m_new) + pv
        return m_new, acc

    m, acc = None, None
    st_next = scores(0)
    for j in range(n_lat_chunks + 1):
        st = st_next
        if j < n_lat_chunks:
            st_next = scores(j + 1)
        m, acc = update(j, st, m, acc)
    o_ref[...] = (acc / acc[MLA_V:MLA_V + 1, :]).T.astype(o_ref.dtype)


def _mla(q, kc, vct, kl, vlt, tq, tk):
    b, s, _ = q.shape
    n_ctx = kc.shape[1]
    tq = min(tq, s)
    qspec = pl.BlockSpec((None, tq, LANES), lambda bi, h, i: (bi, i, h))
    in_specs = [qspec,
                pl.BlockSpec((None, n_ctx, LANES), lambda bi, h, i: (bi, 0, h)),
                pl.BlockSpec((None, LANES, n_ctx), lambda bi, h, i: (bi, h, 0))]
    args = [q, kc, vct]
    n_lat_chunks = 0
    if kl is not None:
        n_lat = kl.shape[1]
        tk = min(tk, n_lat)
        n_lat_chunks = n_lat // tk
        in_specs += [pl.BlockSpec((None, n_lat, LANES), lambda bi, h, i: (bi, 0, h)),
                     pl.BlockSpec((None, LANES, n_lat), lambda bi, h, i: (bi, h, 0))]
        args += [kl, vlt]
    return pl.pallas_call(
        functools.partial(_mla_kernel, n_lat_chunks=n_lat_chunks, tk=tk),
        name="mla_lat" if n_lat_chunks else "mla_ctx",
        grid=(b, MLA_HEADS, s // tq),
        in_specs=in_specs,
        out_specs=qspec,
        out_shape=jax.ShapeDtypeStruct((b, s, _HL), BF16),
        compiler_params=_params(3),
    )(*args)


def _swa_kernel(*refs, windowed, n_blocks):
    if windowed:
        (sink_ref, q_ref, kc_ref, vct_ref, kp_ref, k0_ref, kn_ref,
         vpt_ref, v0t_ref, vnt_ref, o_ref) = refs
    else:
        sink_ref, q_ref, kc_ref, vct_ref, o_ref = refs
    i = pl.program_id(1)
    tq = q_ref.shape[0]
    cols = SWA_GROUP * tq
    lane = lax.broadcasted_iota(jnp.int32, (1, LANES), 1)
    half_mask = [(lane < SWA_HEAD_DIM), (lane >= SWA_HEAD_DIM)]
    if windowed:
        w = SWA_WINDOW
        j = lax.broadcasted_iota(jnp.int32, (tq + 2 * w, cols), 0)
        r = jnp.bitwise_and(lax.broadcasted_iota(jnp.int32, (tq + 2 * w, cols), 1), tq - 1)
        j_lo = jnp.where(i > 0, r, jnp.maximum(r, w))
        j_hi = jnp.where(i < n_blocks - 1, r + 2 * w, jnp.minimum(r + 2 * w, w + tq - 1))
    staged = []
    for kv in range(SWA_KV_HEADS):
        lo, hi = kv * LANES, (kv + 1) * LANES
        qs, sk = [], []
        for g in range(SWA_GROUP):
            hd = kv * SWA_GROUP + g
            t = hd // 2
            qt = q_ref[:, t * LANES:(t + 1) * LANES]
            qs.append(jnp.where(half_mask[hd % 2], qt, jnp.zeros_like(qt)))
            sk.append(jnp.full((1, tq), sink_ref[hd] * math.log2(math.e), F32))
        q4 = jnp.concatenate(qs, axis=0)
        scores = [_dot_nt(kc_ref[:, lo:hi], q4)]
        if windowed:
            kw = jnp.concatenate([kp_ref[:, lo:hi], k0_ref[:, lo:hi], kn_ref[:, lo:hi]], axis=0)
            scores.append(_dot_nt(kw, q4))
        staged.append((scores, jnp.concatenate(sk, axis=1)))
    for kv in range(SWA_KV_HEADS):
        lo, hi = kv * LANES, (kv + 1) * LANES
        scores, sink = staged[kv]
        vals = [vct_ref[lo:hi, :]]
        if windowed:
            scores[1] = jnp.where(j >= j_lo, jnp.where(j <= j_hi, scores[1], NEG_INF), NEG_INF)
            vals.append(jnp.concatenate([vpt_ref[lo:hi, :], v0t_ref[lo:hi, :], vnt_ref[lo:hi, :]], axis=1))
        m = sink
        for s_ in scores:
            m = jnp.maximum(m, jnp.max(s_, axis=0, keepdims=True))
        acc = None
        for s_, vt in zip(scores, vals):
            pv = jnp.dot(vt, jnp.exp2(s_ - m).astype(BF16), preferred_element_type=F32)
            acc = pv if acc is None else acc + pv
        denom = acc[SWA_HEAD_DIM:SWA_HEAD_DIM + 1, :] + jnp.exp2(sink - m)
        o = acc / denom
        for g in range(SWA_GROUP):
            hd = kv * SWA_GROUP + g
            o_ref[:, hd * LANES:(hd + 1) * LANES] = o[:, g * tq:(g + 1) * tq].T.astype(o_ref.dtype)


def _swa(sink, q, kc, vct, kl, vlt):
    b, s, _ = q.shape
    n_ctx = kc.shape[1]
    windowed = kl is not None
    tq = (2 * Q_BLOCK if s % (2 * Q_BLOCK) == 0 else Q_BLOCK) if windowed else s
    nb = s // tq
    w = SWA_WINDOW
    per = tq // w
    in_specs = [pl.BlockSpec(memory_space=pltpu.SMEM),
                pl.BlockSpec((None, tq, _QS_W), lambda bi, i: (bi, i, 0)),
                pl.BlockSpec((None, n_ctx, _KS_W), lambda bi, i: (bi, 0, 0)),
                pl.BlockSpec((None, _KS_W, n_ctx), lambda bi, i: (bi, 0, 0))]
    args = [sink, q, kc, vct]
    if windowed:
        blocks = ((w, lambda i: jnp.maximum(i * per - 1, 0)), (tq, lambda i: i),
                  (w, lambda i: jnp.minimum((i + 1) * per, s // w - 1)))
        in_specs += [pl.BlockSpec((None, n, _KS_W), lambda bi, i, f=f: (bi, f(i), 0)) for n, f in blocks]
        in_specs += [pl.BlockSpec((None, _KS_W, n), lambda bi, i, f=f: (bi, 0, f(i))) for n, f in blocks]
        args += [kl, kl, kl, vlt, vlt, vlt]
    return pl.pallas_call(
        functools.partial(_swa_kernel, windowed=windowed, n_blocks=nb),
        name="swa_lat" if windowed else "swa_ctx",
        grid=(b, nb),
        in_specs=in_specs,
        out_specs=pl.BlockSpec((None, tq, SWA_HEADS * LANES), lambda bi, i: (bi, i, 0)),
        out_shape=jax.ShapeDtypeStruct((b, s, SWA_HEADS * LANES), BF16),
        compiler_params=_params(2),
    )(*args)


def _attn_out_body(x, mod_ref, oa_ref, ob_ref, gpost_ref, wa_ref, wb_ref):
    y = jnp.dot(oa_ref[...], wa_ref[...], preferred_element_type=F32)
    y = y + jnp.dot(ob_ref[...], wb_ref[...], preferred_element_type=F32)
    return x + mod_ref[5:6, :] * _rms(y, gpost_ref[...])


def _attn_out_stage(oa, ob, g_post, wa, wb):
    def make(b, s, tm):
        row = lambda bi, i: (bi, i, 0)
        return _Stage(_attn_out_body,
                      [(pl.BlockSpec((None, tm, oa.shape[2]), row), oa),
                       (pl.BlockSpec((None, tm, ob.shape[2]), row), ob),
                       _const_in(g_post.reshape(1, -1)), _const_in(wa), _const_in(wb)])
    return make


def _s5_in_body(x, mod_ref, gpre_ref, wt_ref, o_ref):
    a = (_rms(x, gpre_ref[...]) * (1.0 + mod_ref[4:5, :]) + mod_ref[3:4, :]).astype(BF16)
    o_ref[...] = _dot_nt(wt_ref[...], a)


def _s5_in_stage(g_pre, w_in_t):
    def make(b, s, tm):
        w = w_in_t.shape[0]
        return _Stage(_s5_in_body, [_const_in(g_pre.reshape(1, -1)), _const_in(w_in_t)],
                      [(pl.BlockSpec((None, w, tm), lambda bi, i: (bi, 0, i)),
                        jax.ShapeDtypeStruct((b, w, s), F32))])
    return make


def _s5_core_kernel(u_ref, uc_ref, kk_ref, et_ref, ft_ref, ap_ref, dsk_ref, o_ref,
                    er_ref, ei_ref, xrf_ref, xrb_ref, xif_ref, xib_ref, m_ref, *, nb, nc, ncc):
    h = S5_GROUP
    rows = nb * nc

    mi = lax.broadcasted_iota(jnp.int32, (S5_CHUNK, S5_CHUNK), 0)
    li = lax.broadcasted_iota(jnp.int32, (S5_CHUNK, S5_CHUNK), 1)
    causal = li >= mi

    def build(i, j):
        lags = kk_ref[i, j:j + 1, :]
        fwd = pltpu.roll(jnp.broadcast_to(lags[:, S5_CHUNK:], (S5_CHUNK, S5_CHUNK)), 0, 1,
                         stride=1, stride_axis=0)
        bwd = pltpu.roll(jnp.broadcast_to(lags[:, :S5_CHUNK], (S5_CHUNK, S5_CHUNK)), 0, 1,
                         stride=1, stride_axis=0)
        m_ref[i * S5_CHUNK:(i + 1) * S5_CHUNK, j * S5_CHUNK:(j + 1) * S5_CHUNK] = (
            jnp.where(causal, fwd, bwd).astype(BF16))

    u32 = jnp.concatenate([u_ref[:, i].reshape(rows, S5_CHUNK) for i in range(h)], axis=1)
    ub = u32.astype(BF16)
    et = et_ref[...]
    ein = jnp.dot(ub, et, preferred_element_type=F32)
    er_ref[...] = ein[:, :LANES]
    ei_ref[...] = ein[:, LANES:]
    ar, ai = ap_ref[0:1, :], ap_ref[1:2, :]
    fwd_lane = lax.broadcasted_iota(jnp.int32, (1, LANES), 1) < S5_STATE

    def step(xr, xi, e_r, e_i):
        return xr * ar - xi * ai + e_r, xr * ai + xi * ar + e_i

    xr = jnp.zeros((nb, LANES), F32)
    xi = jnp.zeros((nb, LANES), F32)
    for c in range(ncc):
        ecf = jnp.dot(uc_ref[c].astype(BF16), et, preferred_element_type=F32)
        ecb = jnp.dot(uc_ref[ncc - 1 - c].astype(BF16), et, preferred_element_type=F32)
        xr, xi = step(xr, xi, jnp.where(fwd_lane, ecf[:, :LANES], ecb[:, :LANES]),
                      jnp.where(fwd_lane, ecf[:, LANES:], ecb[:, LANES:]))

    for t in range(nc):
        cf = pl.ds(t, nb, stride=nc)
        cb = pl.ds(nc - 1 - t, nb, stride=nc)
        xrf_ref[cf, :] = xr
        xrb_ref[cb, :] = xr
        xif_ref[cf, :] = xi
        xib_ref[cb, :] = xi
        xr, xi = step(xr, xi, jnp.where(fwd_lane, er_ref[cf, :], er_ref[cb, :]),
                      jnp.where(fwd_lane, ei_ref[cf, :], ei_ref[cb, :]))
    xs = jnp.concatenate([jnp.where(fwd_lane, xrf_ref[...], xrb_ref[...]),
                          jnp.where(fwd_lane, xif_ref[...], xib_ref[...])], axis=1).astype(BF16)

    cw = 2 * S5_CHUNK
    for jp in range(h // 2):
        for i in range(h):
            build(i, 2 * jp)
            build(i, 2 * jp + 1)
        cs = slice(jp * cw, (jp + 1) * cw)
        y = jnp.dot(ub, m_ref[:, cs], preferred_element_type=F32)
        y = y + jnp.dot(xs, ft_ref[:, cs], preferred_element_type=F32)
        y = y + u32[:, cs] * dsk_ref[:, cs]
        o_ref[:, 2 * jp] = y[:, :S5_CHUNK].reshape(nb, nc, S5_CHUNK)
        o_ref[:, 2 * jp + 1] = y[:, S5_CHUNK:].reshape(nb, nc, S5_CHUNK)


def _s5_core(u_t, uc, kk, et, ft, ap, dsk):
    b, w, s = u_t.shape
    g = w // S5_GROUP
    nc = s // S5_CHUNK
    ncc = uc.shape[1]
    hl = S5_GROUP * S5_CHUNK
    u5 = u_t.reshape(b, g, S5_GROUP, nc, S5_CHUNK)
    blk = pl.BlockSpec((b, None, S5_GROUP, nc, S5_CHUNK), lambda gi: (0, gi, 0, 0, 0))
    per_g = lambda shape: pl.BlockSpec((None,) + shape, lambda gi: (gi,) + (0,) * len(shape))
    out = pl.pallas_call(
        functools.partial(_s5_core_kernel, nb=b, nc=nc, ncc=ncc),
        name="s5_core",
        grid=(g,),
        in_specs=[blk, per_g((ncc, b, hl)),
                  per_g((S5_GROUP, S5_GROUP, 2 * S5_CHUNK)),
                  per_g((hl, 2 * LANES)), per_g((2 * LANES, hl)),
                  per_g((2, LANES)), per_g((1, hl))],
        out_specs=blk,
        out_shape=jax.ShapeDtypeStruct(u5.shape, F32),
        scratch_shapes=[pltpu.VMEM((b * nc, LANES), F32) for _ in range(6)]
        + [pltpu.VMEM((hl, hl), BF16)],
        compiler_params=_params(1),
    )(u5, uc, kk, et, ft, ap, dsk)
    return out.reshape(b, w, s)


def _s5_out_body(x, mod_ref, y_ref, gpost_ref, wt_ref):
    d = x.shape[1]
    gy = jax.nn.gelu(y_ref[...]).astype(BF16)
    z = jnp.dot(wt_ref[...], gy, preferred_element_type=F32)
    v = (z[:d, :] * jax.nn.sigmoid(z[d:, :])).T
    return x + mod_ref[5:6, :] * _rms(v, gpost_ref[...])


def _s5_out_stage(y_t, g_post, w_glu_t):
    def make(b, s, tm):
        w = y_t.shape[1]
        return _Stage(_s5_out_body,
                      [(pl.BlockSpec((None, w, tm), lambda bi, i: (bi, 0, i)), y_t),
                       _const_in(g_post.reshape(1, -1)), _const_in(w_glu_t)])
    return make


def _rope_angles(n, d_rot):
    d_axis = d_rot // 2
    inv = ROPE_BASE ** (-jnp.arange(0, d_axis, 2, dtype=F32) / d_axis)
    n_rows = n // GRID_W
    row_ang = jnp.arange(n_rows, dtype=F32)[:, None] * inv
    col_ang = jnp.arange(GRID_W, dtype=F32)[:, None] * inv

    def grid(fr, fc):
        r = jnp.broadcast_to(fr[:, None, :], (n_rows, GRID_W, fr.shape[1]))
        c = jnp.broadcast_to(fc[None, :, :], (n_rows, GRID_W, fc.shape[1]))
        return jnp.concatenate([r, c], axis=-1).reshape(n, -1)

    return grid(jnp.cos(row_ang), jnp.cos(col_ang)), grid(jnp.sin(row_ang), jnp.sin(col_ang))


def _rope_tables(n, rotate):
    pad = LANES - MLA_NOPE - MLA_ROPE
    if rotate:
        ca, sa = _rope_angles(n, MLA_ROPE)
        cb, sb = _rope_angles(n, SWA_HEAD_DIM)
    else:
        ca, sa = jnp.ones((n, MLA_ROPE // 2), F32), jnp.zeros((n, MLA_ROPE // 2), F32)
        cb, sb = jnp.ones((n, SWA_HEAD_DIM // 2), F32), jnp.zeros((n, SWA_HEAD_DIM // 2), F32)
    one, zero = jnp.ones((n, MLA_NOPE), F32), jnp.zeros((n, MLA_NOPE), F32)
    zpad = jnp.zeros((n, pad), F32)
    return (jnp.concatenate([one, ca, ca, zpad], axis=1), jnp.concatenate([zero, sa, sa, zpad], axis=1),
            jnp.concatenate([cb] * 4, axis=1), jnp.concatenate([sb] * 4, axis=1))


def _rot_partner(w, half):
    return jnp.concatenate([-w[..., half:], w[..., :half]], axis=-1)


def _attn_weights(w_in, w_uq, w_ukv, w_out):
    d = w_in.shape[0]
    sizes = [MLA_Q_LORA, MLA_KV_LORA, MLA_ROPE, SWA_HEADS * SWA_HEAD_DIM,
             SWA_KV_HEADS * SWA_HEAD_DIM, SWA_KV_HEADS * SWA_HEAD_DIM]
    cq, ckv, kpe, qs, ks, vs = jnp.split(w_in, [int(v) for v in np.cumsum(sizes)[:-1]], axis=1)
    pad_a = LANES - MLA_NOPE - MLA_ROPE
    z = lambda *shape: jnp.zeros(shape, F32)
    kpe_blk = jnp.concatenate([z(d, MLA_NOPE), kpe, z(d, pad_a)], axis=1)
    kpe_sw = jnp.concatenate([z(d, MLA_NOPE), _rot_partner(kpe, MLA_ROPE // 2), z(d, pad_a)], axis=1)
    qs3 = qs.reshape(d, SWA_HEADS, SWA_HEAD_DIM)
    qs_sw = _rot_partner(qs3, SWA_HEAD_DIM // 2).reshape(d, -1)
    ks3 = ks.reshape(d, SWA_KV_HEADS, SWA_HEAD_DIM)
    ks_sw3 = _rot_partner(ks3, SWA_HEAD_DIM // 2)
    dup = lambda t: jnp.concatenate([t, t], axis=-1).reshape(d, -1)
    vs3 = vs.reshape(d, SWA_KV_HEADS, SWA_HEAD_DIM)
    vs_pad = jnp.concatenate([vs3, jnp.zeros_like(vs3)], axis=-1).reshape(d, -1)
    w1 = jnp.concatenate([cq, ckv, kpe_blk, kpe_sw, qs, qs_sw, dup(ks3), dup(ks_sw3)], axis=1)
    wvst = vs_pad.T
    assert w1.shape[1] == _W1_COLS

    ql = w_uq.shape[0]
    uq = w_uq.reshape(ql, MLA_HEADS, MLA_NOPE + MLA_ROPE)
    nope, pe = uq[..., :MLA_NOPE], uq[..., MLA_NOPE:]
    zq = jnp.zeros((ql, MLA_HEADS, pad_a), F32)
    wq_blk = jnp.concatenate([nope, pe, zq], axis=-1).reshape(ql, -1)
    wq_sw = jnp.concatenate([jnp.zeros_like(nope), _rot_partner(pe, MLA_ROPE // 2), zq], axis=-1).reshape(ql, -1)
    wq = jnp.concatenate([wq_blk, wq_sw], axis=1)

    kl = w_ukv.shape[0]
    ukv = w_ukv.reshape(kl, MLA_HEADS, MLA_NOPE + MLA_V)
    kn, vv = ukv[..., :MLA_NOPE], ukv[..., MLA_NOPE:]
    wk = jnp.concatenate([kn, jnp.zeros((kl, MLA_HEADS, LANES - MLA_NOPE), F32)], axis=-1).reshape(kl, -1)
    wv = jnp.concatenate([vv, jnp.zeros((kl, MLA_HEADS, LANES - MLA_V), F32)], axis=-1).reshape(kl, -1)
    wvt = wv.T

    dm = w_out.shape[1]
    na = MLA_HEADS * MLA_V
    oa = w_out[:na].reshape(MLA_HEADS, MLA_V, dm)
    ob = w_out[na:].reshape(SWA_HEADS, SWA_HEAD_DIM, dm)
    wa = jnp.concatenate([oa, jnp.zeros((MLA_HEADS, LANES - MLA_V, dm), F32)], axis=1).reshape(-1, dm)
    wb = jnp.concatenate([ob, jnp.zeros((SWA_HEADS, LANES - SWA_HEAD_DIM, dm), F32)], axis=1).reshape(-1, dm)
    return tuple(t.astype(BF16) for t in (w1, wq, wk, wvt, wvst, wa, wb))


def _complex_powers(a_re, a_im, n):
    pr, pi = jnp.ones_like(a_re)[None], jnp.zeros_like(a_im)[None]
    sr, si = a_re, a_im
    while pr.shape[0] < n + 1:
        nr, ni = pr * sr - pi * si, pr * si + pi * sr
        pr, pi = jnp.concatenate([pr, nr], 0), jnp.concatenate([pi, ni], 0)
        sr, si = sr * sr - si * si, 2.0 * sr * si
    return pr[:n + 1], pi[:n + 1]


def _s5_operators(lam_re, lam_im, b_re, b_im, c_re, c_im, log_step):
    hi = lax.Precision.HIGHEST
    ln = S5_CHUNK
    g, p = lam_re.shape[1:]
    hh = b_re.shape[-1]
    ks, ets, fts, aps = [], [], [], []
    for dr in range(2):
        lre = jnp.minimum(lam_re[dr], S5_MAX_RE)
        lim = lam_im[dr]
        dt = jnp.exp(log_step[dr])[:, None]
        mag = jnp.exp(lre * dt)
        a_re, a_im = mag * jnp.cos(lim * dt), mag * jnp.sin(lim * dt)
        den = lre * lre + lim * lim
        f_re = ((a_re - 1.0) * lre + a_im * lim) / den
        f_im = (a_im * lre - (a_re - 1.0) * lim) / den
        bb_re = f_re[..., None] * b_re[dr] - f_im[..., None] * b_im[dr]
        bb_im = f_re[..., None] * b_im[dr] + f_im[..., None] * b_re[dr]
        cr, ci = c_re[dr], c_im[dr]
        pr, pi = _complex_powers(a_re, a_im, ln)
        cb_re = jnp.einsum('gjp,gpi->gpji', cr, bb_re) - jnp.einsum('gjp,gpi->gpji', ci, bb_im)
        cb_im = jnp.einsum('gjp,gpi->gpji', cr, bb_im) + jnp.einsum('gjp,gpi->gpji', ci, bb_re)
        k = (jnp.einsum('dgp,gpji->gijd', pr[:ln], cb_re, precision=hi)
             - jnp.einsum('dgp,gpji->gijd', pi[:ln], cb_im, precision=hi))
        ks.append(k)
        er, ei = (pr[:ln][::-1], pi[:ln][::-1]) if dr == 0 else (pr[:ln], pi[:ln])
        e_re = jnp.einsum('mgp,gpi->gimp', er, bb_re) - jnp.einsum('mgp,gpi->gimp', ei, bb_im)
        e_im = jnp.einsum('mgp,gpi->gimp', er, bb_im) + jnp.einsum('mgp,gpi->gimp', ei, bb_re)
        ets.append((e_re.reshape(g, hh * ln, p), e_im.reshape(g, hh * ln, p)))
        fr, fi = (pr[1:], pi[1:]) if dr == 0 else (pr[1:][::-1], pi[1:][::-1])
        f_xre = jnp.einsum('gjp,lgp->gpjl', cr, fr) - jnp.einsum('gjp,lgp->gpjl', ci, fi)
        f_xim = -(jnp.einsum('gjp,lgp->gpjl', cr, fi) + jnp.einsum('gjp,lgp->gpjl', ci, fr))
        fts.append((f_xre.reshape(g, p, hh * ln), f_xim.reshape(g, p, hh * ln)))
        aps.append((pr[ln], pi[ln]))
    kf, kb = ks
    k0 = kf[..., :1] + kb[..., :1]
    kk = jnp.concatenate([jnp.zeros_like(k0), kb[..., :0:-1], k0, kf[..., 1:]], axis=-1)
    et = jnp.concatenate([ets[0][0], ets[1][0], ets[0][1], ets[1][1]], axis=-1).astype(BF16)
    ft = jnp.concatenate([fts[0][0], fts[1][0], fts[0][1], fts[1][1]], axis=1).astype(BF16)
    ap = jnp.stack([jnp.concatenate([aps[0][0], aps[1][0]], axis=-1),
                    jnp.concatenate([aps[0][1], aps[1][1]], axis=-1)], axis=1)
    return kk, et, ft, ap


def kernel(x, c, ctx, c_ctx, mod_w, mod_b, norm_pre, norm_post, ffn_w13, ffn_w2,
           attn_w_in, mla_q_norm, mla_w_uq, mla_kv_norm, mla_w_ukv, swa_sink, attn_w_out,
           s5_w_in, s5_lambda_re, s5_lambda_im, s5_b_re, s5_b_im, s5_c_re, s5_c_im,
           s5_log_step, s5_d, s5_w_glu):
    b, n, d = x.shape
    n_ctx = ctx.shape[1]
    depth = mod_w.shape[0]
    tm = 512

    rows = -(-(b + 1) // 8) * 8
    cs = jnp.concatenate([c, c_ctx[None], jnp.zeros((rows - b - 1, d), F32)], axis=0)
    mods = _modulation(cs, mod_w, mod_b)
    w13 = ffn_w13.astype(BF16)
    w2 = ffn_w2.astype(BF16)

    h_lat, h_ctx = x, ctx
    for l in range(depth):
        last = l == depth - 1
        m_lat = mods[l, :b].reshape(b, N_MOD, d)
        m_ctx = jnp.broadcast_to(mods[l, b].reshape(1, N_MOD, d), (b, N_MOD, d))
        ffn1 = functools.partial(_ffn, j=0, g_pre=norm_pre[l, 0], g_post=norm_post[l, 0],
                                 w13=w13[l, 0], w2=w2[l, 0], tm=tm)
        ffn2 = functools.partial(_ffn, j=2, g_pre=norm_pre[l, 2], g_post=norm_post[l, 2],
                                 w13=w13[l, 1], w2=w2[l, 1], tm=tm)
        if l % 2 == 0:
            e = l // 2
            w1, wq, wk, wvt, wvst, wa, wb = _attn_weights(attn_w_in[e], mla_w_uq[e], mla_w_ukv[e], attn_w_out[e])
            proj = functools.partial(_attn_proj_stage, norm_pre[l, 1], w1, mla_q_norm[e], wq,
                                     mla_kv_norm[e], wk, wvt, wvst)
            h_ctx, qa_c, ka_c, va_c, qs_c, ks_c, vs_c = ffn1(h_ctx, m_ctx, post=proj(_rope_tables(n_ctx, False)))
            h_lat, qa_l, ka_l, va_l, qs_l, ks_l, vs_l = ffn1(h_lat, m_lat, post=proj(_rope_tables(n, True)))
            o_a = _mla(qa_l, ka_c, va_c, ka_l, va_l, tq=1024, tk=512)
            o_b = _swa(swa_sink[e], qs_l, ks_c, vs_c, ks_l, vs_l)
            h_lat = ffn2(h_lat, m_lat, pre=_attn_out_stage(o_a, o_b, norm_post[l, 1], wa, wb))
            if not last:
                o_a_c = _mla(qa_c, ka_c, va_c, None, None, tq=1024, tk=512)
                o_b_c = _swa(swa_sink[e], qs_c, ks_c, vs_c, None, None)
                h_ctx = ffn2(h_ctx, m_ctx, pre=_attn_out_stage(o_a_c, o_b_c, norm_post[l, 1], wa, wb))
        else:
            o = l // 2
            assert last, "S5 context outputs are only needed when another layer follows"
            w_in_t = s5_w_in[o].T.astype(BF16)
            w_glu_t = s5_w_glu[o].T.astype(BF16)
            kk, et, ft, ap = _s5_operators(s5_lambda_re[o], s5_lambda_im[o], s5_b_re[o], s5_b_im[o],
                                             s5_c_re[o], s5_c_im[o], s5_log_step[o])
            width = s5_w_in.shape[2]
            g = width // S5_GROUP
            s5_in = _s5_in_stage(norm_pre[l, 1], w_in_t)
            h_lat, u_lat = ffn1(h_lat, m_lat, post=s5_in)
            h_ctx, u_ctx = ffn1(h_ctx, m_ctx, post=s5_in)
            ncc = n_ctx // S5_CHUNK
            uc = u_ctx.reshape(b, g, S5_GROUP, ncc, S5_CHUNK).transpose(1, 3, 0, 2, 4)
            uc = uc.reshape(g, ncc, b, S5_GROUP * S5_CHUNK)
            dsk = jnp.repeat(s5_d[o].astype(F32), S5_CHUNK).reshape(g, 1, S5_GROUP * S5_CHUNK)
            y_t = _s5_core(u_lat, uc, kk, et, ft, ap, dsk)
            h_lat = ffn2(h_lat, m_lat, pre=_s5_out_stage(y_t, norm_post[l, 1], w_glu_t))
    return h_lat
```

```python
import functools
import math

import numpy as np
import jax
import jax.numpy as jnp
from jax import lax
from jax.experimental import pallas as pl
from jax.experimental.pallas import tpu as pltpu

F32 = jnp.float32
BF16 = jnp.bfloat16

LANES = 128
VMEM_LIMIT = 56 * 1024 * 1024

N_MOD = 9
FFN_RES = 0.5
EPS = 1e-6
ROPE_BASE = 10000.0
GRID_W = 64
NEG_INF = -1e30

MLA_HEADS = 8
MLA_Q_LORA = 256
MLA_KV_LORA = 128
MLA_NOPE = 64
MLA_ROPE = 32
MLA_V = 64
MLA_SCALE = (MLA_NOPE + MLA_ROPE) ** -0.5

SWA_HEADS = 8
SWA_KV_HEADS = 2
SWA_GROUP = SWA_HEADS // SWA_KV_HEADS
SWA_HEAD_DIM = 64
SWA_WINDOW = 128
SWA_SCALE = SWA_HEAD_DIM ** -0.5
Q_BLOCK = 128

S5_GROUP = 16
S5_STATE = 64
S5_CHUNK = 128
S5_MAX_RE = -1e-4


def _params(n_grid):
    return pltpu.CompilerParams(dimension_semantics=("arbitrary",) * n_grid,
                                vmem_limit_bytes=VMEM_LIMIT)


def _rms(x, g):
    return x * lax.rsqrt(jnp.mean(x * x, axis=-1, keepdims=True) + EPS) * g


def _const_spec(shape):
    nd = len(shape)
    return pl.BlockSpec(shape, lambda *_: (0,) * nd, pipeline_mode=pl.Buffered(1))


def _dot_nt(a, b):
    return lax.dot_general(a, b, (((1,), (1,)), ((), ())), preferred_element_type=F32)


def _mod_kernel(c_ref, w_ref, b_ref, o_ref):
    a = jax.nn.silu(c_ref[...]).astype(BF16)
    o_ref[...] = jnp.dot(a, w_ref[...].astype(BF16), preferred_element_type=F32) + b_ref[...]


def _modulation(cs, mod_w, mod_b, tn=1024):
    depth, d, n = mod_w.shape
    r = cs.shape[0]
    return pl.pallas_call(
        _mod_kernel,
        name="modulation",
        grid=(depth, n // tn),
        in_specs=[pl.BlockSpec((r, d), lambda l, j: (0, 0)),
                  pl.BlockSpec((None, d, tn), lambda l, j: (l, 0, j)),
                  pl.BlockSpec((None, 1, tn), lambda l, j: (l, 0, j))],
        out_specs=pl.BlockSpec((None, r, tn), lambda l, j: (l, 0, j)),
        out_shape=jax.ShapeDtypeStruct((depth, r, n), F32),
        compiler_params=_params(2),
    )(cs, mod_w, mod_b.reshape(depth, 1, n))


class _Stage:
    def __init__(self, body, ins, outs=()):
        self.body, self.ins, self.outs = body, list(ins), list(outs)


def _ffn_kernel(*refs, j, fc, pre, post):
    h_ref, mod_ref, gpre_ref, gpost_ref, w13_ref, w2_ref = refs[:6]
    n_pre = len(pre.ins) if pre else 0
    n_post = len(post.ins) if post else 0
    pre_refs = refs[6:6 + n_pre]
    post_refs = refs[6 + n_pre:6 + n_pre + n_post]
    o_ref = refs[6 + n_pre + n_post]
    post_outs = refs[7 + n_pre + n_post:-1]
    acc_ref = refs[-1]
    x = h_ref[...]
    if pre:
        x = pre.body(x, mod_ref, *pre_refs)
    shift = mod_ref[3 * j:3 * j + 1, :]
    scale = mod_ref[3 * j + 1:3 * j + 2, :]
    gate = mod_ref[3 * j + 2:3 * j + 3, :]
    a = (_rms(x, gpre_ref[...]) * (1.0 + scale) + shift).astype(BF16)
    f = w2_ref.shape[0]
    for c in range(f // fc):
        g = jnp.dot(a, w13_ref[:, c * fc:(c + 1) * fc], preferred_element_type=F32)
        u = jnp.dot(a, w13_ref[:, f + c * fc:f + (c + 1) * fc], preferred_element_type=F32)
        act = (jax.nn.silu(g) * u).astype(BF16)
        contrib = jnp.dot(act, w2_ref[c * fc:(c + 1) * fc, :], preferred_element_type=F32)
        if c == 0:
            acc_ref[...] = contrib
        else:
            acc_ref[...] += contrib
    out = x + FFN_RES * gate * _rms(acc_ref[...], gpost_ref[...])
    o_ref[...] = out
    if post:
        post.body(out, mod_ref, *post_refs, *post_outs)


def _ffn(h, mod, j, g_pre, g_post, w13, w2, tm, pre=None, post=None):
    b, s, d = h.shape
    f = w2.shape[0]
    fc = 256 if f % 256 == 0 else f
    tm = min(tm, s)
    pre = pre(b, s, tm) if pre else None
    post = post(b, s, tm) if post else None
    extra = (pre.ins if pre else []) + (post.ins if post else [])
    outs = [(pl.BlockSpec((None, tm, d), lambda bi, i: (bi, i, 0)), jax.ShapeDtypeStruct((b, s, d), F32))]
    outs += post.outs if post else []
    res = pl.pallas_call(
        functools.partial(_ffn_kernel, j=j, fc=fc, pre=pre, post=post),
        name="ffn",
        grid=(b, s // tm),
        in_specs=[pl.BlockSpec((None, tm, d), lambda bi, i: (bi, i, 0)),
                  pl.BlockSpec((None, N_MOD, d), lambda bi, i: (bi, 0, 0)),
                  _const_spec((1, d)), _const_spec((1, d)),
                  _const_spec(w13.shape), _const_spec(w2.shape)] + [e[0] for e in extra],
        out_specs=[o[0] for o in outs],
        out_shape=[o[1] for o in outs],
        scratch_shapes=[pltpu.VMEM((tm, d), F32)],
        compiler_params=_params(2),
    )(h, mod, g_pre.reshape(1, d), g_post.reshape(1, d), w13, w2, *[e[1] for e in extra])
    return res if post else res[0]


_O_CQ = 0
_O_CKV = _O_CQ + MLA_Q_LORA
_O_KPE = _O_CKV + MLA_KV_LORA
_O_KPE_SW = _O_KPE + LANES
_O_QS = _O_KPE_SW + LANES
_O_QS_SW = _O_QS + SWA_HEADS * SWA_HEAD_DIM
_O_KS = _O_QS_SW + SWA_HEADS * SWA_HEAD_DIM
_O_KS_SW = _O_KS + SWA_KV_HEADS * LANES
_W1_COLS = _O_KS_SW + SWA_KV_HEADS * LANES
_HL = MLA_HEADS * LANES
_QS_W = SWA_HEADS * SWA_HEAD_DIM
_KS_W = SWA_KV_HEADS * LANES


def _attn_proj_body(x, mod_ref, gpre_ref, w1_ref, qn_ref, wq_ref, kvn_ref, wk_ref, wvt_ref, wvst_ref,
                    ca_ref, sa_ref, cb_ref, sb_ref,
                    qa_ref, ka_ref, vat_ref, qs_ref, ks_ref, vst_ref):
    shift = mod_ref[3:4, :]
    scale = mod_ref[4:5, :]
    a = (_rms(x, gpre_ref[...]) * (1.0 + scale) + shift).astype(BF16)
    p = jnp.dot(a, w1_ref[...], preferred_element_type=F32)
    ca, sa, cb, sb = ca_ref[...], sa_ref[...], cb_ref[...], sb_ref[...]

    cqn = _rms(p[:, _O_CQ:_O_CQ + MLA_Q_LORA], qn_ref[...]).astype(BF16)
    q2 = jnp.dot(cqn, wq_ref[...], preferred_element_type=F32)
    ckvn = _rms(p[:, _O_CKV:_O_CKV + MLA_KV_LORA], kvn_ref[...]).astype(BF16)
    kn = jnp.dot(ckvn, wk_ref[...], preferred_element_type=F32)
    kpe = p[:, _O_KPE:_O_KPE + LANES] * ca + p[:, _O_KPE_SW:_O_KPE_SW + LANES] * sa
    for hd in range(MLA_HEADS):
        lo, hi = hd * LANES, (hd + 1) * LANES
        q = q2[:, lo:hi] * ca + q2[:, _HL + lo:_HL + hi] * sa
        qa_ref[:, lo:hi] = (q * (MLA_SCALE * math.log2(math.e))).astype(BF16)
        ka_ref[:, lo:hi] = (kn[:, lo:hi] + kpe).astype(BF16)
    vt = _dot_nt(wvt_ref[...], ckvn)
    row = lax.broadcasted_iota(jnp.int32, (_HL, 1), 0)
    vat_ref[...] = (vt + (jnp.bitwise_and(row, LANES - 1) == MLA_V).astype(F32)).astype(BF16)

    for t in range(_QS_W // LANES):
        lo, hi = t * LANES, (t + 1) * LANES
        q = p[:, _O_QS + lo:_O_QS + hi] * cb + p[:, _O_QS_SW + lo:_O_QS_SW + hi] * sb
        qs_ref[:, lo:hi] = (q * (SWA_SCALE * math.log2(math.e))).astype(BF16)
    for t in range(SWA_KV_HEADS):
        lo, hi = t * LANES, (t + 1) * LANES
        k = p[:, _O_KS + lo:_O_KS + hi] * cb + p[:, _O_KS_SW + lo:_O_KS_SW + hi] * sb
        ks_ref[:, lo:hi] = k.astype(BF16)
    vst = _dot_nt(wvst_ref[...], a)
    row = lax.broadcasted_iota(jnp.int32, (_KS_W, 1), 0)
    vst_ref[...] = (vst + (jnp.bitwise_and(row, LANES - 1) == SWA_HEAD_DIM).astype(F32)).astype(BF16)


def _const_in(a):
    return (_const_spec(a.shape), a)


def _attn_proj_stage(g_pre, w1, q_norm, wq, kv_norm, wk, wvt, wvst, tabs):
    def make(b, s, tm):
        row = lambda bi, i: (bi, i, 0)
        tab = pl.BlockSpec((tm, LANES), lambda bi, i: (i, 0))
        tok = lambda w: (pl.BlockSpec((None, tm, w), row), jax.ShapeDtypeStruct((b, s, w), BF16))
        chan = lambda w: (pl.BlockSpec((None, w, tm), lambda bi, i: (bi, 0, i)),
                          jax.ShapeDtypeStruct((b, w, s), BF16))
        ins = [_const_in(g_pre.reshape(1, -1)), _const_in(w1), _const_in(q_norm.reshape(1, -1)), _const_in(wq),
               _const_in(kv_norm.reshape(1, -1)), _const_in(wk), _const_in(wvt), _const_in(wvst)]
        ins += [(tab, t) for t in tabs]
        return _Stage(_attn_proj_body, ins,
                      [tok(_HL), tok(_HL), chan(_HL), tok(_QS_W), tok(_KS_W), chan(_KS_W)])
    return make


def _mla_kernel(*refs, n_lat_chunks, tk):
    if n_lat_chunks:
        q_ref, kc_ref, vct_ref, kl_ref, vlt_ref, o_ref = refs
    else:
        q_ref, kc_ref, vct_ref, o_ref = refs
    q = q_ref[...]

    def scores(j):
        k = kc_ref[...] if j == 0 else kl_ref[(j - 1) * tk:j * tk, :]
        return _dot_nt(k, q)

    def update(j, st, m, acc):
        vt = vct_ref[...] if j == 0 else vlt_ref[:, (j - 1) * tk:j * tk]
        m_blk = jnp.max(st, axis=0, keepdims=True)
        m_new = m_blk if m is None else jnp.maximum(m, m_blk)
        pt = jnp.exp2(st - m_new).astype(BF16)
        pv = jnp.dot(vt, pt, preferred_element_type=F32)
        acc = pv if acc is None else acc * jnp.exp2(m - m_new) + pv
        return m_new, acc

    m, acc = None, None
    st_next = scores(0)
    for j in range(n_lat_chunks + 1):
        st = st_next
        if j < n_lat_chunks:
            st_next = scores(j + 1)
        m, acc = update(j, st, m, acc)
    o_ref[...] = (acc / acc[MLA_V:MLA_V + 1, :]).T.astype(o_ref.dtype)


def _mla(q, kc, vct, kl, vlt, tq, tk):
    b, s, _ = q.shape
    n_ctx = kc.shape[1]
    tq = min(tq, s)
    qspec = pl.BlockSpec((None, tq, LANES), lambda bi, h, i: (bi, i, h))
    in_specs = [qspec,
                pl.BlockSpec((None, n_ctx, LANES), lambda bi, h, i: (bi, 0, h)),
                pl.BlockSpec((None, LANES, n_ctx), lambda bi, h, i: (bi, h, 0))]
    args = [q, kc, vct]
    n_lat_chunks = 0
    if kl is not None:
        n_lat = kl.shape[1]
        tk = min(tk, n_lat)
        n_lat_chunks = n_lat // tk
        in_specs += [pl.BlockSpec((None, n_lat, LANES), lambda bi, h, i: (bi, 0, h)),
                     pl.BlockSpec((None, LANES, n_lat), lambda bi, h, i: (bi, h, 0))]
        args += [kl, vlt]
    return pl.pallas_call(
        functools.partial(_mla_kernel, n_lat_chunks=n_lat_chunks, tk=tk),
        name="mla_lat" if n_lat_chunks else "mla_ctx",
        grid=(b, MLA_HEADS, s // tq),
        in_specs=in_specs,
        out_specs=qspec,
        out_shape=jax.ShapeDtypeStruct((b, s, _HL), BF16),
        compiler_params=_params(3),
    )(*args)


def _swa_kernel(*refs, windowed, n_blocks):
    if windowed:
        (sink_ref, q_ref, kc_ref, vct_ref, kp_ref, k0_ref, kn_ref,
         vpt_ref, v0t_ref, vnt_ref, o_ref) = refs
    else:
        sink_ref, q_ref, kc_ref, vct_ref, o_ref = refs
    i = pl.program_id(1)
    tq = q_ref.shape[0]
    cols = SWA_GROUP * tq
    lane = lax.broadcasted_iota(jnp.int32, (1, LANES), 1)
    half_mask = [(lane < SWA_HEAD_DIM), (lane >= SWA_HEAD_DIM)]
    if windowed:
        w = SWA_WINDOW
        j = lax.broadcasted_iota(jnp.int32, (tq + 2 * w, cols), 0)
        r = jnp.bitwise_and(lax.broadcasted_iota(jnp.int32, (tq + 2 * w, cols), 1), tq - 1)
        j_lo = jnp.where(i > 0, r, jnp.maximum(r, w))
        j_hi = jnp.where(i < n_blocks - 1, r + 2 * w, jnp.minimum(r + 2 * w, w + tq - 1))
    staged = []
    for kv in range(SWA_KV_HEADS):
        lo, hi = kv * LANES, (kv + 1) * LANES
        qs, sk = [], []
        for g in range(SWA_GROUP):
            hd = kv * SWA_GROUP + g
            t = hd // 2
            qt = q_ref[:, t * LANES:(t + 1) * LANES]
            qs.append(jnp.where(half_mask[hd % 2], qt, jnp.zeros_like(qt)))
            sk.append(jnp.full((1, tq), sink_ref[hd] * math.log2(math.e), F32))
        q4 = jnp.concatenate(qs, axis=0)
        scores = [_dot_nt(kc_ref[:, lo:hi], q4)]
        if windowed:
            kw = jnp.concatenate([kp_ref[:, lo:hi], k0_ref[:, lo:hi], kn_ref[:, lo:hi]], axis=0)
            scores.append(_dot_nt(kw, q4))
        staged.append((scores, jnp.concatenate(sk, axis=1)))
    for kv in range(SWA_KV_HEADS):
        lo, hi = kv * LANES, (kv + 1) * LANES
        scores, sink = staged[kv]
        vals = [vct_ref[lo:hi, :]]
        if windowed:
            scores[1] = jnp.where(j >= j_lo, jnp.where(j <= j_hi, scores[1], NEG_INF), NEG_INF)
            vals.append(jnp.concatenate([vpt_ref[lo:hi, :], v0t_ref[lo:hi, :], vnt_ref[lo:hi, :]], axis=1))
        m = sink
        for s_ in scores:
            m = jnp.maximum(m, jnp.max(s_, axis=0, keepdims=True))
        acc = None
        for s_, vt in zip(scores, vals):
            pv = jnp.dot(vt, jnp.exp2(s_ - m).astype(BF16), preferred_element_type=F32)
            acc = pv if acc is None else acc + pv
        denom = acc[SWA_HEAD_DIM:SWA_HEAD_DIM + 1, :] + jnp.exp2(sink - m)
        o = acc / denom
        for g in range(SWA_GROUP):
            hd = kv * SWA_GROUP + g
            o_ref[:, hd * LANES:(hd + 1) * LANES] = o[:, g * tq:(g + 1) * tq].T.astype(o_ref.dtype)


def _swa(sink, q, kc, vct, kl, vlt):
    b, s, _ = q.shape
    n_ctx = kc.shape[1]
    windowed = kl is not None
    tq = next(n * Q_BLOCK for n in (4, 2, 1) if s % (n * Q_BLOCK) == 0) if windowed else s
    nb = s // tq
    w = SWA_WINDOW
    per = tq // w
    in_specs = [pl.BlockSpec(memory_space=pltpu.SMEM),
                pl.BlockSpec((None, tq, _QS_W), lambda bi, i: (bi, i, 0)),
                pl.BlockSpec((None, n_ctx, _KS_W), lambda bi, i: (bi, 0, 0)),
                pl.BlockSpec((None, _KS_W, n_ctx), lambda bi, i: (bi, 0, 0))]
    args = [sink, q, kc, vct]
    if windowed:
        blocks = ((w, lambda i: jnp.maximum(i * per - 1, 0)), (tq, lambda i: i),
                  (w, lambda i: jnp.minimum((i + 1) * per, s // w - 1)))
        in_specs += [pl.BlockSpec((None, n, _KS_W), lambda bi, i, f=f: (bi, f(i), 0)) for n, f in blocks]
        in_specs += [pl.BlockSpec((None, _KS_W, n), lambda bi, i, f=f: (bi, 0, f(i))) for n, f in blocks]
        args += [kl, kl, kl, vlt, vlt, vlt]
    return pl.pallas_call(
        functools.partial(_swa_kernel, windowed=windowed, n_blocks=nb),
        name="swa_lat" if windowed else "swa_ctx",
        grid=(b, nb),
        in_specs=in_specs,
        out_specs=pl.BlockSpec((None, tq, SWA_HEADS * LANES), lambda bi, i: (bi, i, 0)),
        out_shape=jax.ShapeDtypeStruct((b, s, SWA_HEADS * LANES), BF16),
        compiler_params=_params(2),
    )(*args)


def _attn_out_body(x, mod_ref, oa_ref, ob_ref, gpost_ref, wa_ref, wb_ref):
    y = jnp.dot(oa_ref[...], wa_ref[...], preferred_element_type=F32)
    y = y + jnp.dot(ob_ref[...], wb_ref[...], preferred_element_type=F32)
    return x + mod_ref[5:6, :] * _rms(y, gpost_ref[...])


def _attn_out_stage(oa, ob, g_post, wa, wb):
    def make(b, s, tm):
        row = lambda bi, i: (bi, i, 0)
        return _Stage(_attn_out_body,
                      [(pl.BlockSpec((None, tm, oa.shape[2]), row), oa),
                       (pl.BlockSpec((None, tm, ob.shape[2]), row), ob),
                       _const_in(g_post.reshape(1, -1)), _const_in(wa), _const_in(wb)])
    return make


def _s5_in_body(x, mod_ref, gpre_ref, wt_ref, o_ref):
    a = (_rms(x, gpre_ref[...]) * (1.0 + mod_ref[4:5, :]) + mod_ref[3:4, :]).astype(BF16)
    o_ref[...] = _dot_nt(wt_ref[...], a)


def _s5_in_stage(g_pre, w_in_t):
    def make(b, s, tm):
        w = w_in_t.shape[0]
        return _Stage(_s5_in_body, [_const_in(g_pre.reshape(1, -1)), _const_in(w_in_t)],
                      [(pl.BlockSpec((None, w, tm), lambda bi, i: (bi, 0, i)),
                        jax.ShapeDtypeStruct((b, w, s), F32))])
    return make


def _s5_core_kernel(u_ref, uc_ref, kk_ref, et_ref, ft_ref, ap_ref, dsk_ref, o_ref,
                    er_ref, ei_ref, xrf_ref, xrb_ref, xif_ref, xib_ref, m_ref, *, nb, nc, ncc):
    h = S5_GROUP
    rows = nb * nc

    mi = lax.broadcasted_iota(jnp.int32, (S5_CHUNK, S5_CHUNK), 0)
    li = lax.broadcasted_iota(jnp.int32, (S5_CHUNK, S5_CHUNK), 1)
    causal = li >= mi

    def build(i, j):
        lags = kk_ref[i, j:j + 1, :]
        fwd = pltpu.roll(jnp.broadcast_to(lags[:, S5_CHUNK:], (S5_CHUNK, S5_CHUNK)), 0, 1,
                         stride=1, stride_axis=0)
        bwd = pltpu.roll(jnp.broadcast_to(lags[:, :S5_CHUNK], (S5_CHUNK, S5_CHUNK)), 0, 1,
                         stride=1, stride_axis=0)
        m_ref[i * S5_CHUNK:(i + 1) * S5_CHUNK, j * S5_CHUNK:(j + 1) * S5_CHUNK] = (
            jnp.where(causal, fwd, bwd).astype(BF16))

    u32 = jnp.concatenate([u_ref[:, i].reshape(rows, S5_CHUNK) for i in range(h)], axis=1)
    ub = u32.astype(BF16)
    et = et_ref[...]
    ein = jnp.dot(ub, et, preferred_element_type=F32)
    er_ref[...] = ein[:, :LANES]
    ei_ref[...] = ein[:, LANES:]
    ar, ai = ap_ref[0:1, :], ap_ref[1:2, :]
    fwd_lane = lax.broadcasted_iota(jnp.int32, (1, LANES), 1) < S5_STATE

    def step(xr, xi, e_r, e_i):
        return xr * ar - xi * ai + e_r, xr * ai + xi * ar + e_i

    xr = jnp.zeros((nb, LANES), F32)
    xi = jnp.zeros((nb, LANES), F32)
    for c in range(ncc):
        ecf = jnp.dot(uc_ref[c].astype(BF16), et, preferred_element_type=F32)
        ecb = jnp.dot(uc_ref[ncc - 1 - c].astype(BF16), et, preferred_element_type=F32)
        xr, xi = step(xr, xi, jnp.where(fwd_lane, ecf[:, :LANES], ecb[:, :LANES]),
                      jnp.where(fwd_lane, ecf[:, LANES:], ecb[:, LANES:]))

    for t in range(nc):
        cf = pl.ds(t, nb, stride=nc)
        cb = pl.ds(nc - 1 - t, nb, stride=nc)
        xrf_ref[cf, :] = xr
        xrb_ref[cb, :] = xr
        xif_ref[cf, :] = xi
        xib_ref[cb, :] = xi
        xr, xi = step(xr, xi, jnp.where(fwd_lane, er_ref[cf, :], er_ref[cb, :]),
                      jnp.where(fwd_lane, ei_ref[cf, :], ei_ref[cb, :]))
    xs = jnp.concatenate([jnp.where(fwd_lane, xrf_ref[...], xrb_ref[...]),
                          jnp.where(fwd_lane, xif_ref[...], xib_ref[...])], axis=1).astype(BF16)

    cw = 2 * S5_CHUNK
    for jp in range(h // 2):
        for i in range(h):
            build(i, 2 * jp)
            build(i, 2 * jp + 1)
        cs = slice(jp * cw, (jp + 1) * cw)
        y = jnp.dot(ub, m_ref[:, cs], preferred_element_type=F32)
        y = y + jnp.dot(xs, ft_ref[:, cs], preferred_element_type=F32)
        y = y + u32[:, cs] * dsk_ref[:, cs]
        o_ref[:, 2 * jp] = y[:, :S5_CHUNK].reshape(nb, nc, S5_CHUNK)
        o_ref[:, 2 * jp + 1] = y[:, S5_CHUNK:].reshape(nb, nc, S5_CHUNK)


def _s5_core(u_t, uc, kk, et, ft, ap, dsk):
    b, w, s = u_t.shape
    g = w // S5_GROUP
    nc = s // S5_CHUNK
    ncc = uc.shape[1]
    hl = S5_GROUP * S5_CHUNK
    u5 = u_t.reshape(b, g, S5_GROUP, nc, S5_CHUNK)
    blk = pl.BlockSpec((b, None, S5_GROUP, nc, S5_CHUNK), lambda gi: (0, gi, 0, 0, 0))
    per_g = lambda shape: pl.BlockSpec((None,) + shape, lambda gi: (gi,) + (0,) * len(shape))
    out = pl.pallas_call(
        functools.partial(_s5_core_kernel, nb=b, nc=nc, ncc=ncc),
        name="s5_core",
        grid=(g,),
        in_specs=[blk, per_g((ncc, b, hl)),
                  per_g((S5_GROUP, S5_GROUP, 2 * S5_CHUNK)),
                  per_g((hl, 2 * LANES)), per_g((2 * LANES, hl)),
                  per_g((2, LANES)), per_g((1, hl))],
        out_specs=blk,
        out_shape=jax.ShapeDtypeStruct(u5.shape, F32),
        scratch_shapes=[pltpu.VMEM((b * nc, LANES), F32) for _ in range(6)]
        + [pltpu.VMEM((hl, hl), BF16)],
        compiler_params=_params(1),
    )(u5, uc, kk, et, ft, ap, dsk)
    return out.reshape(b, w, s)


def _s5_out_body(x, mod_ref, y_ref, gpost_ref, wt_ref):
    d = x.shape[1]
    gy = jax.nn.gelu(y_ref[...]).astype(BF16)
    z = jnp.dot(wt_ref[...], gy, preferred_element_type=F32)
    v = (z[:d, :] * jax.nn.sigmoid(z[d:, :])).T
    return x + mod_ref[5:6, :] * _rms(v, gpost_ref[...])


def _s5_out_stage(y_t, g_post, w_glu_t):
    def make(b, s, tm):
        w = y_t.shape[1]
        return _Stage(_s5_out_body,
                      [(pl.BlockSpec((None, w, tm), lambda bi, i: (bi, 0, i)), y_t),
                       _const_in(g_post.reshape(1, -1)), _const_in(w_glu_t)])
    return make


def _rope_angles(n, d_rot):
    d_axis = d_rot // 2
    inv = ROPE_BASE ** (-jnp.arange(0, d_axis, 2, dtype=F32) / d_axis)
    n_rows = n // GRID_W
    row_ang = jnp.arange(n_rows, dtype=F32)[:, None] * inv
    col_ang = jnp.arange(GRID_W, dtype=F32)[:, None] * inv

    def grid(fr, fc):
        r = jnp.broadcast_to(fr[:, None, :], (n_rows, GRID_W, fr.shape[1]))
        c = jnp.broadcast_to(fc[None, :, :], (n_rows, GRID_W, fc.shape[1]))
        return jnp.concatenate([r, c], axis=-1).reshape(n, -1)

    return grid(jnp.cos(row_ang), jnp.cos(col_ang)), grid(jnp.sin(row_ang), jnp.sin(col_ang))


def _rope_tables(n, rotate):
    pad = LANES - MLA_NOPE - MLA_ROPE
    if rotate:
        ca, sa = _rope_angles(n, MLA_ROPE)
        cb, sb = _rope_angles(n, SWA_HEAD_DIM)
    else:
        ca, sa = jnp.ones((n, MLA_ROPE // 2), F32), jnp.zeros((n, MLA_ROPE // 2), F32)
        cb, sb = jnp.ones((n, SWA_HEAD_DIM // 2), F32), jnp.zeros((n, SWA_HEAD_DIM // 2), F32)
    one, zero = jnp.ones((n, MLA_NOPE), F32), jnp.zeros((n, MLA_NOPE), F32)
    zpad = jnp.zeros((n, pad), F32)
    return (jnp.concatenate([one, ca, ca, zpad], axis=1), jnp.concatenate([zero, sa, sa, zpad], axis=1),
            jnp.concatenate([cb] * 4, axis=1), jnp.concatenate([sb] * 4, axis=1))


def _rot_partner(w, half):
    return jnp.concatenate([-w[..., half:], w[..., :half]], axis=-1)


def _attn_weights(w_in, w_uq, w_ukv, w_out):
    d = w_in.shape[0]
    sizes = [MLA_Q_LORA, MLA_KV_LORA, MLA_ROPE, SWA_HEADS * SWA_HEAD_DIM,
             SWA_KV_HEADS * SWA_HEAD_DIM, SWA_KV_HEADS * SWA_HEAD_DIM]
    cq, ckv, kpe, qs, ks, vs = jnp.split(w_in, [int(v) for v in np.cumsum(sizes)[:-1]], axis=1)
    pad_a = LANES - MLA_NOPE - MLA_ROPE
    z = lambda *shape: jnp.zeros(shape, F32)
    kpe_blk = jnp.concatenate([z(d, MLA_NOPE), kpe, z(d, pad_a)], axis=1)
    kpe_sw = jnp.concatenate([z(d, MLA_NOPE), _rot_partner(kpe, MLA_ROPE // 2), z(d, pad_a)], axis=1)
    qs3 = qs.reshape(d, SWA_HEADS, SWA_HEAD_DIM)
    qs_sw = _rot_partner(qs3, SWA_HEAD_DIM // 2).reshape(d, -1)
    ks3 = ks.reshape(d, SWA_KV_HEADS, SWA_HEAD_DIM)
    ks_sw3 = _rot_partner(ks3, SWA_HEAD_DIM // 2)
    dup = lambda t: jnp.concatenate([t, t], axis=-1).reshape(d, -1)
    vs3 = vs.reshape(d, SWA_KV_HEADS, SWA_HEAD_DIM)
    vs_pad = jnp.concatenate([vs3, jnp.zeros_like(vs3)], axis=-1).reshape(d, -1)
    w1 = jnp.concatenate([cq, ckv, kpe_blk, kpe_sw, qs, qs_sw, dup(ks3), dup(ks_sw3)], axis=1)
    wvst = vs_pad.T
    assert w1.shape[1] == _W1_COLS

    ql = w_uq.shape[0]
    uq = w_uq.reshape(ql, MLA_HEADS, MLA_NOPE + MLA_ROPE)
    nope, pe = uq[..., :MLA_NOPE], uq[..., MLA_NOPE:]
    zq = jnp.zeros((ql, MLA_HEADS, pad_a), F32)
    wq_blk = jnp.concatenate([nope, pe, zq], axis=-1).reshape(ql, -1)
    wq_sw = jnp.concatenate([jnp.zeros_like(nope), _rot_partner(pe, MLA_ROPE // 2), zq], axis=-1).reshape(ql, -1)
    wq = jnp.concatenate([wq_blk, wq_sw], axis=1)

    kl = w_ukv.shape[0]
    ukv = w_ukv.reshape(kl, MLA_HEADS, MLA_NOPE + MLA_V)
    kn, vv = ukv[..., :MLA_NOPE], ukv[..., MLA_NOPE:]
    wk = jnp.concatenate([kn, jnp.zeros((kl, MLA_HEADS, LANES - MLA_NOPE), F32)], axis=-1).reshape(kl, -1)
    wv = jnp.concatenate([vv, jnp.zeros((kl, MLA_HEADS, LANES - MLA_V), F32)], axis=-1).reshape(kl, -1)
    wvt = wv.T

    dm = w_out.shape[1]
    na = MLA_HEADS * MLA_V
    oa = w_out[:na].reshape(MLA_HEADS, MLA_V, dm)
    ob = w_out[na:].reshape(SWA_HEADS, SWA_HEAD_DIM, dm)
    wa = jnp.concatenate([oa, jnp.zeros((MLA_HEADS, LANES - MLA_V, dm), F32)], axis=1).reshape(-1, dm)
    wb = jnp.concatenate([ob, jnp.zeros((SWA_HEADS, LANES - SWA_HEAD_DIM, dm), F32)], axis=1).reshape(-1, dm)
    return tuple(t.astype(BF16) for t in (w1, wq, wk, wvt, wvst, wa, wb))


def _complex_powers(a_re, a_im, n):
    pr, pi = jnp.ones_like(a_re)[None], jnp.zeros_like(a_im)[None]
    sr, si = a_re, a_im
    while pr.shape[0] < n + 1:
        nr, ni = pr * sr - pi * si, pr * si + pi * sr
        pr, pi = jnp.concatenate([pr, nr], 0), jnp.concatenate([pi, ni], 0)
        sr, si = sr * sr - si * si, 2.0 * sr * si
    return pr[:n + 1], pi[:n + 1]


def _s5_operators(lam_re, lam_im, b_re, b_im, c_re, c_im, log_step):
    hi = lax.Precision.HIGHEST
    ln = S5_CHUNK
    g, p = lam_re.shape[1:]
    hh = b_re.shape[-1]
    ks, ets, fts, aps = [], [], [], []
    for dr in range(2):
        lre = jnp.minimum(lam_re[dr], S5_MAX_RE)
        lim = lam_im[dr]
        dt = jnp.exp(log_step[dr])[:, None]
        mag = jnp.exp(lre * dt)
        a_re, a_im = mag * jnp.cos(lim * dt), mag * jnp.sin(lim * dt)
        den = lre * lre + lim * lim
        f_re = ((a_re - 1.0) * lre + a_im * lim) / den
        f_im = (a_im * lre - (a_re - 1.0) * lim) / den
        bb_re = f_re[..., None] * b_re[dr] - f_im[..., None] * b_im[dr]
        bb_im = f_re[..., None] * b_im[dr] + f_im[..., None] * b_re[dr]
        cr, ci = c_re[dr], c_im[dr]
        pr, pi = _complex_powers(a_re, a_im, ln)
        cb_re = jnp.einsum('gjp,gpi->gpji', cr, bb_re) - jnp.einsum('gjp,gpi->gpji', ci, bb_im)
        cb_im = jnp.einsum('gjp,gpi->gpji', cr, bb_im) + jnp.einsum('gjp,gpi->gpji', ci, bb_re)
        k = (jnp.einsum('dgp,gpji->gijd', pr[:ln], cb_re, precision=hi)
             - jnp.einsum('dgp,gpji->gijd', pi[:ln], cb_im, precision=hi))
        ks.append(k)
        er, ei = (pr[:ln][::-1], pi[:ln][::-1]) if dr == 0 else (pr[:ln], pi[:ln])
        e_re = jnp.einsum('mgp,gpi->gimp', er, bb_re) - jnp.einsum('mgp,gpi->gimp', ei, bb_im)
        e_im = jnp.einsum('mgp,gpi->gimp', er, bb_im) + jnp.einsum('mgp,gpi->gimp', ei, bb_re)
        ets.append((e_re.reshape(g, hh * ln, p), e_im.reshape(g, hh * ln, p)))
        fr, fi = (pr[1:], pi[1:]) if dr == 0 else (pr[1:][::-1], pi[1:][::-1])
        f_xre = jnp.einsum('gjp,lgp->gpjl', cr, fr) - jnp.einsum('gjp,lgp->gpjl', ci, fi)
        f_xim = -(jnp.einsum('gjp,lgp->gpjl', cr, fi) + jnp.einsum('gjp,lgp->gpjl', ci, fr))
        fts.append((f_xre.reshape(g, p, hh * ln), f_xim.reshape(g, p, hh * ln)))
        aps.append((pr[ln], pi[ln]))
    kf, kb = ks
    k0 = kf[..., :1] + kb[..., :1]
    kk = jnp.concatenate([jnp.zeros_like(k0), kb[..., :0:-1], k0, kf[..., 1:]], axis=-1)
    et = jnp.concatenate([ets[0][0], ets[1][0], ets[0][1], ets[1][1]], axis=-1).astype(BF16)
    ft = jnp.concatenate([fts[0][0], fts[1][0], fts[0][1], fts[1][1]], axis=1).astype(BF16)
    ap = jnp.stack([jnp.concatenate([aps[0][0], aps[1][0]], axis=-1),
                    jnp.concatenate([aps[0][1], aps[1][1]], axis=-1)], axis=1)
    return kk, et, ft, ap


def kernel(x, c, ctx, c_ctx, mod_w, mod_b, norm_pre, norm_post, ffn_w13, ffn_w2,
           attn_w_in, mla_q_norm, mla_w_uq, mla_kv_norm, mla_w_ukv, swa_sink, attn_w_out,
           s5_w_in, s5_lambda_re, s5_lambda_im, s5_b_re, s5_b_im, s5_c_re, s5_c_im,
           s5_log_step, s5_d, s5_w_glu):
    b, n, d = x.shape
    n_ctx = ctx.shape[1]
    depth = mod_w.shape[0]
    tm = 512

    rows = -(-(b + 1) // 8) * 8
    cs = jnp.concatenate([c, c_ctx[None], jnp.zeros((rows - b - 1, d), F32)], axis=0)
    mods = _modulation(cs, mod_w, mod_b)
    w13 = ffn_w13.astype(BF16)
    w2 = ffn_w2.astype(BF16)

    h_lat, h_ctx = x, ctx
    for l in range(depth):
        last = l == depth - 1
        m_lat = mods[l, :b].reshape(b, N_MOD, d)
        m_ctx = jnp.broadcast_to(mods[l, b].reshape(1, N_MOD, d), (b, N_MOD, d))
        ffn1 = functools.partial(_ffn, j=0, g_pre=norm_pre[l, 0], g_post=norm_post[l, 0],
                                 w13=w13[l, 0], w2=w2[l, 0], tm=tm)
        ffn2 = functools.partial(_ffn, j=2, g_pre=norm_pre[l, 2], g_post=norm_post[l, 2],
                                 w13=w13[l, 1], w2=w2[l, 1], tm=tm)
        if l % 2 == 0:
            e = l // 2
            w1, wq, wk, wvt, wvst, wa, wb = _attn_weights(attn_w_in[e], mla_w_uq[e], mla_w_ukv[e], attn_w_out[e])
            proj = functools.partial(_attn_proj_stage, norm_pre[l, 1], w1, mla_q_norm[e], wq,
                                     mla_kv_norm[e], wk, wvt, wvst)
            h_ctx, qa_c, ka_c, va_c, qs_c, ks_c, vs_c = ffn1(h_ctx, m_ctx, post=proj(_rope_tables(n_ctx, False)))
            h_lat, qa_l, ka_l, va_l, qs_l, ks_l, vs_l = ffn1(h_lat, m_lat, post=proj(_rope_tables(n, True)))
            o_a = _mla(qa_l, ka_c, va_c, ka_l, va_l, tq=2048, tk=512)
            o_b = _swa(swa_sink[e], qs_l, ks_c, vs_c, ks_l, vs_l)
            h_lat = ffn2(h_lat, m_lat, pre=_attn_out_stage(o_a, o_b, norm_post[l, 1], wa, wb))
            if not last:
                o_a_c = _mla(qa_c, ka_c, va_c, None, None, tq=1024, tk=512)
                o_b_c = _swa(swa_sink[e], qs_c, ks_c, vs_c, None, None)
                h_ctx = ffn2(h_ctx, m_ctx, pre=_attn_out_stage(o_a_c, o_b_c, norm_post[l, 1], wa, wb))
        else:
            o = l // 2
            assert last, "S5 context outputs are only needed when another layer follows"
            w_in_t = s5_w_in[o].T.astype(BF16)
            w_glu_t = s5_w_glu[o].T.astype(BF16)
            kk, et, ft, ap = _s5_operators(s5_lambda_re[o], s5_lambda_im[o], s5_b_re[o], s5_b_im[o],
                                             s5_c_re[o], s5_c_im[o], s5_log_step[o])
            width = s5_w_in.shape[2]
            g = width // S5_GROUP
            s5_in = _s5_in_stage(norm_pre[l, 1], w_in_t)
            h_lat, u_lat = ffn1(h_lat, m_lat, post=s5_in)
            h_ctx, u_ctx = ffn1(h_ctx, m_ctx, post=s5_in)
            ncc = n_ctx // S5_CHUNK
            uc = u_ctx.reshape(b, g, S5_GROUP, ncc, S5_CHUNK).transpose(1, 3, 0, 2, 4)
            uc = uc.reshape(g, ncc, b, S5_GROUP * S5_CHUNK)
            dsk = jnp.repeat(s5_d[o].astype(F32), S5_CHUNK).reshape(g, 1, S5_GROUP * S5_CHUNK)
            y_t = _s5_core(u_lat, uc, kk, et, ft, ap, dsk)
            h_lat = ffn2(h_lat, m_lat, pre=_s5_out_stage(y_t, norm_post[l, 1], w_glu_t))
    return h_lat
```

```python
import functools
import math

import numpy as np
import jax
import jax.numpy as jnp
from jax import lax
from jax.experimental import pallas as pl
from jax.experimental.pallas import tpu as pltpu

F32 = jnp.float32
BF16 = jnp.bfloat16

LANES = 128
VMEM_LIMIT = 56 * 1024 * 1024

N_MOD = 9
FFN_RES = 0.5
EPS = 1e-6
ROPE_BASE = 10000.0
GRID_W = 64
NEG_INF = -1e30

MLA_HEADS = 8
MLA_Q_LORA = 256
MLA_KV_LORA = 128
MLA_NOPE = 64
MLA_ROPE = 32
MLA_V = 64
MLA_SCALE = (MLA_NOPE + MLA_ROPE) ** -0.5

SWA_HEADS = 8
SWA_KV_HEADS = 2
SWA_GROUP = SWA_HEADS // SWA_KV_HEADS
SWA_HEAD_DIM = 64
SWA_WINDOW = 128
SWA_SCALE = SWA_HEAD_DIM ** -0.5
Q_BLOCK = 128

S5_GROUP = 16
S5_STATE = 64
S5_CHUNK = 128
S5_MAX_RE = -1e-4


def _params(n_grid):
    return pltpu.CompilerParams(dimension_semantics=("arbitrary",) * n_grid,
                                vmem_limit_bytes=VMEM_LIMIT)


def _rms(x, g):
    return x * lax.rsqrt(jnp.mean(x * x, axis=-1, keepdims=True) + EPS) * g


def _const_spec(shape):
    nd = len(shape)
    return pl.BlockSpec(shape, lambda *_: (0,) * nd, pipeline_mode=pl.Buffered(1))


def _dot_nt(a, b):
    return lax.dot_general(a, b, (((1,), (1,)), ((), ())), preferred_element_type=F32)


def _mod_kernel(c_ref, w_ref, b_ref, o_ref):
    a = jax.nn.silu(c_ref[...]).astype(BF16)
    o_ref[...] = jnp.dot(a, w_ref[...].astype(BF16), preferred_element_type=F32) + b_ref[...]


def _modulation(cs, mod_w, mod_b, tn=1024):
    depth, d, n = mod_w.shape
    r = cs.shape[0]
    return pl.pallas_call(
        _mod_kernel,
        name="modulation",
        grid=(depth, n // tn),
        in_specs=[pl.BlockSpec((r, d), lambda l, j: (0, 0)),
                  pl.BlockSpec((None, d, tn), lambda l, j: (l, 0, j)),
                  pl.BlockSpec((None, 1, tn), lambda l, j: (l, 0, j))],
        out_specs=pl.BlockSpec((None, r, tn), lambda l, j: (l, 0, j)),
        out_shape=jax.ShapeDtypeStruct((depth, r, n), F32),
        compiler_params=_params(2),
    )(cs, mod_w, mod_b.reshape(depth, 1, n))


class _Stage:
    def __init__(self, body, ins, outs=()):
        self.body, self.ins, self.outs = body, list(ins), list(outs)


def _ffn_kernel(*refs, j, fc, pre, post):
    h_ref, mod_ref, gpre_ref, gpost_ref, w13_ref, w2_ref = refs[:6]
    n_pre = len(pre.ins) if pre else 0
    n_post = len(post.ins) if post else 0
    pre_refs = refs[6:6 + n_pre]
    post_refs = refs[6 + n_pre:6 + n_pre + n_post]
    o_ref = refs[6 + n_pre + n_post]
    post_outs = refs[7 + n_pre + n_post:-1]
    acc_ref = refs[-1]
    x = h_ref[...]
    if pre:
        x = pre.body(x, mod_ref, *pre_refs)
    shift = mod_ref[3 * j:3 * j + 1, :]
    scale = mod_ref[3 * j + 1:3 * j + 2, :]
    gate = mod_ref[3 * j + 2:3 * j + 3, :]
    a = (_rms(x, gpre_ref[...]) * (1.0 + scale) + shift).astype(BF16)
    f = w2_ref.shape[0]
    for c in range(f // fc):
        g = jnp.dot(a, w13_ref[:, c * fc:(c + 1) * fc], preferred_element_type=F32)
        u = jnp.dot(a, w13_ref[:, f + c * fc:f + (c + 1) * fc], preferred_element_type=F32)
        act = (jax.nn.silu(g) * u).astype(BF16)
        contrib = jnp.dot(act, w2_ref[c * fc:(c + 1) * fc, :], preferred_element_type=F32)
        if c == 0:
            acc_ref[...] = contrib
        else:
            acc_ref[...] += contrib
    out = x + FFN_RES * gate * _rms(acc_ref[...], gpost_ref[...])
    o_ref[...] = out
    if post:
        post.body(out, mod_ref, *post_refs, *post_outs)


def _ffn(h, mod, j, g_pre, g_post, w13, w2, tm, pre=None, post=None):
    b, s, d = h.shape
    f = w2.shape[0]
    fc = 256 if f % 256 == 0 else f
    tm = min(tm, s)
    pre = pre(b, s, tm) if pre else None
    post = post(b, s, tm) if post else None
    extra = (pre.ins if pre else []) + (post.ins if post else [])
    outs = [(pl.BlockSpec((None, tm, d), lambda bi, i: (bi, i, 0)), jax.ShapeDtypeStruct((b, s, d), F32))]
    outs += post.outs if post else []
    res = pl.pallas_call(
        functools.partial(_ffn_kernel, j=j, fc=fc, pre=pre, post=post),
        name="ffn",
        grid=(b, s // tm),
        in_specs=[pl.BlockSpec((None, tm, d), lambda bi, i: (bi, i, 0)),
                  pl.BlockSpec((None, N_MOD, d), lambda bi, i: (bi, 0, 0)),
                  _const_spec((1, d)), _const_spec((1, d)),
                  _const_spec(w13.shape), _const_spec(w2.shape)] + [e[0] for e in extra],
        out_specs=[o[0] for o in outs],
        out_shape=[o[1] for o in outs],
        scratch_shapes=[pltpu.VMEM((tm, d), F32)],
        compiler_params=_params(2),
    )(h, mod, g_pre.reshape(1, d), g_post.reshape(1, d), w13, w2, *[e[1] for e in extra])
    return res if post else res[0]


_O_CQ = 0
_O_CKV = _O_CQ + MLA_Q_LORA
_O_KPE = _O_CKV + MLA_KV_LORA
_O_KPE_SW = _O_KPE + LANES
_O_QS = _O_KPE_SW + LANES
_O_QS_SW = _O_QS + SWA_HEADS * SWA_HEAD_DIM
_O_KS = _O_QS_SW + SWA_HEADS * SWA_HEAD_DIM
_O_KS_SW = _O_KS + SWA_KV_HEADS * LANES
_W1_COLS = _O_KS_SW + SWA_KV_HEADS * LANES
_HL = MLA_HEADS * LANES
_QS_W = SWA_HEADS * SWA_HEAD_DIM
_KS_W = SWA_KV_HEADS * LANES


def _attn_proj_body(x, mod_ref, gpre_ref, w1_ref, qn_ref, wq_ref, kvn_ref, wk_ref, wvt_ref, wvst_ref,
                    ca_ref, sa_ref, cb_ref, sb_ref,
                    qa_ref, ka_ref, vat_ref, qs_ref, ks_ref, vst_ref):
    shift = mod_ref[3:4, :]
    scale = mod_ref[4:5, :]
    a = (_rms(x, gpre_ref[...]) * (1.0 + scale) + shift).astype(BF16)
    p = jnp.dot(a, w1_ref[...], preferred_element_type=F32)
    ca, sa, cb, sb = ca_ref[...], sa_ref[...], cb_ref[...], sb_ref[...]

    cqn = _rms(p[:, _O_CQ:_O_CQ + MLA_Q_LORA], qn_ref[...]).astype(BF16)
    q2 = jnp.dot(cqn, wq_ref[...], preferred_element_type=F32)
    ckvn = _rms(p[:, _O_CKV:_O_CKV + MLA_KV_LORA], kvn_ref[...]).astype(BF16)
    kn = jnp.dot(ckvn, wk_ref[...], preferred_element_type=F32)
    kpe = p[:, _O_KPE:_O_KPE + LANES] * ca + p[:, _O_KPE_SW:_O_KPE_SW + LANES] * sa
    for hd in range(MLA_HEADS):
        lo, hi = hd * LANES, (hd + 1) * LANES
        q = q2[:, lo:hi] * ca + q2[:, _HL + lo:_HL + hi] * sa
        qa_ref[:, lo:hi] = (q * (MLA_SCALE * math.log2(math.e))).astype(BF16)
        ka_ref[hd] = (kn[:, lo:hi] + kpe).astype(BF16)
    vt = _dot_nt(wvt_ref[...], ckvn)
    row = lax.broadcasted_iota(jnp.int32, (_HL, 1), 0)
    vat_ref[...] = (vt + (jnp.bitwise_and(row, LANES - 1) == MLA_V).astype(F32)).astype(BF16)

    for t in range(_QS_W // LANES):
        lo, hi = t * LANES, (t + 1) * LANES
        q = p[:, _O_QS + lo:_O_QS + hi] * cb + p[:, _O_QS_SW + lo:_O_QS_SW + hi] * sb
        qs_ref[:, lo:hi] = (q * (SWA_SCALE * math.log2(math.e))).astype(BF16)
    for t in range(SWA_KV_HEADS):
        lo, hi = t * LANES, (t + 1) * LANES
        k = p[:, _O_KS + lo:_O_KS + hi] * cb + p[:, _O_KS_SW + lo:_O_KS_SW + hi] * sb
        ks_ref[:, lo:hi] = k.astype(BF16)
    vst = _dot_nt(wvst_ref[...], a)
    row = lax.broadcasted_iota(jnp.int32, (_KS_W, 1), 0)
    vst_ref[...] = (vst + (jnp.bitwise_and(row, LANES - 1) == SWA_HEAD_DIM).astype(F32)).astype(BF16)


def _const_in(a):
    return (_const_spec(a.shape), a)


def _attn_proj_stage(g_pre, w1, q_norm, wq, kv_norm, wk, wvt, wvst, tabs):
    def make(b, s, tm):
        row = lambda bi, i: (bi, i, 0)
        tab = pl.BlockSpec((tm, LANES), lambda bi, i: (i, 0))
        tok = lambda w: (pl.BlockSpec((None, tm, w), row), jax.ShapeDtypeStruct((b, s, w), BF16))
        chan = lambda w: (pl.BlockSpec((None, w, tm), lambda bi, i: (bi, 0, i)),
                          jax.ShapeDtypeStruct((b, w, s), BF16))
        ins = [_const_in(g_pre.reshape(1, -1)), _const_in(w1), _const_in(q_norm.reshape(1, -1)), _const_in(wq),
               _const_in(kv_norm.reshape(1, -1)), _const_in(wk), _const_in(wvt), _const_in(wvst)]
        ins += [(tab, t) for t in tabs]
        heads = (pl.BlockSpec((None, MLA_HEADS, tm, LANES), lambda bi, i: (bi, 0, i, 0)),
                 jax.ShapeDtypeStruct((b, MLA_HEADS, s, LANES), BF16))
        return _Stage(_attn_proj_body, ins,
                      [tok(_HL), heads, chan(_HL), tok(_QS_W), tok(_KS_W), chan(_KS_W)])
    return make


def _mla_kernel(*refs, n_lat_chunks, tk):
    if n_lat_chunks:
        q_ref, kc_ref, vct_ref, kl_ref, vlt_ref, o_ref = refs
    else:
        q_ref, kc_ref, vct_ref, o_ref = refs
    q = q_ref[...]

    def scores(j):
        k = kc_ref[...] if j == 0 else kl_ref[(j - 1) * tk:j * tk, :]
        return _dot_nt(k, q)

    def update(j, st, m, acc):
        vt = vct_ref[...] if j == 0 else vlt_ref[:, (j - 1) * tk:j * tk]
        m_blk = jnp.max(st, axis=0, keepdims=True)
        m_new = m_blk if m is None else jnp.maximum(m, m_blk)
        pt = jnp.exp2(st - m_new).astype(BF16)
        pv = jnp.dot(vt, pt, preferred_element_type=F32)
        acc = pv if acc is None else acc * jnp.exp2(m - m_new) + pv
        return m_new, acc

    m, acc = None, None
    st_next = scores(0)
    for j in range(n_lat_chunks + 1):
        st = st_next
        if j < n_lat_chunks:
            st_next = scores(j + 1)
        m, acc = update(j, st, m, acc)
    o_ref[...] = (acc / acc[MLA_V:MLA_V + 1, :]).T.astype(o_ref.dtype)


def _mla(q, kc, vct, kl, vlt, tq, tk):
    b, s, _ = q.shape
    n_ctx = kc.shape[2]
    tq = min(tq, s)
    qspec = pl.BlockSpec((None, tq, LANES), lambda bi, h, i: (bi, i, h))
    in_specs = [qspec,
                pl.BlockSpec((None, None, n_ctx, LANES), lambda bi, h, i: (bi, h, 0, 0)),
                pl.BlockSpec((None, LANES, n_ctx), lambda bi, h, i: (bi, h, 0))]
    args = [q, kc, vct]
    n_lat_chunks = 0
    if kl is not None:
        n_lat = kl.shape[2]
        tk = min(tk, n_lat)
        n_lat_chunks = n_lat // tk
        in_specs += [pl.BlockSpec((None, None, n_lat, LANES), lambda bi, h, i: (bi, h, 0, 0)),
                     pl.BlockSpec((None, LANES, n_lat), lambda bi, h, i: (bi, h, 0))]
        args += [kl, vlt]
    return pl.pallas_call(
        functools.partial(_mla_kernel, n_lat_chunks=n_lat_chunks, tk=tk),
        name="mla_lat" if n_lat_chunks else "mla_ctx",
        grid=(b, MLA_HEADS, s // tq),
        in_specs=in_specs,
        out_specs=qspec,
        out_shape=jax.ShapeDtypeStruct((b, s, _HL), BF16),
        compiler_params=_params(3),
    )(*args)


def _swa_kernel(*refs, windowed, n_blocks):
    if windowed:
        (sink_ref, q_ref, kc_ref, vct_ref, kp_ref, k0_ref, kn_ref,
         vpt_ref, v0t_ref, vnt_ref, o_ref) = refs
    else:
        sink_ref, q_ref, kc_ref, vct_ref, o_ref = refs
    i = pl.program_id(1)
    tq = q_ref.shape[0]
    cols = SWA_GROUP * tq
    lane = lax.broadcasted_iota(jnp.int32, (1, LANES), 1)
    half_mask = [(lane < SWA_HEAD_DIM), (lane >= SWA_HEAD_DIM)]
    if windowed:
        w = SWA_WINDOW
        j = lax.broadcasted_iota(jnp.int32, (tq + 2 * w, cols), 0)
        r = jnp.bitwise_and(lax.broadcasted_iota(jnp.int32, (tq + 2 * w, cols), 1), tq - 1)
        j_lo = jnp.where(i > 0, r, jnp.maximum(r, w))
        j_hi = jnp.where(i < n_blocks - 1, r + 2 * w, jnp.minimum(r + 2 * w, w + tq - 1))
    staged = []
    for kv in range(SWA_KV_HEADS):
        lo, hi = kv * LANES, (kv + 1) * LANES
        qs, sk = [], []
        for g in range(SWA_GROUP):
            hd = kv * SWA_GROUP + g
            t = hd // 2
            qt = q_ref[:, t * LANES:(t + 1) * LANES]
            qs.append(jnp.where(half_mask[hd % 2], qt, jnp.zeros_like(qt)))
            sk.append(jnp.full((1, tq), sink_ref[hd] * math.log2(math.e), F32))
        q4 = jnp.concatenate(qs, axis=0)
        scores = [_dot_nt(kc_ref[:, lo:hi], q4)]
        if windowed:
            kw = jnp.concatenate([kp_ref[:, lo:hi], k0_ref[:, lo:hi], kn_ref[:, lo:hi]], axis=0)
            scores.append(_dot_nt(kw, q4))
        staged.append((scores, jnp.concatenate(sk, axis=1)))
    for kv in range(SWA_KV_HEADS):
        lo, hi = kv * LANES, (kv + 1) * LANES
        scores, sink = staged[kv]
        vals = [vct_ref[lo:hi, :]]
        if windowed:
            scores[1] = jnp.where(j >= j_lo, jnp.where(j <= j_hi, scores[1], NEG_INF), NEG_INF)
            vals.append(jnp.concatenate([vpt_ref[lo:hi, :], v0t_ref[lo:hi, :], vnt_ref[lo:hi, :]], axis=1))
        m = sink
        for s_ in scores:
            m = jnp.maximum(m, jnp.max(s_, axis=0, keepdims=True))
        acc = None
        for s_, vt in zip(scores, vals):
            pv = jnp.dot(vt, jnp.exp2(s_ - m).astype(BF16), preferred_element_type=F32)
            acc = pv if acc is None else acc + pv
        denom = acc[SWA_HEAD_DIM:SWA_HEAD_DIM + 1, :] + jnp.exp2(sink - m)
        o = acc / denom
        for g in range(SWA_GROUP):
            hd = kv * SWA_GROUP + g
            o_ref[:, hd * LANES:(hd + 1) * LANES] = o[:, g * tq:(g + 1) * tq].T.astype(o_ref.dtype)


def _swa(sink, q, kc, vct, kl, vlt):
    b, s, _ = q.shape
    n_ctx = kc.shape[1]
    windowed = kl is not None
    tq = next(n * Q_BLOCK for n in (4, 2, 1) if s % (n * Q_BLOCK) == 0) if windowed else s
    nb = s // tq
    w = SWA_WINDOW
    per = tq // w
    in_specs = [pl.BlockSpec(memory_space=pltpu.SMEM),
                pl.BlockSpec((None, tq, _QS_W), lambda bi, i: (bi, i, 0)),
                pl.BlockSpec((None, n_ctx, _KS_W), lambda bi, i: (bi, 0, 0)),
                pl.BlockSpec((None, _KS_W, n_ctx), lambda bi, i: (bi, 0, 0))]
    args = [sink, q, kc, vct]
    if windowed:
        blocks = ((w, lambda i: jnp.maximum(i * per - 1, 0)), (tq, lambda i: i),
                  (w, lambda i: jnp.minimum((i + 1) * per, s // w - 1)))
        in_specs += [pl.BlockSpec((None, n, _KS_W), lambda bi, i, f=f: (bi, f(i), 0)) for n, f in blocks]
        in_specs += [pl.BlockSpec((None, _KS_W, n), lambda bi, i, f=f: (bi, 0, f(i))) for n, f in blocks]
        args += [kl, kl, kl, vlt, vlt, vlt]
    return pl.pallas_call(
        functools.partial(_swa_kernel, windowed=windowed, n_blocks=nb),
        name="swa_lat" if windowed else "swa_ctx",
        grid=(b, nb),
        in_specs=in_specs,
        out_specs=pl.BlockSpec((None, tq, SWA_HEADS * LANES), lambda bi, i: (bi, i, 0)),
        out_shape=jax.ShapeDtypeStruct((b, s, SWA_HEADS * LANES), BF16),
        compiler_params=_params(2),
    )(*args)


def _attn_out_body(x, mod_ref, oa_ref, ob_ref, gpost_ref, wa_ref, wb_ref):
    y = jnp.dot(oa_ref[...], wa_ref[...], preferred_element_type=F32)
    y = y + jnp.dot(ob_ref[...], wb_ref[...], preferred_element_type=F32)
    return x + mod_ref[5:6, :] * _rms(y, gpost_ref[...])


def _attn_out_stage(oa, ob, g_post, wa, wb):
    def make(b, s, tm):
        row = lambda bi, i: (bi, i, 0)
        return _Stage(_attn_out_body,
                      [(pl.BlockSpec((None, tm, oa.shape[2]), row), oa),
                       (pl.BlockSpec((None, tm, ob.shape[2]), row), ob),
                       _const_in(g_post.reshape(1, -1)), _const_in(wa), _const_in(wb)])
    return make


def _s5_in_body(x, mod_ref, gpre_ref, wt_ref, o_ref):
    a = (_rms(x, gpre_ref[...]) * (1.0 + mod_ref[4:5, :]) + mod_ref[3:4, :]).astype(BF16)
    o_ref[...] = _dot_nt(wt_ref[...], a)


def _s5_in_stage(g_pre, w_in_t):
    def make(b, s, tm):
        w = w_in_t.shape[0]
        return _Stage(_s5_in_body, [_const_in(g_pre.reshape(1, -1)), _const_in(w_in_t)],
                      [(pl.BlockSpec((None, w, tm), lambda bi, i: (bi, 0, i)),
                        jax.ShapeDtypeStruct((b, w, s), F32))])
    return make


def _s5_core_kernel(u_ref, uc_ref, kk_ref, et_ref, ft_ref, ap_ref, dsk_ref, o_ref,
                    er_ref, ei_ref, xrf_ref, xrb_ref, xif_ref, xib_ref, m_ref, *, nb, nc, ncc):
    h = S5_GROUP
    rows = nb * nc

    mi = lax.broadcasted_iota(jnp.int32, (S5_CHUNK, S5_CHUNK), 0)
    li = lax.broadcasted_iota(jnp.int32, (S5_CHUNK, S5_CHUNK), 1)
    causal = li >= mi

    def build(i, j):
        lags = kk_ref[i, j:j + 1, :]
        fwd = pltpu.roll(jnp.broadcast_to(lags[:, S5_CHUNK:], (S5_CHUNK, S5_CHUNK)), 0, 1,
                         stride=1, stride_axis=0)
        bwd = pltpu.roll(jnp.broadcast_to(lags[:, :S5_CHUNK], (S5_CHUNK, S5_CHUNK)), 0, 1,
                         stride=1, stride_axis=0)
        m_ref[i * S5_CHUNK:(i + 1) * S5_CHUNK, j * S5_CHUNK:(j + 1) * S5_CHUNK] = (
            jnp.where(causal, fwd, bwd).astype(BF16))

    u32 = jnp.concatenate([u_ref[:, i].reshape(rows, S5_CHUNK) for i in range(h)], axis=1)
    ub = u32.astype(BF16)
    et = et_ref[...]
    ein = jnp.dot(ub, et, preferred_element_type=F32)
    er_ref[...] = ein[:, :LANES]
    ei_ref[...] = ein[:, LANES:]
    ar, ai = ap_ref[0:1, :], ap_ref[1:2, :]
    fwd_lane = lax.broadcasted_iota(jnp.int32, (1, LANES), 1) < S5_STATE

    def step(xr, xi, e_r, e_i):
        return xr * ar - xi * ai + e_r, xr * ai + xi * ar + e_i

    xr = jnp.zeros((nb, LANES), F32)
    xi = jnp.zeros((nb, LANES), F32)
    for c in range(ncc):
        ecf = jnp.dot(uc_ref[c].astype(BF16), et, preferred_element_type=F32)
        ecb = jnp.dot(uc_ref[ncc - 1 - c].astype(BF16), et, preferred_element_type=F32)
        xr, xi = step(xr, xi, jnp.where(fwd_lane, ecf[:, :LANES], ecb[:, :LANES]),
                      jnp.where(fwd_lane, ecf[:, LANES:], ecb[:, LANES:]))

    for t in range(nc):
        cf = pl.ds(t, nb, stride=nc)
        cb = pl.ds(nc - 1 - t, nb, stride=nc)
        xrf_ref[cf, :] = xr
        xrb_ref[cb, :] = xr
        xif_ref[cf, :] = xi
        xib_ref[cb, :] = xi
        xr, xi = step(xr, xi, jnp.where(fwd_lane, er_ref[cf, :], er_ref[cb, :]),
                      jnp.where(fwd_lane, ei_ref[cf, :], ei_ref[cb, :]))
    xs = jnp.concatenate([jnp.where(fwd_lane, xrf_ref[...], xrb_ref[...]),
                          jnp.where(fwd_lane, xif_ref[...], xib_ref[...])], axis=1).astype(BF16)

    cw = 2 * S5_CHUNK
    for jp in range(h // 2):
        for i in range(h):
            build(i, 2 * jp)
            build(i, 2 * jp + 1)
        cs = slice(jp * cw, (jp + 1) * cw)
        y = jnp.dot(ub, m_ref[:, cs], preferred_element_type=F32)
        y = y + jnp.dot(xs, ft_ref[:, cs], preferred_element_type=F32)
        y = y + u32[:, cs] * dsk_ref[:, cs]
        o_ref[:, 2 * jp] = y[:, :S5_CHUNK].reshape(nb, nc, S5_CHUNK)
        o_ref[:, 2 * jp + 1] = y[:, S5_CHUNK:].reshape(nb, nc, S5_CHUNK)


def _s5_core(u_t, uc, kk, et, ft, ap, dsk):
    b, w, s = u_t.shape
    g = w // S5_GROUP
    nc = s // S5_CHUNK
    ncc = uc.shape[1]
    hl = S5_GROUP * S5_CHUNK
    u5 = u_t.reshape(b, g, S5_GROUP, nc, S5_CHUNK)
    blk = pl.BlockSpec((b, None, S5_GROUP, nc, S5_CHUNK), lambda gi: (0, gi, 0, 0, 0))
    per_g = lambda shape: pl.BlockSpec((None,) + shape, lambda gi: (gi,) + (0,) * len(shape))
    out = pl.pallas_call(
        functools.partial(_s5_core_kernel, nb=b, nc=nc, ncc=ncc),
        name="s5_core",
        grid=(g,),
        in_specs=[blk, per_g((ncc, b, hl)),
                  per_g((S5_GROUP, S5_GROUP, 2 * S5_CHUNK)),
                  per_g((hl, 2 * LANES)), per_g((2 * LANES, hl)),
                  per_g((2, LANES)), per_g((1, hl))],
        out_specs=blk,
        out_shape=jax.ShapeDtypeStruct(u5.shape, F32),
        scratch_shapes=[pltpu.VMEM((b * nc, LANES), F32) for _ in range(6)]
        + [pltpu.VMEM((hl, hl), BF16)],
        compiler_params=_params(1),
    )(u5, uc, kk, et, ft, ap, dsk)
    return out.reshape(b, w, s)


def _s5_out_body(x, mod_ref, y_ref, gpost_ref, wt_ref):
    d = x.shape[1]
    gy = jax.nn.gelu(y_ref[...]).astype(BF16)
    z = jnp.dot(wt_ref[...], gy, preferred_element_type=F32)
    v = (z[:d, :] * jax.nn.sigmoid(z[d:, :])).T
    return x + mod_ref[5:6, :] * _rms(v, gpost_ref[...])


def _s5_out_stage(y_t, g_post, w_glu_t):
    def make(b, s, tm):
        w = y_t.shape[1]
        return _Stage(_s5_out_body,
                      [(pl.BlockSpec((None, w, tm), lambda bi, i: (bi, 0, i)), y_t),
                       _const_in(g_post.reshape(1, -1)), _const_in(w_glu_t)])
    return make


def _rope_angles(n, d_rot):
    d_axis = d_rot // 2
    inv = ROPE_BASE ** (-jnp.arange(0, d_axis, 2, dtype=F32) / d_axis)
    n_rows = n // GRID_W
    row_ang = jnp.arange(n_rows, dtype=F32)[:, None] * inv
    col_ang = jnp.arange(GRID_W, dtype=F32)[:, None] * inv

    def grid(fr, fc):
        r = jnp.broadcast_to(fr[:, None, :], (n_rows, GRID_W, fr.shape[1]))
        c = jnp.broadcast_to(fc[None, :, :], (n_rows, GRID_W, fc.shape[1]))
        return jnp.concatenate([r, c], axis=-1).reshape(n, -1)

    return grid(jnp.cos(row_ang), jnp.cos(col_ang)), grid(jnp.sin(row_ang), jnp.sin(col_ang))


def _rope_tables(n, rotate):
    pad = LANES - MLA_NOPE - MLA_ROPE
    if rotate:
        ca, sa = _rope_angles(n, MLA_ROPE)
        cb, sb = _rope_angles(n, SWA_HEAD_DIM)
    else:
        ca, sa = jnp.ones((n, MLA_ROPE // 2), F32), jnp.zeros((n, MLA_ROPE // 2), F32)
        cb, sb = jnp.ones((n, SWA_HEAD_DIM // 2), F32), jnp.zeros((n, SWA_HEAD_DIM // 2), F32)
    one, zero = jnp.ones((n, MLA_NOPE), F32), jnp.zeros((n, MLA_NOPE), F32)
    zpad = jnp.zeros((n, pad), F32)
    return (jnp.concatenate([one, ca, ca, zpad], axis=1), jnp.concatenate([zero, sa, sa, zpad], axis=1),
            jnp.concatenate([cb] * 4, axis=1), jnp.concatenate([sb] * 4, axis=1))


def _rot_partner(w, half):
    return jnp.concatenate([-w[..., half:], w[..., :half]], axis=-1)


def _attn_weights(w_in, w_uq, w_ukv, w_out):
    d = w_in.shape[0]
    sizes = [MLA_Q_LORA, MLA_KV_LORA, MLA_ROPE, SWA_HEADS * SWA_HEAD_DIM,
             SWA_KV_HEADS * SWA_HEAD_DIM, SWA_KV_HEADS * SWA_HEAD_DIM]
    cq, ckv, kpe, qs, ks, vs = jnp.split(w_in, [int(v) for v in np.cumsum(sizes)[:-1]], axis=1)
    pad_a = LANES - MLA_NOPE - MLA_ROPE
    z = lambda *shape: jnp.zeros(shape, F32)
    kpe_blk = jnp.concatenate([z(d, MLA_NOPE), kpe, z(d, pad_a)], axis=1)
    kpe_sw = jnp.concatenate([z(d, MLA_NOPE), _rot_partner(kpe, MLA_ROPE // 2), z(d, pad_a)], axis=1)
    qs3 = qs.reshape(d, SWA_HEADS, SWA_HEAD_DIM)
    qs_sw = _rot_partner(qs3, SWA_HEAD_DIM // 2).reshape(d, -1)
    ks3 = ks.reshape(d, SWA_KV_HEADS, SWA_HEAD_DIM)
    ks_sw3 = _rot_partner(ks3, SWA_HEAD_DIM // 2)
    dup = lambda t: jnp.concatenate([t, t], axis=-1).reshape(d, -1)
    vs3 = vs.reshape(d, SWA_KV_HEADS, SWA_HEAD_DIM)
    vs_pad = jnp.concatenate([vs3, jnp.zeros_like(vs3)], axis=-1).reshape(d, -1)
    w1 = jnp.concatenate([cq, ckv, kpe_blk, kpe_sw, qs, qs_sw, dup(ks3), dup(ks_sw3)], axis=1)
    wvst = vs_pad.T
    assert w1.shape[1] == _W1_COLS

    ql = w_uq.shape[0]
    uq = w_uq.reshape(ql, MLA_HEADS, MLA_NOPE + MLA_ROPE)
    nope, pe = uq[..., :MLA_NOPE], uq[..., MLA_NOPE:]
    zq = jnp.zeros((ql, MLA_HEADS, pad_a), F32)
    wq_blk = jnp.concatenate([nope, pe, zq], axis=-1).reshape(ql, -1)
    wq_sw = jnp.concatenate([jnp.zeros_like(nope), _rot_partner(pe, MLA_ROPE // 2), zq], axis=-1).reshape(ql, -1)
    wq = jnp.concatenate([wq_blk, wq_sw], axis=1)

    kl = w_ukv.shape[0]
    ukv = w_ukv.reshape(kl, MLA_HEADS, MLA_NOPE + MLA_V)
    kn, vv = ukv[..., :MLA_NOPE], ukv[..., MLA_NOPE:]
    wk = jnp.concatenate([kn, jnp.zeros((kl, MLA_HEADS, LANES - MLA_NOPE), F32)], axis=-1).reshape(kl, -1)
    wv = jnp.concatenate([vv, jnp.zeros((kl, MLA_HEADS, LANES - MLA_V), F32)], axis=-1).reshape(kl, -1)
    wvt = wv.T

    dm = w_out.shape[1]
    na = MLA_HEADS * MLA_V
    oa = w_out[:na].reshape(MLA_HEADS, MLA_V, dm)
    ob = w_out[na:].reshape(SWA_HEADS, SWA_HEAD_DIM, dm)
    wa = jnp.concatenate([oa, jnp.zeros((MLA_HEADS, LANES - MLA_V, dm), F32)], axis=1).reshape(-1, dm)
    wb = jnp.concatenate([ob, jnp.zeros((SWA_HEADS, LANES - SWA_HEAD_DIM, dm), F32)], axis=1).reshape(-1, dm)
    return tuple(t.astype(BF16) for t in (w1, wq, wk, wvt, wvst, wa, wb))


def _complex_powers(a_re, a_im, n):
    pr, pi = jnp.ones_like(a_re)[None], jnp.zeros_like(a_im)[None]
    sr, si = a_re, a_im
    while pr.shape[0] < n + 1:
        nr, ni = pr * sr - pi * si, pr * si + pi * sr
        pr, pi = jnp.concatenate([pr, nr], 0), jnp.concatenate([pi, ni], 0)
        sr, si = sr * sr - si * si, 2.0 * sr * si
    return pr[:n + 1], pi[:n + 1]


def _s5_operators(lam_re, lam_im, b_re, b_im, c_re, c_im, log_step):
    hi = lax.Precision.HIGHEST
    ln = S5_CHUNK
    g, p = lam_re.shape[1:]
    hh = b_re.shape[-1]
    ks, ets, fts, aps = [], [], [], []
    for dr in range(2):
        lre = jnp.minimum(lam_re[dr], S5_MAX_RE)
        lim = lam_im[dr]
        dt = jnp.exp(log_step[dr])[:, None]
        mag = jnp.exp(lre * dt)
        a_re, a_im = mag * jnp.cos(lim * dt), mag * jnp.sin(lim * dt)
        den = lre * lre + lim * lim
        f_re = ((a_re - 1.0) * lre + a_im * lim) / den
        f_im = (a_im * lre - (a_re - 1.0) * lim) / den
        bb_re = f_re[..., None] * b_re[dr] - f_im[..., None] * b_im[dr]
        bb_im = f_re[..., None] * b_im[dr] + f_im[..., None] * b_re[dr]
        cr, ci = c_re[dr], c_im[dr]
        pr, pi = _complex_powers(a_re, a_im, ln)
        cb_re = jnp.einsum('gjp,gpi->gpji', cr, bb_re) - jnp.einsum('gjp,gpi->gpji', ci, bb_im)
        cb_im = jnp.einsum('gjp,gpi->gpji', cr, bb_im) + jnp.einsum('gjp,gpi->gpji', ci, bb_re)
        k = (jnp.einsum('dgp,gpji->gijd', pr[:ln], cb_re, precision=hi)
             - jnp.einsum('dgp,gpji->gijd', pi[:ln], cb_im, precision=hi))
        ks.append(k)
        er, ei = (pr[:ln][::-1], pi[:ln][::-1]) if dr == 0 else (pr[:ln], pi[:ln])
        e_re = jnp.einsum('mgp,gpi->gimp', er, bb_re) - jnp.einsum('mgp,gpi->gimp', ei, bb_im)
        e_im = jnp.einsum('mgp,gpi->gimp', er, bb_im) + jnp.einsum('mgp,gpi->gimp', ei, bb_re)
        ets.append((e_re.reshape(g, hh * ln, p), e_im.reshape(g, hh * ln, p)))
        fr, fi = (pr[1:], pi[1:]) if dr == 0 else (pr[1:][::-1], pi[1:][::-1])
        f_xre = jnp.einsum('gjp,lgp->gpjl', cr, fr) - jnp.einsum('gjp,lgp->gpjl', ci, fi)
        f_xim = -(jnp.einsum('gjp,lgp->gpjl', cr, fi) + jnp.einsum('gjp,lgp->gpjl', ci, fr))
        fts.append((f_xre.reshape(g, p, hh * ln), f_xim.reshape(g, p, hh * ln)))
        aps.append((pr[ln], pi[ln]))
    kf, kb = ks
    k0 = kf[..., :1] + kb[..., :1]
    kk = jnp.concatenate([jnp.zeros_like(k0), kb[..., :0:-1], k0, kf[..., 1:]], axis=-1)
    et = jnp.concatenate([ets[0][0], ets[1][0], ets[0][1], ets[1][1]], axis=-1).astype(BF16)
    ft = jnp.concatenate([fts[0][0], fts[1][0], fts[0][1], fts[1][1]], axis=1).astype(BF16)
    ap = jnp.stack([jnp.concatenate([aps[0][0], aps[1][0]], axis=-1),
                    jnp.concatenate([aps[0][1], aps[1][1]], axis=-1)], axis=1)
    return kk, et, ft, ap


def kernel(x, c, ctx, c_ctx, mod_w, mod_b, norm_pre, norm_post, ffn_w13, ffn_w2,
           attn_w_in, mla_q_norm, mla_w_uq, mla_kv_norm, mla_w_ukv, swa_sink, attn_w_out,
           s5_w_in, s5_lambda_re, s5_lambda_im, s5_b_re, s5_b_im, s5_c_re, s5_c_im,
           s5_log_step, s5_d, s5_w_glu):
    b, n, d = x.shape
    n_ctx = ctx.shape[1]
    depth = mod_w.shape[0]
    tm = 512

    rows = -(-(b + 1) // 8) * 8
    cs = jnp.concatenate([c, c_ctx[None], jnp.zeros((rows - b - 1, d), F32)], axis=0)
    mods = _modulation(cs, mod_w, mod_b)
    w13 = ffn_w13.astype(BF16)
    w2 = ffn_w2.astype(BF16)

    h_lat, h_ctx = x, ctx
    for l in range(depth):
        last = l == depth - 1
        m_lat = mods[l, :b].reshape(b, N_MOD, d)
        m_ctx = jnp.broadcast_to(mods[l, b].reshape(1, N_MOD, d), (b, N_MOD, d))
        ffn1 = functools.partial(_ffn, j=0, g_pre=norm_pre[l, 0], g_post=norm_post[l, 0],
                                 w13=w13[l, 0], w2=w2[l, 0], tm=tm)
        ffn2 = functools.partial(_ffn, j=2, g_pre=norm_pre[l, 2], g_post=norm_post[l, 2],
                                 w13=w13[l, 1], w2=w2[l, 1], tm=tm)
        if l % 2 == 0:
            e = l // 2
            w1, wq, wk, wvt, wvst, wa, wb = _attn_weights(attn_w_in[e], mla_w_uq[e], mla_w_ukv[e], attn_w_out[e])
            proj = functools.partial(_attn_proj_stage, norm_pre[l, 1], w1, mla_q_norm[e], wq,
                                     mla_kv_norm[e], wk, wvt, wvst)
            h_ctx, qa_c, ka_c, va_c, qs_c, ks_c, vs_c = ffn1(h_ctx, m_ctx, post=proj(_rope_tables(n_ctx, False)))
            h_lat, qa_l, ka_l, va_l, qs_l, ks_l, vs_l = ffn1(h_lat, m_lat, post=proj(_rope_tables(n, True)))
            o_a = _mla(qa_l, ka_c, va_c, ka_l, va_l, tq=2048, tk=512)
            o_b = _swa(swa_sink[e], qs_l, ks_c, vs_c, ks_l, vs_l)
            h_lat = ffn2(h_lat, m_lat, pre=_attn_out_stage(o_a, o_b, norm_post[l, 1], wa, wb))
            if not last:
                o_a_c = _mla(qa_c, ka_c, va_c, None, None, tq=1024, tk=512)
                o_b_c = _swa(swa_sink[e], qs_c, ks_c, vs_c, None, None)
                h_ctx = ffn2(h_ctx, m_ctx, pre=_attn_out_stage(o_a_c, o_b_c, norm_post[l, 1], wa, wb))
        else:
            o = l // 2
            assert last, "S5 context outputs are only needed when another layer follows"
            w_in_t = s5_w_in[o].T.astype(BF16)
            w_glu_t = s5_w_glu[o].T.astype(BF16)
            kk, et, ft, ap = _s5_operators(s5_lambda_re[o], s5_lambda_im[o], s5_b_re[o], s5_b_im[o],
                                             s5_c_re[o], s5_c_im[o], s5_log_step[o])
            width = s5_w_in.shape[2]
            g = width // S5_GROUP
            s5_in = _s5_in_stage(norm_pre[l, 1], w_in_t)
            h_lat, u_lat = ffn1(h_lat, m_lat, post=s5_in)
            h_ctx, u_ctx = ffn1(h_ctx, m_ctx, post=s5_in)
            ncc = n_ctx // S5_CHUNK
            uc = u_ctx.reshape(b, g, S5_GROUP, ncc, S5_CHUNK).transpose(1, 3, 0, 2, 4)
            uc = uc.reshape(g, ncc, b, S5_GROUP * S5_CHUNK)
            dsk = jnp.repeat(s5_d[o].astype(F32), S5_CHUNK).reshape(g, 1, S5_GROUP * S5_CHUNK)
            y_t = _s5_core(u_lat, uc, kk, et, ft, ap, dsk)
            h_lat = ffn2(h_lat, m_lat, pre=_s5_out_stage(y_t, norm_post[l, 1], w_glu_t))
    return h_lat
```
